```python
import math, functools
import jax, jax.numpy as jnp
from jax import lax
import numpy as np

D_MODEL = 1024
BATCH = 8
SEQ = 2048
DEPTH = 2

GRID_W = 64
CTX_LEN = 256
N_EVEN = (DEPTH + 1) // 2
N_ODD = DEPTH // 2
D_FF = 2816
FFN_RES = 0.5
N_MOD = 9
EPS = 1e-6
ROPE_THETA = 10000.0
Q_BLOCK = 128
CHUNK = 64

MLA_HEADS = 8
MLA_NOPE = 64
MLA_ROPE = 32
MLA_QK = MLA_NOPE + MLA_ROPE
MLA_V = 64
MLA_Q_RANK = 384
MLA_KV_RANK = 256
MLA_VW = MLA_HEADS * MLA_V

GDN_HEADS = 8
GDN_DK = 64
GDN_DV = 64
GDN_CONV = 5
GDN_QK = GDN_HEADS * GDN_DK
GDN_VW = GDN_HEADS * GDN_DV

GLA_HEADS = 4
GLA_DK = 64
GLA_DV = 128
GLA_RANK = 16
GLA_TAU = 16.0
GLA_QK = GLA_HEADS * GLA_DK
GLA_VW = GLA_HEADS * GLA_DV

SWA_HEADS = 8
SWA_KV_HEADS = 2
SWA_DH = 64
SWA_WINDOW = 128
SWA_QW = SWA_HEADS * SWA_DH
SWA_KW = SWA_KV_HEADS * SWA_DH

EVEN_SPLITS = (MLA_Q_RANK, MLA_KV_RANK, MLA_ROPE, GDN_QK, GDN_QK, GDN_VW,
               GDN_HEADS, GDN_HEADS, GDN_HEADS, GDN_HEADS, GDN_VW)
ODD_SPLITS = (GLA_QK, GLA_QK, GLA_VW, GLA_RANK, GLA_RANK, GLA_VW, SWA_QW, SWA_KW, SWA_KW)
EVEN_IN = sum(EVEN_SPLITS)
ODD_IN = sum(ODD_SPLITS)
MIX_EVEN = MLA_VW + GDN_VW
MIX_ODD = GLA_VW + SWA_QW

F32 = jnp.float32

kernel_name = 'hybrid_mla_gdn_gla_swa_prefix_block'


def rms_norm(x, g):
    x32 = x.astype(F32)
    y = x32 * lax.rsqrt(jnp.mean(x32 * x32, axis=-1, keepdims=True) + EPS)
    return (y * g.astype(F32)).astype(x.dtype)


def l2_norm(x):
    x32 = x.astype(F32)
    return (x32 * lax.rsqrt(jnp.sum(x32 * x32, axis=-1, keepdims=True) + EPS)).astype(x.dtype)


def modulate(h, shift, scale):
    return h * (1 + scale) + shift


def split_cols(z, sizes):
    return jnp.split(z, [int(s) for s in np.cumsum(sizes)[:-1]], axis=-1)


def swiglu(h, w_gate, w_up, w_down):
    return (jax.nn.silu(h @ w_gate) * (h @ w_up)) @ w_down


def ffn_half_step(x, g, shift, scale, gate, w_gate, w_up, w_down):
    h = modulate(rms_norm(x, g), shift, scale)
    return x + FFN_RES * gate * swiglu(h, w_gate, w_up, w_down)


def axial_rope_tables(rows, rot_dim):
    t = jnp.arange(rows * GRID_W)
    row = (t // GRID_W).astype(F32)
    col = (t % GRID_W).astype(F32)
    n_freq = rot_dim // 4
    inv = ROPE_THETA ** (-jnp.arange(n_freq, dtype=F32) / n_freq)
    ang = jnp.concatenate([row[:, None] * inv, col[:, None] * inv], axis=-1)
    return jnp.cos(ang), jnp.sin(ang)


def apply_rope(x, cos, sin):
    x1, x2 = jnp.split(x.astype(F32), 2, axis=-1)
    c, s = cos[None, :, None, :], sin[None, :, None, :]
    return jnp.concatenate([x1 * c - x2 * s, x1 * s + x2 * c], axis=-1).astype(x.dtype)


def centred_depthwise_conv(x, w):
    k_w, ch = w.shape
    return lax.conv_general_dilated(x, w[:, None, :].astype(x.dtype), window_strides=(1,),
                                    padding=[(k_w // 2, k_w // 2)],
                                    dimension_numbers=('NWC', 'WIO', 'NWC'),
                                    feature_group_count=ch)


def dense_attention(q, k, v, scale):
    b, n, h, dq = q.shape
    nb = n // Q_BLOCK
    qb = jnp.moveaxis(q.reshape(b, nb, Q_BLOCK, h, dq), 1, 0)

    def block(qi):
        s = jnp.einsum('bqhd,bkhd->bhqk', qi, k, preferred_element_type=F32) * scale
        p = jax.nn.softmax(s, axis=-1).astype(v.dtype)
        return jnp.einsum('bhqk,bkhd->bqhd', p, v, preferred_element_type=F32).astype(q.dtype)

    o = lax.map(block, qb)
    return jnp.moveaxis(o, 0, 1).reshape(b, n, h, v.shape[-1])


def sink_attention(q, k, v, sink, scale):
    b, n, h, dh = q.shape
    g = k.shape[2]
    r = h // g
    qg = q.reshape(b, n, g, r, dh)
    s = jnp.einsum('bqgrd,bkgd->bgrqk', qg, k, preferred_element_type=F32) * scale
    s_sink = jnp.broadcast_to(sink.astype(F32).reshape(1, g, r, 1, 1), s.shape[:-1] + (1,))
    p = jax.nn.softmax(jnp.concatenate([s, s_sink], axis=-1), axis=-1)[..., :-1].astype(v.dtype)
    o = jnp.einsum('bgrqk,bkgd->bqgrd', p, v, preferred_element_type=F32)
    return o.reshape(b, n, h, dh).astype(q.dtype)


def banded_window_attention(q, k, v, k_ctx, v_ctx, sink, scale):
    b, n, h, dh = q.shape
    g = k.shape[2]
    r = h // g
    w = SWA_WINDOW
    nb = n // w
    qb = q.reshape(b, nb, w, g, r, dh)

    def band(t):
        tp = jnp.pad(t, ((0, 0), (w, w), (0, 0), (0, 0))).reshape(b, nb + 2, w, g, dh)
        return jnp.concatenate([tp[:, :-2], tp[:, 1:-1], tp[:, 2:]], axis=2)

    kb, vb = band(k), band(v)
    s_band = jnp.einsum('bnqgrd,bnkgd->bngrqk', qb, kb, preferred_element_type=F32) * scale
    s_ctx = jnp.einsum('bnqgrd,bcgd->bngrqc', qb, k_ctx, preferred_element_type=F32) * scale
    qi = jnp.arange(w)[:, None]
    kr = jnp.arange(3 * w)[None, :]
    kpos = jnp.arange(nb)[:, None, None] * w - w + kr
    rel = kr - qi
    valid = (rel >= 0) & (rel <= 2 * w) & (kpos >= 0) & (kpos < n)
    s_band = jnp.where(valid[None, :, None, None], s_band, -jnp.inf)
    s_sink = jnp.broadcast_to(sink.astype(F32).reshape(1, 1, g, r, 1, 1), s_band.shape[:-1] + (1,))
    p = jax.nn.softmax(jnp.concatenate([s_band, s_ctx, s_sink], axis=-1), axis=-1).astype(v.dtype)
    n_band = 3 * w
    o = (jnp.einsum('bngrqk,bnkgd->bnqgrd', p[..., :n_band], vb, preferred_element_type=F32)
         + jnp.einsum('bngrqc,bcgd->bnqgrd', p[..., n_band:n_band + k_ctx.shape[1]], v_ctx,
                      preferred_element_type=F32))
    return o.reshape(b, n, h, dh).astype(q.dtype)


def to_chunks(t):
    b, n, h = t.shape[:3]
    t = t.reshape(b, n // CHUNK, CHUNK, h, *t.shape[3:])
    return jnp.moveaxis(t, (1, 3), (0, 2))


def from_chunks(t):
    t = jnp.moveaxis(t, (0, 2), (1, 3))
    return t.reshape(t.shape[0], t.shape[1] * t.shape[2], t.shape[3], t.shape[4])


def gated_delta_chunked(q, k, v, g, beta, s0):
    dt = v.dtype
    q, k, v, g, beta = (to_chunks(t.astype(F32)) for t in (q, k, v, g, beta))
    gc = jnp.cumsum(g, axis=-1)
    idx = jnp.arange(CHUNK)
    incl = idx[:, None] >= idx[None, :]
    strict = idx[:, None] > idx[None, :]
    decay = jnp.exp(jnp.where(incl, gc[..., :, None] - gc[..., None, :], -jnp.inf))
    a = jnp.where(strict, beta[..., :, None] * jnp.einsum('nbhid,nbhjd->nbhij', k, k) * decay, 0.0)
    t_mat = a + jnp.eye(CHUNK, dtype=F32)
    solve = functools.partial(lax.linalg.triangular_solve, left_side=True, lower=True, unit_diagonal=True)
    u = solve(t_mat, v * beta[..., None])
    w = solve(t_mat, k * (beta * jnp.exp(gc))[..., None])
    qk = jnp.einsum('nbhid,nbhjd->nbhij', q, k) * decay
    q_dec = q * jnp.exp(gc)[..., None]
    k_dec = k * jnp.exp(gc[..., -1:] - gc)[..., None]
    g_last = jnp.exp(gc[..., -1])

    def step(s, xs):
        u_i, w_i, qk_i, qd_i, kd_i, gl_i = xs
        v_new = u_i - jnp.einsum('bhcd,bhde->bhce', w_i, s)
        o = jnp.einsum('bhcd,bhde->bhce', qd_i, s) + jnp.einsum('bhij,bhje->bhie', qk_i, v_new)
        s = s * gl_i[..., None, None] + jnp.einsum('bhcd,bhce->bhde', kd_i, v_new)
        return s, o

    s_fin, o = lax.scan(step, s0, (u, w, qk, q_dec, k_dec, g_last))
    return from_chunks(o).astype(dt), s_fin


def gla_chunked(q, k, v, log_a, s0):
    dt = v.dtype
    q, k, v, log_a = (to_chunks(t.astype(F32)) for t in (q, k, v, log_a))
    bc = jnp.cumsum(log_a, axis=-2)
    idx = jnp.arange(CHUNK)
    incl = idx[:, None] >= idx[None, :]
    q_dec = q * jnp.exp(bc)
    attn = jnp.where(incl, jnp.einsum('nbhid,nbhjd->nbhij', q_dec, k * jnp.exp(-bc)), 0.0)
    o_intra = jnp.einsum('nbhij,nbhje->nbhie', attn, v)
    b_last = bc[..., -1, :]
    k_dec = k * jnp.exp(b_last[..., None, :] - bc)

    def step(s, xs):
        qd_i, kd_i, v_i, bl_i = xs
        o = jnp.einsum('bhcd,bhde->bhce', qd_i, s)
        s = s * jnp.exp(bl_i)[..., :, None] + jnp.einsum('bhcd,bhce->bhde', kd_i, v_i)
        return s, o

    s_fin, o_inter = lax.scan(step, s0, (q_dec, k_dec, v, b_last))
    return from_chunks(o_intra + o_inter).astype(dt), s_fin


def bidir_prefix_scan(scan_fn, ctx_f, lat_f, ctx_b, lat_b, s0, ctx_out):
    flip = lambda t: jnp.flip(t, axis=1)
    oc_f, sc_f = scan_fn(*ctx_f, s0)
    ol_f, _ = scan_fn(*lat_f, sc_f)
    oc_b, sc_b = scan_fn(*[flip(t) for t in ctx_b], s0)
    ol_b, _ = scan_fn(*[flip(t) for t in lat_b], sc_b)
    o_lat = ol_f + flip(ol_b)
    o_ctx = oc_f + flip(oc_b) if ctx_out else None
    return o_lat, o_ctx


def mla_gdn_mixer(xn, hn, w_in, q_a_norm, w_q_up, kv_a_norm, w_kv_up, q_norm, k_norm,
                  conv_w, a_log, dt_bias, out_norm, w_out, rope, ctx_out):
    b = xn.shape[0]
    zx = split_cols(xn @ w_in, EVEN_SPLITS)
    zh = split_cols(hn @ w_in, EVEN_SPLITS)

    def mla_proj(z, rope_tab):
        cq, ckv, kr = z[0], z[1], z[2]
        n = cq.shape[1]
        q = (rms_norm(cq, q_a_norm) @ w_q_up).reshape(b, n, MLA_HEADS, MLA_QK)
        kv = (rms_norm(ckv, kv_a_norm) @ w_kv_up).reshape(b, n, MLA_HEADS, MLA_NOPE + MLA_V)
        k_nope, v = kv[..., :MLA_NOPE], kv[..., MLA_NOPE:]
        k = jnp.concatenate([k_nope, jnp.broadcast_to(kr[:, :, None, :], (b, n, MLA_HEADS, MLA_ROPE))], axis=-1)
        q, k = rms_norm(q, q_norm), rms_norm(k, k_norm)
        if rope_tab is not None:
            q = jnp.concatenate([q[..., :MLA_NOPE], apply_rope(q[..., MLA_NOPE:], *rope_tab)], axis=-1)
            k = jnp.concatenate([k[..., :MLA_NOPE], apply_rope(k[..., MLA_NOPE:], *rope_tab)], axis=-1)
        return q, k, v

    qx, kx, vx = mla_proj(zx, rope)
    qh, kh, vh = mla_proj(zh, None)
    scale = MLA_QK ** -0.5
    o_mla_x = dense_attention(qx, jnp.concatenate([kx, kh], axis=1), jnp.concatenate([vx, vh], axis=1), scale)

    def gdn_inputs(z):
        qkv = jax.nn.silu(centred_depthwise_conv(jnp.concatenate(z[3:6], axis=-1), conv_w))
        q, k, v = split_cols(qkv, (GDN_QK, GDN_QK, GDN_VW))
        n = q.shape[1]
        q = l2_norm(q.reshape(b, n, GDN_HEADS, GDN_DK)) * GDN_DK ** -0.5
        k = l2_norm(k.reshape(b, n, GDN_HEADS, GDN_DK))
        v = v.reshape(b, n, GDN_HEADS, GDN_DV)
        dirs = []
        for d in range(2):
            g = -jnp.exp(a_log[d].astype(F32)) * jax.nn.softplus(z[6 + d].astype(F32) + dt_bias[d].astype(F32))
            beta = jax.nn.sigmoid(z[8 + d].astype(F32))
            dirs.append((q, k, v, g, beta))
        return dirs

    lat_f, lat_b = gdn_inputs(zx)
    ctx_f, ctx_b = gdn_inputs(zh)
    s0 = jnp.zeros((b, GDN_HEADS, GDN_DK, GDN_DV), F32)
    o_gdn_x, o_gdn_h = bidir_prefix_scan(gated_delta_chunked, ctx_f, lat_f, ctx_b, lat_b, s0, ctx_out)

    def gdn_out(o, zg):
        n = o.shape[1]
        return (rms_norm(o, out_norm) * jax.nn.silu(zg.reshape(b, n, GDN_HEADS, GDN_DV))).reshape(b, n, GDN_VW)

    n_lat = xn.shape[1]
    ox = jnp.concatenate([o_mla_x.reshape(b, n_lat, MLA_VW), gdn_out(o_gdn_x, zx[10])], axis=-1) @ w_out
    if not ctx_out:
        return ox, None
    o_mla_h = dense_attention(qh, kh, vh, scale)
    n_ctx = hn.shape[1]
    oh = jnp.concatenate([o_mla_h.reshape(b, n_ctx, MLA_VW), gdn_out(o_gdn_h, zh[10])], axis=-1) @ w_out
    return ox, oh


def gla_swa_mixer(xn, hn, w_in, gate_w2, gate_b, gla_out_norm, q_norm, k_norm, sink, w_out, rope, ctx_out):
    b = xn.shape[0]
    zx = split_cols(xn @ w_in, ODD_SPLITS)
    zh = split_cols(hn @ w_in, ODD_SPLITS)

    def gla_inputs(z):
        n = z[0].shape[1]
        q = z[0].reshape(b, n, GLA_HEADS, GLA_DK) * GLA_DK ** -0.5
        k = z[1].reshape(b, n, GLA_HEADS, GLA_DK)
        v = z[2].reshape(b, n, GLA_HEADS, GLA_DV)
        dirs = []
        for d in range(2):
            logit = z[3 + d] @ gate_w2[d] + gate_b[d]
            log_a = jax.nn.log_sigmoid(logit.astype(F32)) / GLA_TAU
            dirs.append((q, k, v, log_a.reshape(b, n, GLA_HEADS, GLA_DK)))
        return dirs

    lat_f, lat_b = gla_inputs(zx)
    ctx_f, ctx_b = gla_inputs(zh)
    s0 = jnp.zeros((b, GLA_HEADS, GLA_DK, GLA_DV), F32)
    o_gla_x, o_gla_h = bidir_prefix_scan(gla_chunked, ctx_f, lat_f, ctx_b, lat_b, s0, ctx_out)

    def gla_out(o, r):
        n = o.shape[1]
        return (rms_norm(o, gla_out_norm) * jax.nn.silu(r.reshape(b, n, GLA_HEADS, GLA_DV))).reshape(b, n, GLA_VW)

    def swa_proj(z, rope_tab):
        n = z[6].shape[1]
        q = rms_norm(z[6].reshape(b, n, SWA_HEADS, SWA_DH), q_norm)
        k = rms_norm(z[7].reshape(b, n, SWA_KV_HEADS, SWA_DH), k_norm)
        v = z[8].reshape(b, n, SWA_KV_HEADS, SWA_DH)
        if rope_tab is not None:
            q, k = apply_rope(q, *rope_tab), apply_rope(k, *rope_tab)
        return q, k, v

    qx, kx, vx = swa_proj(zx, rope)
    qh, kh, vh = swa_proj(zh, None)
    scale = SWA_DH ** -0.5
    o_swa_x = banded_window_attention(qx, kx, vx, kh, vh, sink, scale)
    n_lat = xn.shape[1]
    ox = jnp.concatenate([gla_out(o_gla_x, zx[5]), o_swa_x.reshape(b, n_lat, SWA_QW)], axis=-1) @ w_out
    if not ctx_out:
        return ox, None
    o_swa_h = sink_attention(qh, kh, vh, sink, scale)
    n_ctx = hn.shape[1]
    oh = jnp.concatenate([gla_out(o_gla_h, zh[5]), o_swa_h.reshape(b, n_ctx, SWA_QW)], axis=-1) @ w_out
    return ox, oh


def setup_inputs(seed: int = 0) -> dict:
    key = jax.random.key(seed)
    keys = iter(jax.random.split(key, 32))

    def normal(shape, std):
        return jax.random.normal(next(keys), shape, F32) * std

    def gain(shape):
        return 1.0 + 0.05 * jax.random.normal(next(keys), shape, F32)

    d = D_MODEL
    ne, no = N_EVEN, N_ODD
    x = normal((BATCH, SEQ, d), 1.0)
    c = normal((BATCH, d), 1.0)
    ctx = normal((BATCH, CTX_LEN, d), 1.0)
    c_ctx = normal((d,), 1.0)
    ada_w = normal((DEPTH, d, N_MOD * d), 0.5 * d ** -0.5)
    ada_b = normal((DEPTH, N_MOD * d), 0.02)
    norm_g = gain((DEPTH, 3, d))
    ffn_w_gate = normal((DEPTH, 2, d, D_FF), d ** -0.5)
    ffn_w_up = normal((DEPTH, 2, d, D_FF), d ** -0.5)
    ffn_w_down = normal((DEPTH, 2, D_FF, d), D_FF ** -0.5)
    ev_w_in = normal((ne, d, EVEN_IN), d ** -0.5)
    ev_q_a_norm = gain((ne, MLA_Q_RANK))
    ev_w_q_up = normal((ne, MLA_Q_RANK, MLA_HEADS * MLA_QK), MLA_Q_RANK ** -0.5)
    ev_kv_a_norm = gain((ne, MLA_KV_RANK))
    ev_w_kv_up = normal((ne, MLA_KV_RANK, MLA_HEADS * (MLA_NOPE + MLA_V)), MLA_KV_RANK ** -0.5)
    ev_mla_q_norm = gain((ne, MLA_QK))
    ev_mla_k_norm = gain((ne, MLA_QK))
    ev_gdn_conv = normal((ne, GDN_CONV, 2 * GDN_QK + GDN_VW), GDN_CONV ** -0.5)
    ev_gdn_a_log = jnp.log(jax.random.uniform(next(keys), (ne, 2, GDN_HEADS), F32, 1.0, 16.0))
    dt = jnp.exp(jax.random.uniform(next(keys), (ne, 2, GDN_HEADS), F32, math.log(1e-3), math.log(1e-1)))
    ev_gdn_dt_bias = dt + jnp.log(-jnp.expm1(-dt))
    ev_gdn_out_norm = gain((ne, GDN_DV))
    ev_w_out = normal((ne, MIX_EVEN, d), MIX_EVEN ** -0.5)
    od_w_in = normal((no, d, ODD_IN), d ** -0.5)
    od_gla_gate_w2 = normal((no, 2, GLA_RANK, GLA_QK), GLA_RANK ** -0.5)
    od_gla_gate_b = normal((no, 2, GLA_QK), 0.1)
    od_gla_out_norm = gain((no, GLA_DV))
    od_swa_q_norm = gain((no, SWA_DH))
    od_swa_k_norm = gain((no, SWA_DH))
    od_swa_sink = normal((no, SWA_HEADS), 0.5)
    od_w_out = normal((no, MIX_ODD, d), MIX_ODD ** -0.5)
    return {
        'x': x, 'c': c, 'ctx': ctx, 'c_ctx': c_ctx,
        'ada_w': ada_w, 'ada_b': ada_b, 'norm_g': norm_g,
        'ffn_w_gate': ffn_w_gate, 'ffn_w_up': ffn_w_up, 'ffn_w_down': ffn_w_down,
        'ev_w_in': ev_w_in, 'ev_q_a_norm': ev_q_a_norm, 'ev_w_q_up': ev_w_q_up,
        'ev_kv_a_norm': ev_kv_a_norm, 'ev_w_kv_up': ev_w_kv_up,
        'ev_mla_q_norm': ev_mla_q_norm, 'ev_mla_k_norm': ev_mla_k_norm,
        'ev_gdn_conv': ev_gdn_conv, 'ev_gdn_a_log': ev_gdn_a_log, 'ev_gdn_dt_bias': ev_gdn_dt_bias,
        'ev_gdn_out_norm': ev_gdn_out_norm, 'ev_w_out': ev_w_out,
        'od_w_in': od_w_in, 'od_gla_gate_w2': od_gla_gate_w2, 'od_gla_gate_b': od_gla_gate_b,
        'od_gla_out_norm': od_gla_out_norm, 'od_swa_q_norm': od_swa_q_norm, 'od_swa_k_norm': od_swa_k_norm,
        'od_swa_sink': od_swa_sink, 'od_w_out': od_w_out,
    }


def reference(x, c, ctx, c_ctx, ada_w, ada_b, norm_g, ffn_w_gate, ffn_w_up, ffn_w_down,
              ev_w_in, ev_q_a_norm, ev_w_q_up, ev_kv_a_norm, ev_w_kv_up, ev_mla_q_norm, ev_mla_k_norm,
              ev_gdn_conv, ev_gdn_a_log, ev_gdn_dt_bias, ev_gdn_out_norm, ev_w_out,
              od_w_in, od_gla_gate_w2, od_gla_gate_b, od_gla_out_norm, od_swa_q_norm, od_swa_k_norm,
              od_swa_sink, od_w_out):
    ROWS = x.shape[1] // GRID_W
    rope_mla = axial_rope_tables(ROWS, MLA_ROPE)
    rope_swa = axial_rope_tables(ROWS, SWA_DH)
    h = ctx
    sc = jax.nn.silu(c)
    scc = jax.nn.silu(c_ctx)
    for i in range(DEPTH):
        ctx_out = i < DEPTH - 1
        mx = jnp.split((sc @ ada_w[i] + ada_b[i])[:, None, :], N_MOD, axis=-1)
        mh = jnp.split((scc @ ada_w[i] + ada_b[i])[None, None, :], N_MOD, axis=-1)
        x = ffn_half_step(x, norm_g[i, 0], mx[0], mx[1], mx[2], ffn_w_gate[i, 0], ffn_w_up[i, 0], ffn_w_down[i, 0])
        h = ffn_half_step(h, norm_g[i, 0], mh[0], mh[1], mh[2], ffn_w_gate[i, 0], ffn_w_up[i, 0], ffn_w_down[i, 0])
        xn = modulate(rms_norm(x, norm_g[i, 1]), mx[3], mx[4])
        hn = modulate(rms_norm(h, norm_g[i, 1]), mh[3], mh[4])
        j = i // 2
        if i % 2 == 0:
            ox, oh = mla_gdn_mixer(xn, hn, ev_w_in[j], ev_q_a_norm[j], ev_w_q_up[j], ev_kv_a_norm[j],
                                   ev_w_kv_up[j], ev_mla_q_norm[j], ev_mla_k_norm[j], ev_gdn_conv[j],
                                   ev_gdn_a_log[j], ev_gdn_dt_bias[j], ev_gdn_out_norm[j], ev_w_out[j],
                                   rope_mla, ctx_out)
        else:
            ox, oh = gla_swa_mixer(xn, hn, od_w_in[j], od_gla_gate_w2[j], od_gla_gate_b[j], od_gla_out_norm[j],
                                   od_swa_q_norm[j], od_swa_k_norm[j], od_swa_sink[j], od_w_out[j],
                                   rope_swa, ctx_out)
        x = x + mx[5] * ox
        x = ffn_half_step(x, norm_g[i, 2], mx[6], mx[7], mx[8], ffn_w_gate[i, 1], ffn_w_up[i, 1], ffn_w_down[i, 1])
        if ctx_out:
            h = h + mh[5] * oh
            h = ffn_half_step(h, norm_g[i, 2], mh[6], mh[7], mh[8], ffn_w_gate[i, 1], ffn_w_up[i, 1], ffn_w_down[i, 1])
    return x
```

```python
import functools

import jax
import jax.numpy as jnp
from jax import lax
from jax.experimental import pallas as pl
from jax.experimental.pallas import tpu as pltpu

F32 = jnp.float32
BF16 = jnp.bfloat16

D_MODEL = 1024
GRID_W = 64
D_FF = 2816
FFN_RES = 0.5
N_MOD = 9
EPS = 1e-6
ROPE_THETA = 10000.0
CHUNK = 64

MLA_HEADS = 8
MLA_NOPE = 64
MLA_ROPE = 32
MLA_QK = MLA_NOPE + MLA_ROPE
MLA_V = 64
MLA_Q_RANK = 384
MLA_KV_RANK = 256
MLA_VW = MLA_HEADS * MLA_V

GDN_HEADS = 8
GDN_DK = 64
GDN_DV = 64
GDN_CONV = 5
GDN_QK = GDN_HEADS * GDN_DK
GDN_VW = GDN_HEADS * GDN_DV

GLA_HEADS = 4
GLA_DK = 64
GLA_DV = 128
GLA_RANK = 16
GLA_TAU = 16.0
GLA_QK = GLA_HEADS * GLA_DK
GLA_VW = GLA_HEADS * GLA_DV

SWA_HEADS = 8
SWA_KV_HEADS = 2
SWA_DH = 64
SWA_WINDOW = 128
SWA_QW = SWA_HEADS * SWA_DH
SWA_KW = SWA_KV_HEADS * SWA_DH

LANES = 128
HEAD_PAD = 128
TM = 256
VMEM_LIMIT = 56 * 1024 * 1024

EVEN_COLS = MLA_Q_RANK + MLA_KV_RANK + LANES + 2 * GDN_QK + GDN_VW + GDN_VW + LANES
ODD_COLS = 2 * GLA_QK + GLA_VW + GLA_VW + SWA_QW + 2 * SWA_KW + LANES


def _dot(a, b):
    return jnp.dot(a.astype(BF16), b.astype(BF16), preferred_element_type=F32)


def _dot_nt(a, b):
    return lax.dot_general(a.astype(BF16), b.astype(BF16), (((1,), (1,)), ((), ())), preferred_element_type=F32)


def _dot_tn(a, b):
    return lax.dot_general(a.astype(BF16), b.astype(BF16), (((0,), (0,)), ((), ())), preferred_element_type=F32)


def _split2(x):
    hi = x.astype(BF16)
    lo = (x - hi.astype(F32)).astype(BF16)
    return hi, lo


def _split3(x):
    hi = x.astype(BF16)
    r = x - hi.astype(F32)
    mid = r.astype(BF16)
    lo = (r - mid.astype(F32)).astype(BF16)
    return hi, mid, lo


def _dot3(a, b):
    ah, al = _split2(a)
    bh, bl = _split2(b)
    d = functools.partial(jnp.dot, preferred_element_type=F32)
    return d(ah, bh) + (d(ah, bl) + d(al, bh))


def _dot_sel(sel, x):
    s = sel.astype(BF16)
    hi, mid, lo = _split3(x)
    d = functools.partial(jnp.dot, preferred_element_type=F32)
    return d(s, hi) + (d(s, mid) + d(s, lo))


def _seg_mean_sq(x, seg, n_real):
    w = x.shape[-1]
    r = lax.broadcasted_iota(jnp.int32, (w, w), 0) // seg
    c = lax.broadcasted_iota(jnp.int32, (w, w), 1) // seg
    ones_bd = (r == c).astype(BF16)
    hi, lo = _split2(x * x)
    d = functools.partial(jnp.dot, preferred_element_type=F32)
    return (d(hi, ones_bd) + d(lo, ones_bd)) * (1.0 / n_real)


def _rms_rows(x, g):
    ms = jnp.mean(x * x, axis=-1, keepdims=True)
    return x * lax.rsqrt(ms + EPS) * g


def _sigmoid(x):
    return 1.0 / (1.0 + jnp.exp(-x))


def _silu(x):
    return x * _sigmoid(x)


def _softplus(x):
    return jnp.maximum(x, 0.0) + jnp.log1p(jnp.exp(-jnp.abs(x)))


def _rope(x, c, s1, s2, half):
    return x * c + pltpu.roll(x, LANES - half, 1) * s1 + pltpu.roll(x, half, 1) * s2


def _chunk_tri(n, reverse):
    i = lax.broadcasted_iota(jnp.int32, (n, n), 0)
    j = lax.broadcasted_iota(jnp.int32, (n, n), 1)
    same = (i // CHUNK) == (j // CHUNK)
    tri = (j >= i) if reverse else (j <= i)
    return (same & tri).astype(F32)


def _ada_kernel(c_ref, w_ref, b_ref, o_ref):
    sc = _silu(c_ref[...])
    o_ref[...] = _dot(sc, w_ref[...]) + b_ref[...]


def _ada(cvec, ada_w, ada_b):
    depth, d, nm = ada_w.shape
    tn = 1024
    return pl.pallas_call(
        _ada_kernel,
        grid=(depth, nm // tn),
        in_specs=[
            pl.BlockSpec(cvec.shape, lambda i, j: (0, 0)),
            pl.BlockSpec((None, d, tn), lambda i, j: (i, 0, j)),
            pl.BlockSpec((None, 1, tn), lambda i, j: (i, 0, j)),
        ],
        out_specs=pl.BlockSpec((None, cvec.shape[0], tn), lambda i, j: (i, 0, j)),
        out_shape=jax.ShapeDtypeStruct((depth, cvec.shape[0], nm), F32),
        compiler_params=pltpu.CompilerParams(vmem_limit_bytes=VMEM_LIMIT),
        name="ada_mod",
    )(cvec, ada_w, ada_b.reshape(depth, 1, nm))


def _const_spec(shape):
    nd = len(shape)
    return pl.BlockSpec(shape, lambda *_: (0,) * nd, pipeline_mode=pl.Buffered(1))


def _tok_spec(width, skip=0):
    return pl.BlockSpec((None, TM, width), lambda b, t: (b, t + skip, 0))


def _mod_spec(ctx_tiles, skip=0):
    return pl.BlockSpec((None, None, N_MOD, D_MODEL),
                        lambda b, t: (b, jnp.where(t + skip >= ctx_tiles, 1, 0), 0, 0))


def _params():
    return pltpu.CompilerParams(dimension_semantics=("parallel", "arbitrary"), vmem_limit_bytes=VMEM_LIMIT)


def _ffn_kernel(x_ref, mod_ref, g_ref, wg_ref, wu_ref, wd_ref, o_ref, *, r0):
    x = x_ref[...]
    m = mod_ref[...]
    h = _rms_rows(x, g_ref[...]) * (1.0 + m[r0 + 1:r0 + 2]) + m[r0:r0 + 1]
    hb = h.astype(BF16)
    a = jnp.dot(hb, wg_ref[...], preferred_element_type=F32)
    u = jnp.dot(hb, wu_ref[...], preferred_element_type=F32)
    act = (_silu(a) * u).astype(BF16)
    y = jnp.dot(act, wd_ref[...], preferred_element_type=F32)
    o_ref[...] = x + (FFN_RES * m[r0 + 2:r0 + 3]) * y


def _ffn(xs, mod, g, wg, wu, wd, *, r0, ctx_tiles, skip=0):
    b, rows, d = xs.shape
    out_rows = rows - skip * TM
    return pl.pallas_call(
        functools.partial(_ffn_kernel, r0=r0),
        grid=(b, out_rows // TM),
        in_specs=[
            _tok_spec(d, skip), _mod_spec(ctx_tiles, skip), _const_spec((1, d)),
            _const_spec(wg.shape), _const_spec(wu.shape), _const_spec(wd.shape),
        ],
        out_specs=_tok_spec(d),
        out_shape=jax.ShapeDtypeStruct((b, out_rows, d), F32),
        compiler_params=_params(),
        name="ffn_half_step",
    )(xs, mod, g.reshape(1, d), wg, wu, wd)


def _inproj_even_kernel(x_ref, mod_ref, g_ref, win_ref, qan_ref, wq_ref, kvan_ref, wkk_ref, wkv_ref,
                        qn_ref, kn_ref, rc_ref, rs1_ref, rs2_ref,
                        q_ref, k_ref, v_ref, zg_ref, gate_ref, small_ref):
    x = x_ref[...]
    m = mod_ref[...]
    h = _rms_rows(x, g_ref[...]) * (1.0 + m[4:5]) + m[3:4]
    z = jnp.dot(h.astype(BF16), win_ref[...], preferred_element_type=F32)
    o = 0
    cq = z[:, o:o + MLA_Q_RANK]; o += MLA_Q_RANK
    ckv = z[:, o:o + MLA_KV_RANK]; o += MLA_KV_RANK
    kr = z[:, o:o + LANES]; o += LANES
    zg_ref[...] = z[:, o:o + 3 * GDN_QK]; o += 3 * GDN_QK
    gate_ref[...] = z[:, o:o + GDN_VW]; o += GDN_VW
    small_ref[...] = z[:, o:o + LANES]

    rc, rs1, rs2 = rc_ref[...], rs1_ref[...], rs2_ref[...]
    qn, kn = qn_ref[...], kn_ref[...]
    qf = jnp.dot(_rms_rows(cq, qan_ref[...]).astype(BF16), wq_ref[...], preferred_element_type=F32)
    ckvn = _rms_rows(ckv, kvan_ref[...]).astype(BF16)
    kf = jnp.dot(ckvn, wkk_ref[...], preferred_element_type=F32)
    v_ref[...] = jnp.dot(ckvn, wkv_ref[...], preferred_element_type=F32).astype(v_ref.dtype)
    for hd in range(MLA_HEADS):
        sl = slice(hd * HEAD_PAD, (hd + 1) * HEAD_PAD)
        qh = qf[:, sl]
        qh = qh * lax.rsqrt(jnp.sum(qh * qh, axis=-1, keepdims=True) * (1.0 / MLA_QK) + EPS) * qn
        q_ref[:, sl] = _rope(qh, rc, rs1, rs2, MLA_ROPE // 2).astype(q_ref.dtype)
        kh = kf[:, sl] + kr
        kh = kh * lax.rsqrt(jnp.sum(kh * kh, axis=-1, keepdims=True) * (1.0 / MLA_QK) + EPS) * kn
        k_ref[:, sl] = _rope(kh, rc, rs1, rs2, MLA_ROPE // 2).astype(k_ref.dtype)


def _inproj_even(xs, mod, g, p, rope):
    b, rows, d = xs.shape
    ctx_tiles = p["ctx_tiles"]
    consts = [g.reshape(1, d), p["w_in"], p["q_a_norm"], p["w_q"], p["kv_a_norm"], p["w_kk"], p["w_kv"],
              p["q_norm"], p["k_norm"]]
    rope_spec = pl.BlockSpec((TM, LANES), lambda b_, t: (t, 0))
    widths = [(MLA_HEADS * HEAD_PAD, BF16), (MLA_HEADS * HEAD_PAD, BF16), (MLA_VW, BF16),
              (3 * GDN_QK, F32), (GDN_VW, F32), (LANES, F32)]
    return pl.pallas_call(
        _inproj_even_kernel,
        grid=(b, rows // TM),
        in_specs=[_tok_spec(d), _mod_spec(ctx_tiles)] + [_const_spec(c.shape) for c in consts] + [rope_spec] * 3,
        out_specs=[_tok_spec(w) for w, _ in widths],
        out_shape=[jax.ShapeDtypeStruct((b, rows, w), dt) for w, dt in widths],
        compiler_params=_params(),
        name="inproj_even",
    )(xs, mod, *consts, *rope)


def _mla_kernel(q_ref, k_ref, v_ref, o_ref, *, n_ctx, n_all, ctx_tiles):
    t = pl.program_id(2)
    scale = MLA_QK ** -0.5
    lo = lax.broadcasted_iota(jnp.int32, (TM, LANES), 1) < MLA_V

    def attend(nk):
        q = q_ref[...]
        v = v_ref[0:nk, :]
        outs = []
        for hh in range(2):
            sl = slice(hh * HEAD_PAD, (hh + 1) * HEAD_PAD)
            s = lax.dot_general(q[:, sl], k_ref[0:nk, sl], (((1,), (1,)), ((), ())),
                                preferred_element_type=F32) * scale
            p = jnp.exp(s - jnp.max(s, axis=-1, keepdims=True))
            den = jnp.sum(p, axis=-1, keepdims=True)
            outs.append(jnp.dot(p.astype(BF16), v, preferred_element_type=F32) / den)
        o_ref[...] = jnp.where(lo, outs[0], outs[1]).astype(o_ref.dtype)

    @pl.when(t < ctx_tiles)
    def _():
        attend(n_ctx)

    @pl.when(t >= ctx_tiles)
    def _():
        attend(n_all)


def _mla_attention(q, k, v, *, n_ctx):
    b, rows, _ = q.shape
    pairs = MLA_HEADS // 2
    return pl.pallas_call(
        functools.partial(_mla_kernel, n_ctx=n_ctx, n_all=rows, ctx_tiles=n_ctx // TM),
        grid=(b, pairs, rows // TM),
        in_specs=[
            pl.BlockSpec((None, TM, 2 * HEAD_PAD), lambda b_, h, t: (b_, t, h)),
            pl.BlockSpec((None, rows, 2 * HEAD_PAD), lambda b_, h, t: (b_, 0, h)),
            pl.BlockSpec((None, rows, 2 * MLA_V), lambda b_, h, t: (b_, 0, h)),
        ],
        out_specs=pl.BlockSpec((None, TM, 2 * MLA_V), lambda b_, h, t: (b_, t, h)),
        out_shape=jax.ShapeDtypeStruct((b, rows, MLA_VW), BF16),
        compiler_params=pltpu.CompilerParams(dimension_semantics=("parallel", "parallel", "arbitrary"),
                                             vmem_limit_bytes=VMEM_LIMIT),
        name="mla_attention",
    )(q, k, v)


def _gdn_prep_kernel(z_ref, zp_ref, zn_ref, cw_ref, sm_ref, nega_ref, dtb_ref,
                     q_ref, k_ref, v_ref, gcol_ref, grow_ref, *, ctx_tiles, n_tiles):
    t = pl.program_id(1)
    first = (t == 0) | (t == ctx_tiles)
    last = (t == ctx_tiles - 1) | (t == n_tiles - 1)
    z = z_ref[...]
    half = GDN_CONV // 2
    prev = jnp.where(first, 0.0, zp_ref[...][8 - half:, :])
    nxt = jnp.where(last, 0.0, zn_ref[...][:half, :])
    ext = jnp.concatenate([prev, z, nxt], axis=0)
    cw = cw_ref[...]
    acc = ext[0:TM] * cw[0:1]
    for j in range(1, GDN_CONV):
        acc = acc + ext[j:j + TM] * cw[j:j + 1]
    qkv = _silu(acc)
    q = qkv[:, :GDN_QK]
    k = qkv[:, GDN_QK:2 * GDN_QK]
    q_ref[...] = q * lax.rsqrt(_seg_mean_sq(q, GDN_DK, 1.0) + EPS) * (GDN_DK ** -0.5)
    k_ref[...] = k * lax.rsqrt(_seg_mean_sq(k, GDN_DK, 1.0) + EPS)
    v_ref[...] = qkv[:, 2 * GDN_QK:]

    sm = sm_ref[...]
    lane = lax.broadcasted_iota(jnp.int32, sm.shape, 1)
    g = nega_ref[...] * _softplus(sm + dtb_ref[...])
    g = jnp.where(lane < 2 * GDN_HEADS, g, 0.0)
    gc_f = _dot_sel(_chunk_tri(TM, False), g)
    gc_b = _dot_sel(_chunk_tri(TM, True), g)
    gates = jnp.where(lane < GDN_HEADS, gc_f, jnp.where(lane < 2 * GDN_HEADS, gc_b, _sigmoid(sm)))
    gcol_ref[...] = gates
    gt = gates.T
    for c in range(TM // CHUNK):
        grow_ref[c] = gt[0:4 * GDN_HEADS, c * CHUNK:(c + 1) * CHUNK]


def _gdn_prep(zg, small, conv_w, nega, dtb, *, ctx_tiles):
    b, rows, w = zg.shape
    n_tiles = rows // TM
    hb = TM // 8
    n_hblk = rows // 8
    return pl.pallas_call(
        functools.partial(_gdn_prep_kernel, ctx_tiles=ctx_tiles, n_tiles=n_tiles),
        grid=(b, n_tiles),
        in_specs=[
            _tok_spec(w),
            pl.BlockSpec((None, 8, w), lambda b_, t: (b_, jnp.maximum(t * hb - 1, 0), 0)),
            pl.BlockSpec((None, 8, w), lambda b_, t: (b_, jnp.minimum((t + 1) * hb, n_hblk - 1), 0)),
            _const_spec(conv_w.shape), _tok_spec(LANES), _const_spec((1, LANES)), _const_spec((1, LANES)),
        ],
        out_specs=[_tok_spec(GDN_QK), _tok_spec(GDN_QK), _tok_spec(GDN_VW), _tok_spec(LANES),
                   pl.BlockSpec((None, TM // CHUNK, 4 * GDN_HEADS, CHUNK), lambda b_, t: (b_, t, 0, 0))],
        out_shape=[jax.ShapeDtypeStruct((b, rows, GDN_QK), F32), jax.ShapeDtypeStruct((b, rows, GDN_QK), F32),
                   jax.ShapeDtypeStruct((b, rows, GDN_VW), F32), jax.ShapeDtypeStruct((b, rows, LANES), F32),
                   jax.ShapeDtypeStruct((b, rows // CHUNK, 4 * GDN_HEADS, CHUNK), F32)],
        compiler_params=_params(),
        name="gdn_prep",
    )(zg, zg, zg, conv_w, small, nega, dtb)


def _unit_tri_inverse(a, reverse):
    n = a.shape[0]
    i = lax.broadcasted_iota(jnp.int32, (n, n), 0)
    j = lax.broadcasted_iota(jnp.int32, (n, n), 1)
    m = (i == j).astype(F32)
    blk = 1
    while blk < n:
        same = (i // (2 * blk)) == (j // (2 * blk))
        off = (i // blk) < (j // blk) if reverse else (i // blk) > (j // blk)
        bm = jnp.where(same & off, a, 0.0)
        if blk == 1:
            m = m - bm
        else:
            m = m - _dot3(m, _dot3(bm, m))
        blk *= 2
    return m


def _gdn_chunk(q, k, v, gcol, grow, s_ref, reverse):
    ii = lax.broadcasted_iota(jnp.int32, (CHUNK, CHUNK), 0)
    jj = lax.broadcasted_iota(jnp.int32, (CHUNK, CHUNK), 1)
    incl = (ii <= jj) if reverse else (ii >= jj)
    strict = (ii < jj) if reverse else (ii > jj)
    last = 0 if reverse else CHUNK - 1
    outs = []
    for h in range(GDN_HEADS):
        go = (GDN_HEADS if reverse else 0) + h
        bo = 2 * GDN_HEADS + go
        gc_c = gcol[:, go:go + 1]
        gc_r = grow[go:go + 1, :]
        beta = gcol[:, bo:bo + 1]
        qh = q[:, h * GDN_DK:(h + 1) * GDN_DK]
        kh = k[:, h * GDN_DK:(h + 1) * GDN_DK]
        vh = v[:, h * GDN_DV:(h + 1) * GDN_DV]
        decay = jnp.exp(jnp.where(incl, gc_c - gc_r, -jnp.inf))
        a = jnp.where(strict, beta * _dot_nt(kh, kh) * decay, 0.0)
        tinv = _unit_tri_inverse(a, reverse)
        egc = jnp.exp(gc_c)
        u = _dot3(tinv, vh * beta)
        w = _dot3(tinv, kh * (beta * egc))
        qk = _dot_nt(qh, kh) * decay
        g_last = gc_c[last:last + 1, :]
        s = s_ref[h]
        v_new = u - _dot(w, s)
        outs.append(_dot(qh * egc, s) + _dot(qk, v_new))
        s_ref[h] = s * jnp.exp(g_last) + _dot_tn(kh * jnp.exp(g_last - gc_c), v_new)
    return jnp.concatenate(outs, axis=-1)


def _gdn_scan_kernel(qf_ref, kf_ref, vf_ref, gcf_ref, grf_ref, qb_ref, kb_ref, vb_ref, gcb_ref, grb_ref,
                     of_ref, ob_ref, sf_ref, sb_ref):
    @pl.when(pl.program_id(1) == 0)
    def _():
        sf_ref[...] = jnp.zeros_like(sf_ref)
        sb_ref[...] = jnp.zeros_like(sb_ref)

    of_ref[...] = _gdn_chunk(qf_ref[...], kf_ref[...], vf_ref[...], gcf_ref[...], grf_ref[...], sf_ref, False)
    ob_ref[...] = _gdn_chunk(qb_ref[...], kb_ref[...], vb_ref[...], gcb_ref[...], grb_ref[...], sb_ref, True)


def _scan_chunk_maps(ctx_chunks, n_chunks):
    fwd = lambda s: s
    bwd = lambda s: jnp.where(s < ctx_chunks, ctx_chunks - 1 - s, n_chunks - 1 - (s - ctx_chunks))
    return fwd, bwd


def _gdn_scan(q, k, v, gcol, grow, *, n_ctx):
    b, rows, _ = q.shape
    n_chunks = rows // CHUNK
    fwd, bwd = _scan_chunk_maps(n_ctx // CHUNK, n_chunks)

    def specs(cm):
        tok = lambda w: pl.BlockSpec((None, CHUNK, w), lambda b_, s: (b_, cm(s), 0))
        return [tok(GDN_QK), tok(GDN_QK), tok(GDN_VW), tok(LANES),
                pl.BlockSpec((None, None, 4 * GDN_HEADS, CHUNK), lambda b_, s: (b_, cm(s), 0, 0))]

    return pl.pallas_call(
        _gdn_scan_kernel,
        grid=(b, n_chunks),
        in_specs=specs(fwd) + specs(bwd),
        out_specs=[pl.BlockSpec((None, CHUNK, GDN_VW), lambda b_, s: (b_, fwd(s), 0)),
                   pl.BlockSpec((None, CHUNK, GDN_VW), lambda b_, s: (b_, bwd(s), 0))],
        out_shape=[jax.ShapeDtypeStruct((b, rows, GDN_VW), F32)] * 2,
        scratch_shapes=[pltpu.VMEM((GDN_HEADS, GDN_DK, GDN_DV), F32)] * 2,
        compiler_params=_params(),
        name="gdn_scan",
    )(q, k, v, gcol, grow, q, k, v, gcol, grow)


def _outproj_kernel(x_ref, mod_ref, oa_ref, of_ref, ob_ref, gate_ref, gn_ref, w_ref, o_ref, *, seg, rec_first):
    o = of_ref[...] + ob_ref[...]
    y = o * lax.rsqrt(_seg_mean_sq(o, seg, float(seg)) + EPS) * gn_ref[...] * _silu(gate_ref[...])
    wr = y.shape[-1]
    wa = oa_ref.shape[-1]
    if rec_first:
        out = jnp.dot(y.astype(BF16), w_ref[0:wr, :], preferred_element_type=F32)
        out = out + jnp.dot(oa_ref[...], w_ref[wr:wr + wa, :], preferred_element_type=F32)
    else:
        out = jnp.dot(oa_ref[...], w_ref[0:wa, :], preferred_element_type=F32)
        out = out + jnp.dot(y.astype(BF16), w_ref[wa:wa + wr, :], preferred_element_type=F32)
    o_ref[...] = x_ref[...] + mod_ref[...][5:6] * out


def _outproj(xs, mod, o_att, o_f, o_b, gate, gn, w_out, *, seg, rec_first, ctx_tiles, skip=0, att_skip=0):
    b, rows, d = xs.shape
    out_rows = rows - skip * TM
    return pl.pallas_call(
        functools.partial(_outproj_kernel, seg=seg, rec_first=rec_first),
        grid=(b, out_rows // TM),
        in_specs=[_tok_spec(d, skip), _mod_spec(ctx_tiles, skip), _tok_spec(o_att.shape[-1], att_skip),
                  _tok_spec(o_f.shape[-1], skip), _tok_spec(o_b.shape[-1], skip), _tok_spec(gate.shape[-1], skip),
                  _const_spec(gn.shape), _const_spec(w_out.shape)],
        out_specs=_tok_spec(d),
        out_shape=jax.ShapeDtypeStruct((b, out_rows, d), F32),
        compiler_params=_params(),
        name="outproj",
    )(xs, mod, o_att, o_f, o_b, gate, gn, w_out)


def _inproj_odd_kernel(x_ref, mod_ref, g_ref, win_ref, w2_ref, b2_ref, qn_ref, kn_ref, rc_ref, rs1_ref, rs2_ref,
                       gq_ref, gk_ref, gv_ref, rg_ref, bc_ref, sq_ref, sk_ref, sv_ref):
    x = x_ref[...]
    m = mod_ref[...]
    h = _rms_rows(x, g_ref[...]) * (1.0 + m[4:5]) + m[3:4]
    z = jnp.dot(h.astype(BF16), win_ref[...], preferred_element_type=F32)
    o = 0
    gq_ref[...] = z[:, o:o + GLA_QK]; o += GLA_QK
    gk_ref[...] = z[:, o:o + GLA_QK]; o += GLA_QK
    gv_ref[...] = z[:, o:o + GLA_VW]; o += GLA_VW
    rg_ref[...] = z[:, o:o + GLA_VW]; o += GLA_VW
    sq = z[:, o:o + SWA_QW]; o += SWA_QW
    sk = z[:, o:o + SWA_KW]; o += SWA_KW
    sv_ref[...] = z[:, o:o + SWA_KW].astype(sv_ref.dtype); o += SWA_KW
    lowrank = z[:, o:o + LANES]

    logit = _dot(lowrank, w2_ref[...]) + b2_ref[...]
    log_a = (jnp.minimum(logit, 0.0) - jnp.log1p(jnp.exp(-jnp.abs(logit)))) * (1.0 / GLA_TAU)
    bc_ref[:, 0:GLA_QK] = _dot_sel(_chunk_tri(TM, False), log_a[:, 0:GLA_QK])
    bc_ref[:, GLA_QK:2 * GLA_QK] = _dot_sel(_chunk_tri(TM, True), log_a[:, GLA_QK:2 * GLA_QK])

    rc, rs1, rs2 = rc_ref[...], rs1_ref[...], rs2_ref[...]
    sqn = sq * lax.rsqrt(_seg_mean_sq(sq, SWA_DH, float(SWA_DH)) + EPS) * qn_ref[...]
    for s in range(SWA_QW // LANES):
        sl = slice(s * LANES, (s + 1) * LANES)
        sq_ref[:, sl] = _rope(sqn[:, sl], rc, rs1, rs2, SWA_DH // 2).astype(sq_ref.dtype)
    skn = sk * lax.rsqrt(_seg_mean_sq(sk, SWA_DH, float(SWA_DH)) + EPS) * kn_ref[...]
    sk_ref[...] = _rope(skn, rc, rs1, rs2, SWA_DH // 2).astype(sk_ref.dtype)


def _inproj_odd(xs, mod, g, p, rope):
    b, rows, d = xs.shape
    consts = [g.reshape(1, d), p["w_in"], p["w2"], p["b2"], p["q_norm"], p["k_norm"]]
    rope_spec = pl.BlockSpec((TM, LANES), lambda b_, t: (t, 0))
    widths = [(GLA_QK, F32), (GLA_QK, F32), (GLA_VW, F32), (GLA_VW, F32), (2 * GLA_QK, F32),
              (SWA_QW, BF16), (SWA_KW, BF16), (SWA_KW, BF16)]
    return pl.pallas_call(
        _inproj_odd_kernel,
        grid=(b, rows // TM),
        in_specs=[_tok_spec(d), _mod_spec(p["ctx_tiles"])] + [_const_spec(c.shape) for c in consts] + [rope_spec] * 3,
        out_specs=[_tok_spec(w) for w, _ in widths],
        out_shape=[jax.ShapeDtypeStruct((b, rows, w), dt) for w, dt in widths],
        compiler_params=_params(),
        name="inproj_odd",
    )(xs, mod, *consts, *rope)


def _gla_chunk(q, k, v, bc, st_ref, reverse):
    ii = lax.broadcasted_iota(jnp.int32, (CHUNK, CHUNK), 0)
    jj = lax.broadcasted_iota(jnp.int32, (CHUNK, CHUNK), 1)
    incl = (ii <= jj) if reverse else (ii >= jj)
    last = 0 if reverse else CHUNK - 1
    q_dec = (q * (GLA_DK ** -0.5)) * jnp.exp(bc)
    k_inv = k * jnp.exp(-bc)
    b_last = bc[last:last + 1, :]
    k_dec = k * jnp.exp(b_last - bc)
    e_last = jnp.exp(b_last)
    head = lax.broadcasted_iota(jnp.int32, (CHUNK, GLA_QK), 1) // GLA_DK
    outs = []
    for h in range(GLA_HEADS):
        hm = head == h
        attn = jnp.where(incl, _dot_nt(q_dec, jnp.where(hm, k_inv, 0.0)), 0.0)
        vh = v[:, h * GLA_DV:(h + 1) * GLA_DV]
        st = st_ref[h]
        outs.append(_dot(attn, vh) + _dot_nt(q_dec, st))
        st_ref[h] = st * e_last + _dot_tn(vh, jnp.where(hm, k_dec, 0.0))
    return jnp.concatenate(outs, axis=-1)


def _gla_scan_kernel(qf_ref, kf_ref, vf_ref, bf_ref, qb_ref, kb_ref, vb_ref, bb_ref, of_ref, ob_ref, sf_ref, sb_ref):
    @pl.when(pl.program_id(1) == 0)
    def _():
        sf_ref[...] = jnp.zeros_like(sf_ref)
        sb_ref[...] = jnp.zeros_like(sb_ref)

    of_ref[...] = _gla_chunk(qf_ref[...], kf_ref[...], vf_ref[...], bf_ref[...][:, 0:GLA_QK], sf_ref, False)
    ob_ref[...] = _gla_chunk(qb_ref[...], kb_ref[...], vb_ref[...], bb_ref[...][:, GLA_QK:2 * GLA_QK], sb_ref, True)


def _gla_scan(q, k, v, bc, *, n_ctx):
    b, rows, _ = q.shape
    n_chunks = rows // CHUNK
    fwd, bwd = _scan_chunk_maps(n_ctx // CHUNK, n_chunks)

    def specs(cm):
        tok = lambda w: pl.BlockSpec((None, CHUNK, w), lambda b_, s: (b_, cm(s), 0))
        return [tok(GLA_QK), tok(GLA_QK), tok(GLA_VW), tok(2 * GLA_QK)]

    return pl.pallas_call(
        _gla_scan_kernel,
        grid=(b, n_chunks),
        in_specs=specs(fwd) + specs(bwd),
        out_specs=[pl.BlockSpec((None, CHUNK, GLA_VW), lambda b_, s: (b_, fwd(s), 0)),
                   pl.BlockSpec((None, CHUNK, GLA_VW), lambda b_, s: (b_, bwd(s), 0))],
        out_shape=[jax.ShapeDtypeStruct((b, rows, GLA_VW), F32)] * 2,
        scratch_shapes=[pltpu.VMEM((GLA_HEADS, GLA_DV, GLA_QK), F32)] * 2,
        compiler_params=_params(),
        name="gla_scan",
    )(q, k, v, bc, q, k, v, bc)


def _swa_kernel(q_ref, k_ref, v_ref, sink_ref, o_ref, *, n_ctx, n_lat):
    w = SWA_WINDOW
    n = pl.program_id(1)
    nb = n_lat // w
    scale = SWA_DH ** -0.5

    def rows(ref, blk):
        return ref[pl.ds(pl.multiple_of(n_ctx + blk * w, w), w), :]

    pb = jnp.maximum(n - 1, 0)
    xb = jnp.minimum(n + 1, nb - 1)
    k_cat = jnp.concatenate([rows(k_ref, pb), rows(k_ref, n), rows(k_ref, xb), k_ref[0:n_ctx, :]], axis=0)
    v_cat = jnp.concatenate([rows(v_ref, pb), rows(v_ref, n), rows(v_ref, xb), v_ref[0:n_ctx, :]], axis=0)
    ii = lax.broadcasted_iota(jnp.int32, (w, w), 0)
    jj = lax.broadcasted_iota(jnp.int32, (w, w), 1)
    ninf = -jnp.inf
    bias = jnp.concatenate([
        jnp.where((jj >= ii) & (n > 0), 0.0, ninf),
        jnp.zeros((w, w), F32),
        jnp.where((jj <= ii) & (n < nb - 1), 0.0, ninf),
        jnp.zeros((w, n_ctx), F32)], axis=1)
    lo = lax.broadcasted_iota(jnp.int32, (w, LANES), 1) < SWA_DH
    sink = sink_ref[...]
    q = q_ref[...]
    half = SWA_HEADS // 2
    for j in range(half):
        slab = q[:, j * LANES:(j + 1) * LANES]
        res = []
        for g in range(SWA_KV_HEADS):
            hd = j + half * g
            qm = jnp.where(lo if g == 0 else jnp.logical_not(lo), slab, jnp.zeros_like(slab))
            s = lax.dot_general(qm, k_cat, (((1,), (1,)), ((), ())), preferred_element_type=F32) * scale + bias
            sk = sink[0:1, hd:hd + 1]
            mx = jnp.maximum(jnp.max(s, axis=-1, keepdims=True), sk)
            p = jnp.exp(s - mx)
            den = jnp.sum(p, axis=-1, keepdims=True) + jnp.exp(sk - mx)
            res.append(jnp.dot(p.astype(BF16), v_cat, preferred_element_type=F32) / den)
        o_ref[:, j * LANES:(j + 1) * LANES] = jnp.where(lo, res[0], res[1]).astype(o_ref.dtype)


def _swa_attention(q, k, v, sink, *, n_ctx):
    b, rows, _ = q.shape
    n_lat = rows - n_ctx
    w = SWA_WINDOW
    skip = n_ctx // w
    return pl.pallas_call(
        functools.partial(_swa_kernel, n_ctx=n_ctx, n_lat=n_lat),
        grid=(b, n_lat // w),
        in_specs=[
            pl.BlockSpec((None, w, SWA_QW), lambda b_, n: (b_, n + skip, 0)),
            pl.BlockSpec((None, rows, SWA_KW), lambda b_, n: (b_, 0, 0)),
            pl.BlockSpec((None, rows, SWA_KW), lambda b_, n: (b_, 0, 0)),
            _const_spec((1, LANES)),
        ],
        out_specs=pl.BlockSpec((None, w, SWA_QW), lambda b_, n: (b_, n, 0)),
        out_shape=jax.ShapeDtypeStruct((b, n_lat, SWA_QW), BF16),
        compiler_params=_params(),
        name="swa_attention",
    )(q, k, v, sink)


def _rope_tables(n_lat, n_ctx, rot_dim):
    t = jnp.arange(n_lat)
    row = (t // GRID_W).astype(F32)
    col = (t % GRID_W).astype(F32)
    n_freq = rot_dim // 4
    inv = ROPE_THETA ** (-jnp.arange(n_freq, dtype=F32) / n_freq)
    ang = jnp.concatenate([row[:, None] * inv, col[:, None] * inv], axis=-1)
    half = rot_dim // 2
    cos = jnp.concatenate([jnp.ones((n_ctx, half), F32), jnp.cos(ang)], axis=0)
    sin = jnp.concatenate([jnp.zeros((n_ctx, half), F32), jnp.sin(ang)], axis=0)
    rows = n_ctx + n_lat
    one = lambda w: jnp.ones((rows, w), F32)
    zero = lambda w: jnp.zeros((rows, w), F32)
    if rot_dim == MLA_ROPE:
        c = jnp.concatenate([one(MLA_NOPE), cos, cos, one(HEAD_PAD - MLA_QK)], axis=1)
        s1 = jnp.concatenate([zero(MLA_NOPE), -sin, zero(half), zero(HEAD_PAD - MLA_QK)], axis=1)
        s2 = jnp.concatenate([zero(MLA_NOPE), zero(half), sin, zero(HEAD_PAD - MLA_QK)], axis=1)
    else:
        c = jnp.concatenate([cos, cos, cos, cos], axis=1)
        s1 = jnp.concatenate([-sin, zero(half), -sin, zero(half)], axis=1)
        s2 = jnp.concatenate([zero(half), sin, zero(half), sin], axis=1)
    return c, s1, s2


def _even_params(j, n_ctx, ev_w_in, ev_q_a_norm, ev_w_q_up, ev_kv_a_norm, ev_w_kv_up, ev_mla_q_norm, ev_mla_k_norm,
                 ev_gdn_conv, ev_gdn_a_log, ev_gdn_dt_bias, ev_gdn_out_norm, ev_w_out):
    w = ev_w_in[j]
    d = w.shape[0]
    z = lambda n: jnp.zeros((d, n), F32)
    o_kr = MLA_Q_RANK + MLA_KV_RANK
    o_g = o_kr + MLA_ROPE
    o_small = o_g + 3 * GDN_QK
    o_gate = o_small + 4 * GDN_HEADS
    w_in = jnp.concatenate([
        w[:, :o_kr], z(MLA_NOPE), w[:, o_kr:o_g], z(HEAD_PAD - MLA_QK),
        w[:, o_g:o_small], w[:, o_gate:o_gate + GDN_VW],
        w[:, o_small:o_gate], z(LANES - 4 * GDN_HEADS)], axis=1).astype(BF16)
    pad_h = HEAD_PAD - MLA_QK
    w_q = jnp.pad(ev_w_q_up[j].reshape(MLA_Q_RANK, MLA_HEADS, MLA_QK), ((0, 0), (0, 0), (0, pad_h)))
    wkv = ev_w_kv_up[j].reshape(MLA_KV_RANK, MLA_HEADS, MLA_NOPE + MLA_V)
    w_kk = jnp.pad(wkv[:, :, :MLA_NOPE], ((0, 0), (0, 0), (0, HEAD_PAD - MLA_NOPE)))
    lane_row = lambda vec: jnp.pad(vec, (0, LANES - vec.shape[0])).reshape(1, LANES)
    fb = lambda a: jnp.concatenate([a[0], a[1]])
    return {
        "ctx_tiles": n_ctx // TM,
        "w_in": w_in,
        "q_a_norm": ev_q_a_norm[j].reshape(1, -1),
        "w_q": w_q.reshape(MLA_Q_RANK, MLA_HEADS * HEAD_PAD).astype(BF16),
        "kv_a_norm": ev_kv_a_norm[j].reshape(1, -1),
        "w_kk": w_kk.reshape(MLA_KV_RANK, MLA_HEADS * HEAD_PAD).astype(BF16),
        "w_kv": wkv[:, :, MLA_NOPE:].reshape(MLA_KV_RANK, MLA_VW).astype(BF16),
        "q_norm": lane_row(ev_mla_q_norm[j]),
        "k_norm": lane_row(ev_mla_k_norm[j]),
        "conv_w": ev_gdn_conv[j],
        "neg_a": lane_row(-jnp.exp(fb(ev_gdn_a_log[j]))),
        "dt_bias": lane_row(fb(ev_gdn_dt_bias[j])),
        "out_norm": jnp.tile(ev_gdn_out_norm[j], GDN_HEADS).reshape(1, GDN_VW),
        "w_out": ev_w_out[j].astype(BF16),
    }


def _swa_head_perm():
    half = SWA_HEADS // 2
    heads = [h for j in range(half) for h in (j, half + j)]
    return jnp.concatenate([jnp.arange(SWA_DH) + h * SWA_DH for h in heads])


def _odd_params(j, n_ctx, od_w_in, od_gla_gate_w2, od_gla_gate_b, od_gla_out_norm, od_swa_q_norm, od_swa_k_norm,
                od_swa_sink, od_w_out):
    w = od_w_in[j]
    d = w.shape[0]
    o_gate = 2 * GLA_QK + GLA_VW
    o_rg = o_gate + 2 * GLA_RANK
    o_sq = o_rg + GLA_VW
    o_sk = o_sq + SWA_QW
    perm = _swa_head_perm()
    w_in = jnp.concatenate([
        w[:, :o_gate], w[:, o_rg:o_sq], w[:, o_sq:o_sk][:, perm], w[:, o_sk:],
        w[:, o_gate:o_rg], jnp.zeros((d, LANES - 2 * GLA_RANK), F32)], axis=1).astype(BF16)
    w2 = jnp.zeros((LANES, 2 * GLA_QK), F32)
    w2 = w2.at[0:GLA_RANK, 0:GLA_QK].set(od_gla_gate_w2[j, 0])
    w2 = w2.at[GLA_RANK:2 * GLA_RANK, GLA_QK:].set(od_gla_gate_w2[j, 1])
    wo = od_w_out[j]
    w_out = jnp.concatenate([wo[:GLA_VW], wo[GLA_VW:][perm]], axis=0).astype(BF16)
    return {
        "ctx_tiles": n_ctx // TM,
        "w_in": w_in,
        "w2": w2.astype(BF16),
        "b2": jnp.concatenate([od_gla_gate_b[j, 0], od_gla_gate_b[j, 1]]).reshape(1, 2 * GLA_QK),
        "q_norm": jnp.tile(od_swa_q_norm[j], SWA_HEADS).reshape(1, SWA_QW),
        "k_norm": jnp.tile(od_swa_k_norm[j], SWA_KV_HEADS).reshape(1, SWA_KW),
        "sink": jnp.pad(od_swa_sink[j], (0, LANES - SWA_HEADS)).reshape(1, LANES),
        "out_norm": jnp.tile(od_gla_out_norm[j], GLA_HEADS).reshape(1, GLA_VW),
        "w_out": w_out,
    }


def kernel(x, c, ctx, c_ctx, ada_w, ada_b, norm_g, ffn_w_gate, ffn_w_up, ffn_w_down, ev_w_in, ev_q_a_norm, ev_w_q_up, ev_kv_a_norm, ev_w_kv_up, ev_mla_q_norm, ev_mla_k_norm, ev_gdn_conv, ev_gdn_a_log, ev_gdn_dt_bias, ev_gdn_out_norm, ev_w_out, od_w_in, od_gla_gate_w2, od_gla_gate_b, od_gla_out_norm, od_swa_q_norm, od_swa_k_norm, od_swa_sink, od_w_out):
    b, n_lat, d = x.shape
    n_ctx = ctx.shape[1]
    depth = ada_w.shape[0]
    assert d == D_MODEL and n_ctx % TM == 0 and n_lat % TM == 0 and n_lat % GRID_W == 0
    assert depth % 2 == 0, "the last layer must be an odd (GLA/SWA) layer: context outputs of that mixer are not built"
    ctx_tiles = n_ctx // TM

    cvec = jnp.concatenate([c, c_ctx[None, :], jnp.zeros((16 - b - 1, d), F32)], axis=0)
    mod_all = _ada(cvec, ada_w, ada_b).reshape(depth, 16, N_MOD, d)
    rope_mla = _rope_tables(n_lat, n_ctx, MLA_ROPE)
    rope_swa = _rope_tables(n_lat, n_ctx, SWA_DH)

    xs = jnp.concatenate([ctx, x], axis=1)
    for i in range(depth):
        last = i == depth - 1
        mod = jnp.stack([jnp.broadcast_to(mod_all[i, b][None], (b, N_MOD, d)), mod_all[i, :b]], axis=1)
        wg, wu, wd = (t.astype(BF16) for t in (ffn_w_gate[i], ffn_w_up[i], ffn_w_down[i]))
        xs = _ffn(xs, mod, norm_g[i, 0], wg[0], wu[0], wd[0], r0=0, ctx_tiles=ctx_tiles)
        j = i // 2
        skip = ctx_tiles if last else 0
        if i % 2 == 0:
            p = _even_params(j, n_ctx, ev_w_in, ev_q_a_norm, ev_w_q_up, ev_kv_a_norm, ev_w_kv_up, ev_mla_q_norm,
                             ev_mla_k_norm, ev_gdn_conv, ev_gdn_a_log, ev_gdn_dt_bias, ev_gdn_out_norm, ev_w_out)
            q, k, v, zg, gate, small = _inproj_even(xs, mod, norm_g[i, 1], p, rope_mla)
            o_att = _mla_attention(q, k, v, n_ctx=n_ctx)
            gq, gk, gv, gcol, grow = _gdn_prep(zg, small, p["conv_w"], p["neg_a"], p["dt_bias"], ctx_tiles=ctx_tiles)
            o_f, o_b = _gdn_scan(gq, gk, gv, gcol, grow, n_ctx=n_ctx)
            xs = _outproj(xs, mod, o_att, o_f, o_b, gate, p["out_norm"], p["w_out"], seg=GDN_DV, rec_first=False,
                          ctx_tiles=ctx_tiles, skip=skip, att_skip=skip)
        else:
            assert last, "odd layers that must also produce context outputs are not built"
            p = _odd_params(j, n_ctx, od_w_in, od_gla_gate_w2, od_gla_gate_b, od_gla_out_norm, od_swa_q_norm,
                            od_swa_k_norm, od_swa_sink, od_w_out)
            gq, gk, gv, rg, bc, sq, sk, sv = _inproj_odd(xs, mod, norm_g[i, 1], p, rope_swa)
            o_f, o_b = _gla_scan(gq, gk, gv, bc, n_ctx=n_ctx)
            o_att = _swa_attention(sq, sk, sv, p["sink"], n_ctx=n_ctx)
            xs = _outproj(xs, mod, o_att, o_f, o_b, rg, p["out_norm"], p["w_out"], seg=GLA_DV, rec_first=True,
                          ctx_tiles=ctx_tiles, skip=skip, att_skip=0)
        xs = _ffn(xs, mod, norm_g[i, 2], wg[1], wu[1], wd[1], r0=6, ctx_tiles=0 if last else ctx_tiles)
    return xs
```

```python
import functools

import jax
import jax.numpy as jnp
from jax import lax
from jax.experimental import pallas as pl
from jax.experimental.pallas import tpu as pltpu

F32 = jnp.float32
BF16 = jnp.bfloat16

D_MODEL = 1024
GRID_W = 64
D_FF = 2816
FFN_RES = 0.5
N_MOD = 9
EPS = 1e-6
ROPE_THETA = 10000.0
CHUNK = 64

MLA_HEADS = 8
MLA_NOPE = 64
MLA_ROPE = 32
MLA_QK = MLA_NOPE + MLA_ROPE
MLA_V = 64
MLA_Q_RANK = 384
MLA_KV_RANK = 256
MLA_VW = MLA_HEADS * MLA_V

GDN_HEADS = 8
GDN_DK = 64
GDN_DV = 64
GDN_CONV = 5
GDN_QK = GDN_HEADS * GDN_DK
GDN_VW = GDN_HEADS * GDN_DV
GDN_HALF = 256

GLA_HEADS = 4
GLA_DK = 64
GLA_DV = 128
GLA_RANK = 16
GLA_TAU = 16.0
GLA_QK = GLA_HEADS * GLA_DK
GLA_VW = GLA_HEADS * GLA_DV

SWA_HEADS = 8
SWA_KV_HEADS = 2
SWA_DH = 64
SWA_WINDOW = 128
SWA_QW = SWA_HEADS * SWA_DH
SWA_KW = SWA_KV_HEADS * SWA_DH

LANES = 128
HEAD_PAD = 128
TM = 256
VMEM_LIMIT = 56 * 1024 * 1024

EVEN_COLS = MLA_Q_RANK + MLA_KV_RANK + LANES + 2 * GDN_QK + GDN_VW + GDN_VW + LANES
ODD_COLS = 2 * GLA_QK + GLA_VW + GLA_VW + SWA_QW + 2 * SWA_KW + LANES


def _dot(a, b):
    return jnp.dot(a.astype(BF16), b.astype(BF16), preferred_element_type=F32)


def _dot_nt(a, b):
    return lax.dot_general(a.astype(BF16), b.astype(BF16), (((1,), (1,)), ((), ())), preferred_element_type=F32)


def _dot_tn(a, b):
    return lax.dot_general(a.astype(BF16), b.astype(BF16), (((0,), (0,)), ((), ())), preferred_element_type=F32)


def _split2(x):
    hi = x.astype(BF16)
    lo = (x - hi.astype(F32)).astype(BF16)
    return hi, lo


def _split3(x):
    hi = x.astype(BF16)
    r = x - hi.astype(F32)
    mid = r.astype(BF16)
    lo = (r - mid.astype(F32)).astype(BF16)
    return hi, mid, lo


def _dot3(a, b):
    ah, al = _split2(a)
    bh, bl = _split2(b)
    d = functools.partial(jnp.dot, preferred_element_type=F32)
    return d(ah, bh) + (d(ah, bl) + d(al, bh))


def _dot_sel(sel, x):
    s = sel.astype(BF16)
    hi, mid, lo = _split3(x)
    d = functools.partial(jnp.dot, preferred_element_type=F32)
    return d(s, hi) + (d(s, mid) + d(s, lo))


def _dot_sel_r(x, sel):
    s = sel.astype(BF16)
    hi, mid, lo = _split3(x)
    d = functools.partial(jnp.dot, preferred_element_type=F32)
    return d(hi, s) + (d(mid, s) + d(lo, s))


def _seg_mean_sq(x, seg, n_real):
    w = x.shape[-1]
    r = lax.broadcasted_iota(jnp.int32, (w, w), 0) // seg
    c = lax.broadcasted_iota(jnp.int32, (w, w), 1) // seg
    ones_bd = (r == c).astype(BF16)
    hi, lo = _split2(x * x)
    d = functools.partial(jnp.dot, preferred_element_type=F32)
    return (d(hi, ones_bd) + d(lo, ones_bd)) * (1.0 / n_real)


def _rms_rows(x, g):
    ms = jnp.mean(x * x, axis=-1, keepdims=True)
    return x * lax.rsqrt(ms + EPS) * g


def _sigmoid(x):
    return 1.0 / (1.0 + jnp.exp(-x))


def _silu(x):
    return x * _sigmoid(x)


def _softplus(x):
    return jnp.maximum(x, 0.0) + jnp.log1p(jnp.exp(-jnp.abs(x)))


def _rope(x, c, s1, s2, half):
    return x * c + pltpu.roll(x, LANES - half, 1) * s1 + pltpu.roll(x, half, 1) * s2


def _chunk_tri(n, reverse):
    i = lax.broadcasted_iota(jnp.int32, (n, n), 0)
    j = lax.broadcasted_iota(jnp.int32, (n, n), 1)
    same = (i // CHUNK) == (j // CHUNK)
    tri = (j >= i) if reverse else (j <= i)
    return (same & tri).astype(F32)


def _ada_kernel(c_ref, w_ref, b_ref, o_ref):
    sc = _silu(c_ref[...])
    o_ref[...] = _dot(sc, w_ref[...]) + b_ref[...]


def _ada(cvec, ada_w, ada_b):
    depth, d, nm = ada_w.shape
    tn = 1024
    return pl.pallas_call(
        _ada_kernel,
        grid=(depth, nm // tn),
        in_specs=[
            pl.BlockSpec(cvec.shape, lambda i, j: (0, 0)),
            pl.BlockSpec((None, d, tn), lambda i, j: (i, 0, j)),
            pl.BlockSpec((None, 1, tn), lambda i, j: (i, 0, j)),
        ],
        out_specs=pl.BlockSpec((None, cvec.shape[0], tn), lambda i, j: (i, 0, j)),
        out_shape=jax.ShapeDtypeStruct((depth, cvec.shape[0], nm), F32),
        compiler_params=pltpu.CompilerParams(vmem_limit_bytes=VMEM_LIMIT),
        name="ada_mod",
    )(cvec, ada_w, ada_b.reshape(depth, 1, nm))


def _const_spec(shape):
    nd = len(shape)
    return pl.BlockSpec(shape, lambda *_: (0,) * nd, pipeline_mode=pl.Buffered(1))


def _tok_spec(width, skip=0):
    return pl.BlockSpec((None, TM, width), lambda b, t: (b, t + skip, 0))


def _mod_spec(ctx_tiles, skip=0):
    return pl.BlockSpec((None, None, N_MOD, D_MODEL),
                        lambda b, t: (b, jnp.where(t + skip >= ctx_tiles, 1, 0), 0, 0))


def _params():
    return pltpu.CompilerParams(dimension_semantics=("parallel", "arbitrary"), vmem_limit_bytes=VMEM_LIMIT)


def _ffn_kernel(x_ref, mod_ref, g_ref, wg_ref, wu_ref, wd_ref, o_ref, *, r0):
    x = x_ref[...]
    m = mod_ref[...]
    h = _rms_rows(x, g_ref[...]) * (1.0 + m[r0 + 1:r0 + 2]) + m[r0:r0 + 1]
    hb = h.astype(BF16)
    a = jnp.dot(hb, wg_ref[...], preferred_element_type=F32)
    u = jnp.dot(hb, wu_ref[...], preferred_element_type=F32)
    act = (_silu(a) * u).astype(BF16)
    y = jnp.dot(act, wd_ref[...], preferred_element_type=F32)
    o_ref[...] = x + (FFN_RES * m[r0 + 2:r0 + 3]) * y


def _ffn(xs, mod, g, wg, wu, wd, *, r0, ctx_tiles, skip=0):
    b, rows, d = xs.shape
    out_rows = rows - skip * TM
    return pl.pallas_call(
        functools.partial(_ffn_kernel, r0=r0),
        grid=(b, out_rows // TM),
        in_specs=[
            _tok_spec(d, skip), _mod_spec(ctx_tiles, skip), _const_spec((1, d)),
            _const_spec(wg.shape), _const_spec(wu.shape), _const_spec(wd.shape),
        ],
        out_specs=_tok_spec(d),
        out_shape=jax.ShapeDtypeStruct((b, out_rows, d), F32),
        compiler_params=_params(),
        name="ffn_half_step",
    )(xs, mod, g.reshape(1, d), wg, wu, wd)


def _inproj_even_kernel(x_ref, mod_ref, g_ref, win_ref, qan_ref, wq_ref, kvan_ref, wkk_ref, wkv_ref,
                        qn_ref, kn_ref, rc_ref, rs1_ref, rs2_ref,
                        q_ref, k_ref, v_ref, zg_ref, gate_ref, small_ref):
    x = x_ref[...]
    m = mod_ref[...]
    h = _rms_rows(x, g_ref[...]) * (1.0 + m[4:5]) + m[3:4]
    z = jnp.dot(h.astype(BF16), win_ref[...], preferred_element_type=F32)
    o = 0
    cq = z[:, o:o + MLA_Q_RANK]; o += MLA_Q_RANK
    ckv = z[:, o:o + MLA_KV_RANK]; o += MLA_KV_RANK
    kr = z[:, o:o + LANES]; o += LANES
    zg_ref[...] = z[:, o:o + 3 * GDN_QK]; o += 3 * GDN_QK
    gate_ref[...] = z[:, o:o + GDN_VW]; o += GDN_VW
    small_ref[...] = z[:, o:o + LANES]

    rc, rs1, rs2 = rc_ref[...], rs1_ref[...], rs2_ref[...]
    qn, kn = qn_ref[...], kn_ref[...]
    qf = jnp.dot(_rms_rows(cq, qan_ref[...]).astype(BF16), wq_ref[...], preferred_element_type=F32)
    ckvn = _rms_rows(ckv, kvan_ref[...]).astype(BF16)
    kf = jnp.dot(ckvn, wkk_ref[...], preferred_element_type=F32)
    v_ref[...] = jnp.dot(ckvn, wkv_ref[...], preferred_element_type=F32).astype(v_ref.dtype)
    for hd in range(MLA_HEADS):
        sl = slice(hd * HEAD_PAD, (hd + 1) * HEAD_PAD)
        qh = qf[:, sl]
        qh = qh * lax.rsqrt(jnp.sum(qh * qh, axis=-1, keepdims=True) * (1.0 / MLA_QK) + EPS) * qn
        q_ref[:, sl] = _rope(qh, rc, rs1, rs2, MLA_ROPE // 2).astype(q_ref.dtype)
        kh = kf[:, sl] + kr
        kh = kh * lax.rsqrt(jnp.sum(kh * kh, axis=-1, keepdims=True) * (1.0 / MLA_QK) + EPS) * kn
        k_ref[:, sl] = _rope(kh, rc, rs1, rs2, MLA_ROPE // 2).astype(k_ref.dtype)


def _inproj_even(xs, mod, g, p, rope):
    b, rows, d = xs.shape
    ctx_tiles = p["ctx_tiles"]
    consts = [g.reshape(1, d), p["w_in"], p["q_a_norm"], p["w_q"], p["kv_a_norm"], p["w_kk"], p["w_kv"],
              p["q_norm"], p["k_norm"]]
    rope_spec = pl.BlockSpec((TM, LANES), lambda b_, t: (t, 0))
    widths = [(MLA_HEADS * HEAD_PAD, BF16), (MLA_HEADS * HEAD_PAD, BF16), (MLA_VW, BF16),
              (3 * GDN_QK, F32), (GDN_VW, F32), (LANES, F32)]
    return pl.pallas_call(
        _inproj_even_kernel,
        grid=(b, rows // TM),
        in_specs=[_tok_spec(d), _mod_spec(ctx_tiles)] + [_const_spec(c.shape) for c in consts] + [rope_spec] * 3,
        out_specs=[_tok_spec(w) for w, _ in widths],
        out_shape=[jax.ShapeDtypeStruct((b, rows, w), dt) for w, dt in widths],
        compiler_params=_params(),
        name="inproj_even",
    )(xs, mod, *consts, *rope)


def _mla_kernel(q_ref, k_ref, v_ref, o_ref, *, n_ctx, n_all, ctx_tiles):
    t = pl.program_id(2)
    scale = MLA_QK ** -0.5
    lo = lax.broadcasted_iota(jnp.int32, (TM, LANES), 1) < MLA_V

    def attend(nk):
        q = q_ref[...]
        v = v_ref[0:nk, :]
        outs = []
        for hh in range(2):
            sl = slice(hh * HEAD_PAD, (hh + 1) * HEAD_PAD)
            s = lax.dot_general(q[:, sl], k_ref[0:nk, sl], (((1,), (1,)), ((), ())),
                                preferred_element_type=F32) * scale
            p = jnp.exp(s - jnp.max(s, axis=-1, keepdims=True))
            den = jnp.sum(p, axis=-1, keepdims=True)
            outs.append(jnp.dot(p.astype(BF16), v, preferred_element_type=F32) / den)
        o_ref[...] = jnp.where(lo, outs[0], outs[1]).astype(o_ref.dtype)

    @pl.when(t < ctx_tiles)
    def _():
        attend(n_ctx)

    @pl.when(t >= ctx_tiles)
    def _():
        attend(n_all)


def _mla_attention(q, k, v, *, n_ctx):
    b, rows, _ = q.shape
    pairs = MLA_HEADS // 2
    return pl.pallas_call(
        functools.partial(_mla_kernel, n_ctx=n_ctx, n_all=rows, ctx_tiles=n_ctx // TM),
        grid=(b, pairs, rows // TM),
        in_specs=[
            pl.BlockSpec((None, TM, 2 * HEAD_PAD), lambda b_, h, t: (b_, t, h)),
            pl.BlockSpec((None, rows, 2 * HEAD_PAD), lambda b_, h, t: (b_, 0, h)),
            pl.BlockSpec((None, rows, 2 * MLA_V), lambda b_, h, t: (b_, 0, h)),
        ],
        out_specs=pl.BlockSpec((None, TM, 2 * MLA_V), lambda b_, h, t: (b_, t, h)),
        out_shape=jax.ShapeDtypeStruct((b, rows, MLA_VW), BF16),
        compiler_params=pltpu.CompilerParams(dimension_semantics=("parallel", "parallel", "arbitrary"),
                                             vmem_limit_bytes=VMEM_LIMIT),
        name="mla_attention",
    )(q, k, v)


def _gdn_prep_kernel(z_ref, zp_ref, zn_ref, cw_ref, sm_ref, nega_ref, dtb_ref,
                     q_ref, k_ref, v_ref, gf_ref, gb_ref, *, ctx_tiles, n_tiles):
    t = pl.program_id(1)
    first = (t == 0) | (t == ctx_tiles)
    last = (t == ctx_tiles - 1) | (t == n_tiles - 1)
    z = z_ref[...]
    half = GDN_CONV // 2
    prev = jnp.where(first, 0.0, zp_ref[...][8 - half:, :])
    nxt = jnp.where(last, 0.0, zn_ref[...][:half, :])
    ext = jnp.concatenate([prev, z, nxt], axis=0)
    cw = cw_ref[...]
    acc = ext[0:TM] * cw[0:1]
    for j in range(1, GDN_CONV):
        acc = acc + ext[j:j + TM] * cw[j:j + 1]
    qkv = _silu(acc)
    q = qkv[:, :GDN_QK]
    k = qkv[:, GDN_QK:2 * GDN_QK]
    q_ref[...] = q * lax.rsqrt(_seg_mean_sq(q, GDN_DK, 1.0) + EPS) * (GDN_DK ** -0.5)
    k_ref[...] = k * lax.rsqrt(_seg_mean_sq(k, GDN_DK, 1.0) + EPS)
    v_ref[...] = qkv[:, 2 * GDN_QK:]

    sm = sm_ref[...]
    lane = lax.broadcasted_iota(jnp.int32, sm.shape, 1)
    g = nega_ref[...] * _softplus(sm + dtb_ref[...])
    g = jnp.where(lane < 2 * GDN_HEADS, g, 0.0)
    gc_f = _dot_sel(_chunk_tri(TM, False), g)
    gc_b = _dot_sel(_chunk_tri(TM, True), g)
    gates = jnp.where(lane < GDN_HEADS, gc_f, jnp.where(lane < 2 * GDN_HEADS, gc_b, _sigmoid(sm)))
    row = lax.broadcasted_iota(jnp.int32, (LANES, GDN_QK), 0)
    head = lax.broadcasted_iota(jnp.int32, (LANES, GDN_QK), 1) // GDN_DK
    for d, ref in enumerate((gf_ref, gb_ref)):
        ref[:, 0:GDN_QK] = _dot_sel_r(gates, (row == head + d * GDN_HEADS).astype(F32))
        ref[:, GDN_QK:2 * GDN_QK] = _dot_sel_r(gates, (row == head + (2 + d) * GDN_HEADS).astype(F32))


def _gdn_prep(zg, small, conv_w, nega, dtb, *, ctx_tiles):
    b, rows, w = zg.shape
    n_tiles = rows // TM
    hb = TM // 8
    n_hblk = rows // 8
    return pl.pallas_call(
        functools.partial(_gdn_prep_kernel, ctx_tiles=ctx_tiles, n_tiles=n_tiles),
        grid=(b, n_tiles),
        in_specs=[
            _tok_spec(w),
            pl.BlockSpec((None, 8, w), lambda b_, t: (b_, jnp.maximum(t * hb - 1, 0), 0)),
            pl.BlockSpec((None, 8, w), lambda b_, t: (b_, jnp.minimum((t + 1) * hb, n_hblk - 1), 0)),
            _const_spec(conv_w.shape), _tok_spec(LANES), _const_spec((1, LANES)), _const_spec((1, LANES)),
        ],
        out_specs=[_tok_spec(GDN_QK), _tok_spec(GDN_QK), _tok_spec(GDN_VW), _tok_spec(2 * GDN_QK),
                   _tok_spec(2 * GDN_QK)],
        out_shape=[jax.ShapeDtypeStruct((b, rows, GDN_QK), F32), jax.ShapeDtypeStruct((b, rows, GDN_QK), F32),
                   jax.ShapeDtypeStruct((b, rows, GDN_VW), F32), jax.ShapeDtypeStruct((b, rows, 2 * GDN_QK), F32),
                   jax.ShapeDtypeStruct((b, rows, 2 * GDN_QK), F32)],
        compiler_params=_params(),
        name="gdn_prep",
    )(zg, zg, zg, conv_w, small, nega, dtb)


def _bd_halves(y, bdm):
    yb = y.astype(BF16)
    out = []
    for s in range(2):
        t = jnp.concatenate([yb[:, s * GDN_HALF:(s + 1) * GDN_HALF]] * (GDN_HALF // GDN_DK), axis=0)
        out.append(jnp.where(bdm, t, jnp.zeros_like(t)))
    return out


def _hprod(x, bd):
    xb = x.astype(BF16)
    d = functools.partial(jnp.dot, preferred_element_type=F32)
    return jnp.concatenate([d(xb[:, 0:GDN_HALF], bd[0]), d(xb[:, GDN_HALF:], bd[1])], axis=1)


def _hprod3(x, y, bdm):
    xh, xl = _split2(x)
    yh, yl = _split2(y)
    bh, bl = _bd_halves(yh, bdm), _bd_halves(yl, bdm)
    return _hprod(xh, bh) + (_hprod(xh, bl) + _hprod(xl, bh))


def _gdn_chunk(q, k, v, gexp, s_ref, reverse):
    ii = lax.broadcasted_iota(jnp.int32, (CHUNK, GDN_QK), 0)
    jj = lax.broadcasted_iota(jnp.int32, (CHUNK, GDN_QK), 1) % CHUNK
    r = lax.broadcasted_iota(jnp.int32, (GDN_HALF, GDN_HALF), 0) // GDN_DK
    c = lax.broadcasted_iota(jnp.int32, (GDN_HALF, GDN_HALF), 1) // GDN_DK
    bdm = r == c
    incl = (ii <= jj) if reverse else (ii >= jj)
    strict = (ii < jj) if reverse else (ii > jj)
    diag = ii == jj
    last = 0 if reverse else CHUNK - 1
    gc = gexp[:, 0:GDN_QK]
    beta = gexp[:, GDN_QK:2 * GDN_QK]
    gc_row = jnp.sum(jnp.where(diag, gc, 0.0), axis=0, keepdims=True)
    decay = jnp.exp(jnp.where(incl, gc - gc_row, -jnp.inf))

    kt = k.T.astype(BF16)
    rk = []
    for s in range(2):
        t = jnp.concatenate([kt[s * GDN_HALF:(s + 1) * GDN_HALF]] * (GDN_HALF // CHUNK), axis=1)
        rk.append(jnp.where(bdm, t, jnp.zeros_like(t)))
    kkqk = _hprod(jnp.concatenate([k, q], axis=0), rk)
    a = jnp.where(strict, beta * kkqk[0:CHUNK] * decay, 0.0)
    qk = kkqk[CHUNK:2 * CHUNK] * decay

    def level_mask(blk):
        same = (ii // (2 * blk)) == (jj // (2 * blk))
        off = (ii // blk) < (jj // blk) if reverse else (ii // blk) > (jj // blk)
        return same & off

    eye = diag.astype(F32)
    m = eye - jnp.where(level_mask(1), a, 0.0)
    blk = 2
    while blk < CHUNK:
        x = _hprod(jnp.where(level_mask(blk), a, 0.0), _bd_halves(m, bdm))
        m = m - _hprod(m, _bd_halves(x, bdm))
        blk *= 2
    resid = eye - m - _hprod3(a, m, bdm)
    m = m + _hprod(m, _bd_halves(resid, bdm))

    egc = jnp.exp(gc)
    u = _hprod(m, _bd_halves(v * beta, bdm))
    w = _hprod(m, _bd_halves(k * (beta * egc), bdm))
    g_last = gc[last:last + 1, :]
    s0, s1 = s_ref[0], s_ref[1]

    def sprod(x):
        return jnp.concatenate([_dot(x[:, 0:GDN_HALF], s0), _dot(x[:, GDN_HALF:], s1)], axis=1)

    v_new = u - sprod(w)
    o = sprod(q * egc) + _hprod(qk, _bd_halves(v_new, bdm))
    k_dec = k * jnp.exp(g_last - gc)
    e_last = jnp.exp(g_last)
    for s, st in enumerate((s0, s1)):
        sl = slice(s * GDN_HALF, (s + 1) * GDN_HALF)
        s_ref[s] = st * e_last[:, sl] + jnp.where(bdm, _dot_tn(k_dec[:, sl], v_new[:, sl]), 0.0)
    return o


def _gdn_scan_kernel(qf_ref, kf_ref, vf_ref, gf_ref, qb_ref, kb_ref, vb_ref, gb_ref, of_ref, ob_ref, sf_ref, sb_ref):
    @pl.when(pl.program_id(1) == 0)
    def _():
        sf_ref[...] = jnp.zeros_like(sf_ref)
        sb_ref[...] = jnp.zeros_like(sb_ref)

    of_ref[...] = _gdn_chunk(qf_ref[...], kf_ref[...], vf_ref[...], gf_ref[...], sf_ref, False)
    ob_ref[...] = _gdn_chunk(qb_ref[...], kb_ref[...], vb_ref[...], gb_ref[...], sb_ref, True)


def _scan_chunk_maps(ctx_chunks, n_chunks):
    fwd = lambda s: s
    bwd = lambda s: jnp.where(s < ctx_chunks, ctx_chunks - 1 - s, n_chunks - 1 - (s - ctx_chunks))
    return fwd, bwd


def _gdn_scan(q, k, v, g_f, g_b, *, n_ctx):
    b, rows, _ = q.shape
    n_chunks = rows // CHUNK
    fwd, bwd = _scan_chunk_maps(n_ctx // CHUNK, n_chunks)

    def specs(cm):
        tok = lambda w: pl.BlockSpec((None, CHUNK, w), lambda b_, s: (b_, cm(s), 0))
        return [tok(GDN_QK), tok(GDN_QK), tok(GDN_VW), tok(2 * GDN_QK)]

    return pl.pallas_call(
        _gdn_scan_kernel,
        grid=(b, n_chunks),
        in_specs=specs(fwd) + specs(bwd),
        out_specs=[pl.BlockSpec((None, CHUNK, GDN_VW), lambda b_, s: (b_, fwd(s), 0)),
                   pl.BlockSpec((None, CHUNK, GDN_VW), lambda b_, s: (b_, bwd(s), 0))],
        out_shape=[jax.ShapeDtypeStruct((b, rows, GDN_VW), F32)] * 2,
        scratch_shapes=[pltpu.VMEM((2, GDN_HALF, GDN_HALF), F32)] * 2,
        compiler_params=_params(),
        name="gdn_scan",
    )(q, k, v, g_f, q, k, v, g_b)


def _outproj_kernel(x_ref, mod_ref, oa_ref, of_ref, ob_ref, gate_ref, gn_ref, w_ref, o_ref, *, seg, rec_first):
    o = of_ref[...] + ob_ref[...]
    y = o * lax.rsqrt(_seg_mean_sq(o, seg, float(seg)) + EPS) * gn_ref[...] * _silu(gate_ref[...])
    wr = y.shape[-1]
    wa = oa_ref.shape[-1]
    if rec_first:
        out = jnp.dot(y.astype(BF16), w_ref[0:wr, :], preferred_element_type=F32)
        out = out + jnp.dot(oa_ref[...], w_ref[wr:wr + wa, :], preferred_element_type=F32)
    else:
        out = jnp.dot(oa_ref[...], w_ref[0:wa, :], preferred_element_type=F32)
        out = out + jnp.dot(y.astype(BF16), w_ref[wa:wa + wr, :], preferred_element_type=F32)
    o_ref[...] = x_ref[...] + mod_ref[...][5:6] * out


def _outproj(xs, mod, o_att, o_f, o_b, gate, gn, w_out, *, seg, rec_first, ctx_tiles, skip=0, att_skip=0):
    b, rows, d = xs.shape
    out_rows = rows - skip * TM
    return pl.pallas_call(
        functools.partial(_outproj_kernel, seg=seg, rec_first=rec_first),
        grid=(b, out_rows // TM),
        in_specs=[_tok_spec(d, skip), _mod_spec(ctx_tiles, skip), _tok_spec(o_att.shape[-1], att_skip),
                  _tok_spec(o_f.shape[-1], skip), _tok_spec(o_b.shape[-1], skip), _tok_spec(gate.shape[-1], skip),
                  _const_spec(gn.shape), _const_spec(w_out.shape)],
        out_specs=_tok_spec(d),
        out_shape=jax.ShapeDtypeStruct((b, out_rows, d), F32),
        compiler_params=_params(),
        name="outproj",
    )(xs, mod, o_att, o_f, o_b, gate, gn, w_out)


def _inproj_odd_kernel(x_ref, mod_ref, g_ref, win_ref, w2_ref, b2_ref, qn_ref, kn_ref, rc_ref, rs1_ref, rs2_ref,
                       gq_ref, gk_ref, gv_ref, rg_ref, bc_ref, sq_ref, sk_ref, sv_ref):
    x = x_ref[...]
    m = mod_ref[...]
    h = _rms_rows(x, g_ref[...]) * (1.0 + m[4:5]) + m[3:4]
    z = jnp.dot(h.astype(BF16), win_ref[...], preferred_element_type=F32)
    o = 0
    gq_ref[...] = z[:, o:o + GLA_QK]; o += GLA_QK
    gk_ref[...] = z[:, o:o + GLA_QK]; o += GLA_QK
    gv_ref[...] = z[:, o:o + GLA_VW]; o += GLA_VW
    rg_ref[...] = z[:, o:o + GLA_VW]; o += GLA_VW
    sq = z[:, o:o + SWA_QW]; o += SWA_QW
    sk = z[:, o:o + SWA_KW]; o += SWA_KW
    sv_ref[...] = z[:, o:o + SWA_KW].astype(sv_ref.dtype); o += SWA_KW
    lowrank = z[:, o:o + LANES]

    logit = _dot(lowrank, w2_ref[...]) + b2_ref[...]
    log_a = (jnp.minimum(logit, 0.0) - jnp.log1p(jnp.exp(-jnp.abs(logit)))) * (1.0 / GLA_TAU)
    bc_ref[:, 0:GLA_QK] = _dot_sel(_chunk_tri(TM, False), log_a[:, 0:GLA_QK])
    bc_ref[:, GLA_QK:2 * GLA_QK] = _dot_sel(_chunk_tri(TM, True), log_a[:, GLA_QK:2 * GLA_QK])

    rc, rs1, rs2 = rc_ref[...], rs1_ref[...], rs2_ref[...]
    sqn = sq * lax.rsqrt(_seg_mean_sq(sq, SWA_DH, float(SWA_DH)) + EPS) * qn_ref[...]
    for s in range(SWA_QW // LANES):
        sl = slice(s * LANES, (s + 1) * LANES)
        sq_ref[:, sl] = _rope(sqn[:, sl], rc, rs1, rs2, SWA_DH // 2).astype(sq_ref.dtype)
    skn = sk * lax.rsqrt(_seg_mean_sq(sk, SWA_DH, float(SWA_DH)) + EPS) * kn_ref[...]
    sk_ref[...] = _rope(skn, rc, rs1, rs2, SWA_DH // 2).astype(sk_ref.dtype)


def _inproj_odd(xs, mod, g, p, rope):
    b, rows, d = xs.shape
    consts = [g.reshape(1, d), p["w_in"], p["w2"], p["b2"], p["q_norm"], p["k_norm"]]
    rope_spec = pl.BlockSpec((TM, LANES), lambda b_, t: (t, 0))
    widths = [(GLA_QK, F32), (GLA_QK, F32), (GLA_VW, F32), (GLA_VW, F32), (2 * GLA_QK, F32),
              (SWA_QW, BF16), (SWA_KW, BF16), (SWA_KW, BF16)]
    return pl.pallas_call(
        _inproj_odd_kernel,
        grid=(b, rows // TM),
        in_specs=[_tok_spec(d), _mod_spec(p["ctx_tiles"])] + [_const_spec(c.shape) for c in consts] + [rope_spec] * 3,
        out_specs=[_tok_spec(w) for w, _ in widths],
        out_shape=[jax.ShapeDtypeStruct((b, rows, w), dt) for w, dt in widths],
        compiler_params=_params(),
        name="inproj_odd",
    )(xs, mod, *consts, *rope)


def _gla_chunk(q, k, v, bc, st_ref, reverse):
    ii = lax.broadcasted_iota(jnp.int32, (CHUNK, CHUNK), 0)
    jj = lax.broadcasted_iota(jnp.int32, (CHUNK, CHUNK), 1)
    incl = (ii <= jj) if reverse else (ii >= jj)
    last = 0 if reverse else CHUNK - 1
    q_dec = (q * (GLA_DK ** -0.5)) * jnp.exp(bc)
    k_inv = k * jnp.exp(-bc)
    b_last = bc[last:last + 1, :]
    k_dec = k * jnp.exp(b_last - bc)
    e_last = jnp.exp(b_last)
    head = lax.broadcasted_iota(jnp.int32, (CHUNK, GLA_QK), 1) // GLA_DK
    outs = []
    for h in range(GLA_HEADS):
        hm = head == h
        attn = jnp.where(incl, _dot_nt(q_dec, jnp.where(hm, k_inv, 0.0)), 0.0)
        vh = v[:, h * GLA_DV:(h + 1) * GLA_DV]
        st = st_ref[h]
        outs.append(_dot(attn, vh) + _dot_nt(q_dec, st))
        st_ref[h] = st * e_last + _dot_tn(vh, jnp.where(hm, k_dec, 0.0))
    return jnp.concatenate(outs, axis=-1)


def _gla_scan_kernel(qf_ref, kf_ref, vf_ref, bf_ref, qb_ref, kb_ref, vb_ref, bb_ref, of_ref, ob_ref, sf_ref, sb_ref):
    @pl.when(pl.program_id(1) == 0)
    def _():
        sf_ref[...] = jnp.zeros_like(sf_ref)
        sb_ref[...] = jnp.zeros_like(sb_ref)

    of_ref[...] = _gla_chunk(qf_ref[...], kf_ref[...], vf_ref[...], bf_ref[...][:, 0:GLA_QK], sf_ref, False)
    ob_ref[...] = _gla_chunk(qb_ref[...], kb_ref[...], vb_ref[...], bb_ref[...][:, GLA_QK:2 * GLA_QK], sb_ref, True)


def _gla_scan(q, k, v, bc, *, n_ctx):
    b, rows, _ = q.shape
    n_chunks = rows // CHUNK
    fwd, bwd = _scan_chunk_maps(n_ctx // CHUNK, n_chunks)

    def specs(cm):
        tok = lambda w: pl.BlockSpec((None, CHUNK, w), lambda b_, s: (b_, cm(s), 0))
        return [tok(GLA_QK), tok(GLA_QK), tok(GLA_VW), tok(2 * GLA_QK)]

    return pl.pallas_call(
        _gla_scan_kernel,
        grid=(b, n_chunks),
        in_specs=specs(fwd) + specs(bwd),
        out_specs=[pl.BlockSpec((None, CHUNK, GLA_VW), lambda b_, s: (b_, fwd(s), 0)),
                   pl.BlockSpec((None, CHUNK, GLA_VW), lambda b_, s: (b_, bwd(s), 0))],
        out_shape=[jax.ShapeDtypeStruct((b, rows, GLA_VW), F32)] * 2,
        scratch_shapes=[pltpu.VMEM((GLA_HEADS, GLA_DV, GLA_QK), F32)] * 2,
        compiler_params=_params(),
        name="gla_scan",
    )(q, k, v, bc, q, k, v, bc)


def _swa_kernel(q_ref, k_ref, v_ref, sink_ref, o_ref, *, n_ctx, n_lat):
    w = SWA_WINDOW
    n = pl.program_id(1)
    nb = n_lat // w
    scale = SWA_DH ** -0.5

    def rows(ref, blk):
        return ref[pl.ds(pl.multiple_of(n_ctx + blk * w, w), w), :]

    pb = jnp.maximum(n - 1, 0)
    xb = jnp.minimum(n + 1, nb - 1)
    k_cat = jnp.concatenate([rows(k_ref, pb), rows(k_ref, n), rows(k_ref, xb), k_ref[0:n_ctx, :]], axis=0)
    v_cat = jnp.concatenate([rows(v_ref, pb), rows(v_ref, n), rows(v_ref, xb), v_ref[0:n_ctx, :]], axis=0)
    ii = lax.broadcasted_iota(jnp.int32, (w, w), 0)
    jj = lax.broadcasted_iota(jnp.int32, (w, w), 1)
    ninf = -jnp.inf
    bias = jnp.concatenate([
        jnp.where((jj >= ii) & (n > 0), 0.0, ninf),
        jnp.zeros((w, w), F32),
        jnp.where((jj <= ii) & (n < nb - 1), 0.0, ninf),
        jnp.zeros((w, n_ctx), F32)], axis=1)
    lo = lax.broadcasted_iota(jnp.int32, (w, LANES), 1) < SWA_DH
    sink = sink_ref[...]
    q = q_ref[...]
    half = SWA_HEADS // 2
    for j in range(half):
        slab = q[:, j * LANES:(j + 1) * LANES]
        res = []
        for g in range(SWA_KV_HEADS):
            hd = j + half * g
            qm = jnp.where(lo if g == 0 else jnp.logical_not(lo), slab, jnp.zeros_like(slab))
            s = lax.dot_general(qm, k_cat, (((1,), (1,)), ((), ())), preferred_element_type=F32) * scale + bias
            sk = sink[0:1, hd:hd + 1]
            mx = jnp.maximum(jnp.max(s, axis=-1, keepdims=True), sk)
            p = jnp.exp(s - mx)
            den = jnp.sum(p, axis=-1, keepdims=True) + jnp.exp(sk - mx)
            res.append(jnp.dot(p.astype(BF16), v_cat, preferred_element_type=F32) / den)
        o_ref[:, j * LANES:(j + 1) * LANES] = jnp.where(lo, res[0], res[1]).astype(o_ref.dtype)


def _swa_attention(q, k, v, sink, *, n_ctx):
    b, rows, _ = q.shape
    n_lat = rows - n_ctx
    w = SWA_WINDOW
    skip = n_ctx // w
    return pl.pallas_call(
        functools.partial(_swa_kernel, n_ctx=n_ctx, n_lat=n_lat),
        grid=(b, n_lat // w),
        in_specs=[
            pl.BlockSpec((None, w, SWA_QW), lambda b_, n: (b_, n + skip, 0)),
            pl.BlockSpec((None, rows, SWA_KW), lambda b_, n: (b_, 0, 0)),
            pl.BlockSpec((None, rows, SWA_KW), lambda b_, n: (b_, 0, 0)),
            _const_spec((1, LANES)),
        ],
        out_specs=pl.BlockSpec((None, w, SWA_QW), lambda b_, n: (b_, n, 0)),
        out_shape=jax.ShapeDtypeStruct((b, n_lat, SWA_QW), BF16),
        compiler_params=_params(),
        name="swa_attention",
    )(q, k, v, sink)


def _rope_tables(n_lat, n_ctx, rot_dim):
    t = jnp.arange(n_lat)
    row = (t // GRID_W).astype(F32)
    col = (t % GRID_W).astype(F32)
    n_freq = rot_dim // 4
    inv = ROPE_THETA ** (-jnp.arange(n_freq, dtype=F32) / n_freq)
    ang = jnp.concatenate([row[:, None] * inv, col[:, None] * inv], axis=-1)
    half = rot_dim // 2
    cos = jnp.concatenate([jnp.ones((n_ctx, half), F32), jnp.cos(ang)], axis=0)
    sin = jnp.concatenate([jnp.zeros((n_ctx, half), F32), jnp.sin(ang)], axis=0)
    rows = n_ctx + n_lat
    one = lambda w: jnp.ones((rows, w), F32)
    zero = lambda w: jnp.zeros((rows, w), F32)
    if rot_dim == MLA_ROPE:
        c = jnp.concatenate([one(MLA_NOPE), cos, cos, one(HEAD_PAD - MLA_QK)], axis=1)
        s1 = jnp.concatenate([zero(MLA_NOPE), -sin, zero(half), zero(HEAD_PAD - MLA_QK)], axis=1)
        s2 = jnp.concatenate([zero(MLA_NOPE), zero(half), sin, zero(HEAD_PAD - MLA_QK)], axis=1)
    else:
        c = jnp.concatenate([cos, cos, cos, cos], axis=1)
        s1 = jnp.concatenate([-sin, zero(half), -sin, zero(half)], axis=1)
        s2 = jnp.concatenate([zero(half), sin, zero(half), sin], axis=1)
    return c, s1, s2


def _even_params(j, n_ctx, ev_w_in, ev_q_a_norm, ev_w_q_up, ev_kv_a_norm, ev_w_kv_up, ev_mla_q_norm, ev_mla_k_norm,
                 ev_gdn_conv, ev_gdn_a_log, ev_gdn_dt_bias, ev_gdn_out_norm, ev_w_out):
    w = ev_w_in[j]
    d = w.shape[0]
    z = lambda n: jnp.zeros((d, n), F32)
    o_kr = MLA_Q_RANK + MLA_KV_RANK
    o_g = o_kr + MLA_ROPE
    o_small = o_g + 3 * GDN_QK
    o_gate = o_small + 4 * GDN_HEADS
    w_in = jnp.concatenate([
        w[:, :o_kr], z(MLA_NOPE), w[:, o_kr:o_g], z(HEAD_PAD - MLA_QK),
        w[:, o_g:o_small], w[:, o_gate:o_gate + GDN_VW],
        w[:, o_small:o_gate], z(LANES - 4 * GDN_HEADS)], axis=1).astype(BF16)
    pad_h = HEAD_PAD - MLA_QK
    w_q = jnp.pad(ev_w_q_up[j].reshape(MLA_Q_RANK, MLA_HEADS, MLA_QK), ((0, 0), (0, 0), (0, pad_h)))
    wkv = ev_w_kv_up[j].reshape(MLA_KV_RANK, MLA_HEADS, MLA_NOPE + MLA_V)
    w_kk = jnp.pad(wkv[:, :, :MLA_NOPE], ((0, 0), (0, 0), (0, HEAD_PAD - MLA_NOPE)))
    lane_row = lambda vec: jnp.pad(vec, (0, LANES - vec.shape[0])).reshape(1, LANES)
    fb = lambda a: jnp.concatenate([a[0], a[1]])
    return {
        "ctx_tiles": n_ctx // TM,
        "w_in": w_in,
        "q_a_norm": ev_q_a_norm[j].reshape(1, -1),
        "w_q": w_q.reshape(MLA_Q_RANK, MLA_HEADS * HEAD_PAD).astype(BF16),
        "kv_a_norm": ev_kv_a_norm[j].reshape(1, -1),
        "w_kk": w_kk.reshape(MLA_KV_RANK, MLA_HEADS * HEAD_PAD).astype(BF16),
        "w_kv": wkv[:, :, MLA_NOPE:].reshape(MLA_KV_RANK, MLA_VW).astype(BF16),
        "q_norm": lane_row(ev_mla_q_norm[j]),
        "k_norm": lane_row(ev_mla_k_norm[j]),
        "conv_w": ev_gdn_conv[j],
        "neg_a": lane_row(-jnp.exp(fb(ev_gdn_a_log[j]))),
        "dt_bias": lane_row(fb(ev_gdn_dt_bias[j])),
        "out_norm": jnp.tile(ev_gdn_out_norm[j], GDN_HEADS).reshape(1, GDN_VW),
        "w_out": ev_w_out[j].astype(BF16),
    }


def _swa_head_perm():
    half = SWA_HEADS // 2
    heads = [h for j in range(half) for h in (j, half + j)]
    return jnp.concatenate([jnp.arange(SWA_DH) + h * SWA_DH for h in heads])


def _odd_params(j, n_ctx, od_w_in, od_gla_gate_w2, od_gla_gate_b, od_gla_out_norm, od_swa_q_norm, od_swa_k_norm,
                od_swa_sink, od_w_out):
    w = od_w_in[j]
    d = w.shape[0]
    o_gate = 2 * GLA_QK + GLA_VW
    o_rg = o_gate + 2 * GLA_RANK
    o_sq = o_rg + GLA_VW
    o_sk = o_sq + SWA_QW
    perm = _swa_head_perm()
    w_in = jnp.concatenate([
        w[:, :o_gate], w[:, o_rg:o_sq], w[:, o_sq:o_sk][:, perm], w[:, o_sk:],
        w[:, o_gate:o_rg], jnp.zeros((d, LANES - 2 * GLA_RANK), F32)], axis=1).astype(BF16)
    w2 = jnp.zeros((LANES, 2 * GLA_QK), F32)
    w2 = w2.at[0:GLA_RANK, 0:GLA_QK].set(od_gla_gate_w2[j, 0])
    w2 = w2.at[GLA_RANK:2 * GLA_RANK, GLA_QK:].set(od_gla_gate_w2[j, 1])
    wo = od_w_out[j]
    w_out = jnp.concatenate([wo[:GLA_VW], wo[GLA_VW:][perm]], axis=0).astype(BF16)
    return {
        "ctx_tiles": n_ctx // TM,
        "w_in": w_in,
        "w2": w2.astype(BF16),
        "b2": jnp.concatenate([od_gla_gate_b[j, 0], od_gla_gate_b[j, 1]]).reshape(1, 2 * GLA_QK),
        "q_norm": jnp.tile(od_swa_q_norm[j], SWA_HEADS).reshape(1, SWA_QW),
        "k_norm": jnp.tile(od_swa_k_norm[j], SWA_KV_HEADS).reshape(1, SWA_KW),
        "sink": jnp.pad(od_swa_sink[j], (0, LANES - SWA_HEADS)).reshape(1, LANES),
        "out_norm": jnp.tile(od_gla_out_norm[j], GLA_HEADS).reshape(1, GLA_VW),
        "w_out": w_out,
    }


def kernel(x, c, ctx, c_ctx, ada_w, ada_b, norm_g, ffn_w_gate, ffn_w_up, ffn_w_down, ev_w_in, ev_q_a_norm, ev_w_q_up, ev_kv_a_norm, ev_w_kv_up, ev_mla_q_norm, ev_mla_k_norm, ev_gdn_conv, ev_gdn_a_log, ev_gdn_dt_bias, ev_gdn_out_norm, ev_w_out, od_w_in, od_gla_gate_w2, od_gla_gate_b, od_gla_out_norm, od_swa_q_norm, od_swa_k_norm, od_swa_sink, od_w_out):
    b, n_lat, d = x.shape
    n_ctx = ctx.shape[1]
    depth = ada_w.shape[0]
    assert d == D_MODEL and n_ctx % TM == 0 and n_lat % TM == 0 and n_lat % GRID_W == 0
    assert depth % 2 == 0, "the last layer must be an odd (GLA/SWA) layer: context outputs of that mixer are not built"
    ctx_tiles = n_ctx // TM

    cvec = jnp.concatenate([c, c_ctx[None, :], jnp.zeros((16 - b - 1, d), F32)], axis=0)
    mod_all = _ada(cvec, ada_w, ada_b).reshape(depth, 16, N_MOD, d)
    rope_mla = _rope_tables(n_lat, n_ctx, MLA_ROPE)
    rope_swa = _rope_tables(n_lat, n_ctx, SWA_DH)

    xs = jnp.concatenate([ctx, x], axis=1)
    for i in range(depth):
        last = i == depth - 1
        mod = jnp.stack([jnp.broadcast_to(mod_all[i, b][None], (b, N_MOD, d)), mod_all[i, :b]], axis=1)
        wg, wu, wd = (t.astype(BF16) for t in (ffn_w_gate[i], ffn_w_up[i], ffn_w_down[i]))
        xs = _ffn(xs, mod, norm_g[i, 0], wg[0], wu[0], wd[0], r0=0, ctx_tiles=ctx_tiles)
        j = i // 2
        skip = ctx_tiles if last else 0
        if i % 2 == 0:
            p = _even_params(j, n_ctx, ev_w_in, ev_q_a_norm, ev_w_q_up, ev_kv_a_norm, ev_w_kv_up, ev_mla_q_norm,
                             ev_mla_k_norm, ev_gdn_conv, ev_gdn_a_log, ev_gdn_dt_bias, ev_gdn_out_norm, ev_w_out)
            q, k, v, zg, gate, small = _inproj_even(xs, mod, norm_g[i, 1], p, rope_mla)
            o_att = _mla_attention(q, k, v, n_ctx=n_ctx)
            gq, gk, gv, g_f, g_b = _gdn_prep(zg, small, p["conv_w"], p["neg_a"], p["dt_bias"], ctx_tiles=ctx_tiles)
            o_f, o_b = _gdn_scan(gq, gk, gv, g_f, g_b, n_ctx=n_ctx)
            xs = _outproj(xs, mod, o_att, o_f, o_b, gate, p["out_norm"], p["w_out"], seg=GDN_DV, rec_first=False,
                          ctx_tiles=ctx_tiles, skip=skip, att_skip=skip)
        else:
            assert last, "odd layers that must also produce context outputs are not built"
            p = _odd_params(j, n_ctx, od_w_in, od_gla_gate_w2, od_gla_gate_b, od_gla_out_norm, od_swa_q_norm,
                            od_swa_k_norm, od_swa_sink, od_w_out)
            gq, gk, gv, rg, bc, sq, sk, sv = _inproj_odd(xs, mod, norm_g[i, 1], p, rope_swa)
            o_f, o_b = _gla_scan(gq, gk, gv, bc, n_ctx=n_ctx)
            o_att = _swa_attention(sq, sk, sv, p["sink"], n_ctx=n_ctx)
            xs = _outproj(xs, mod, o_att, o_f, o_b, rg, p["out_norm"], p["w_out"], seg=GLA_DV, rec_first=True,
                          ctx_tiles=ctx_tiles, skip=skip, att_skip=0)
        xs = _ffn(xs, mod, norm_g[i, 2], wg[1], wu[1], wd[1], r0=6, ctx_tiles=0 if last else ctx_tiles)
    return xs
```

```python
import functools

import jax
import jax.numpy as jnp
from jax import lax
from jax.experimental import pallas as pl
from jax.experimental.pallas import tpu as pltpu

F32 = jnp.float32
BF16 = jnp.bfloat16

D_MODEL = 1024
GRID_W = 64
D_FF = 2816
FFN_RES = 0.5
N_MOD = 9
EPS = 1e-6
ROPE_THETA = 10000.0
CHUNK = 64

MLA_HEADS = 8
MLA_NOPE = 64
MLA_ROPE = 32
MLA_QK = MLA_NOPE + MLA_ROPE
MLA_V = 64
MLA_Q_RANK = 384
MLA_KV_RANK = 256
MLA_VW = MLA_HEADS * MLA_V

GDN_HEADS = 8
GDN_DK = 64
GDN_DV = 64
GDN_CONV = 5
GDN_QK = GDN_HEADS * GDN_DK
GDN_VW = GDN_HEADS * GDN_DV
GDN_HALF = 256
GDN_SCAN_SAMPLES = 4

GLA_HEADS = 4
GLA_DK = 64
GLA_DV = 128
GLA_RANK = 16
GLA_TAU = 16.0
GLA_QK = GLA_HEADS * GLA_DK
GLA_VW = GLA_HEADS * GLA_DV
GLA_SCAN_SAMPLES = 4

SWA_HEADS = 8
SWA_KV_HEADS = 2
SWA_DH = 64
SWA_WINDOW = 128
SWA_QW = SWA_HEADS * SWA_DH
SWA_KW = SWA_KV_HEADS * SWA_DH

LANES = 128
HEAD_PAD = 128
TM = 256
FFN_TILES = (576, 512, 256, 128)
VMEM_LIMIT = 56 * 1024 * 1024

EVEN_COLS = MLA_Q_RANK + MLA_KV_RANK + LANES + 2 * GDN_QK + GDN_VW + GDN_VW + LANES
ODD_COLS = 2 * GLA_QK + GLA_VW + GLA_VW + SWA_QW + 2 * SWA_KW + LANES


def _dot(a, b):
    return jnp.dot(a.astype(BF16), b.astype(BF16), preferred_element_type=F32)


def _dot_nt(a, b):
    return lax.dot_general(a.astype(BF16), b.astype(BF16), (((1,), (1,)), ((), ())), preferred_element_type=F32)


def _dot_tn(a, b):
    return lax.dot_general(a.astype(BF16), b.astype(BF16), (((0,), (0,)), ((), ())), preferred_element_type=F32)


def _split2(x):
    hi = x.astype(BF16)
    lo = (x - hi.astype(F32)).astype(BF16)
    return hi, lo


def _split3(x):
    hi = x.astype(BF16)
    r = x - hi.astype(F32)
    mid = r.astype(BF16)
    lo = (r - mid.astype(F32)).astype(BF16)
    return hi, mid, lo


def _dot3(a, b):
    ah, al = _split2(a)
    bh, bl = _split2(b)
    d = functools.partial(jnp.dot, preferred_element_type=F32)
    return d(ah, bh) + (d(ah, bl) + d(al, bh))


def _dot_sel(sel, x):
    s = sel.astype(BF16)
    hi, mid, lo = _split3(x)
    d = functools.partial(jnp.dot, preferred_element_type=F32)
    return d(s, hi) + (d(s, mid) + d(s, lo))


def _dot_sel_r(x, sel):
    s = sel.astype(BF16)
    hi, mid, lo = _split3(x)
    d = functools.partial(jnp.dot, preferred_element_type=F32)
    return d(hi, s) + (d(mid, s) + d(lo, s))


def _seg_mean_sq(x, seg, n_real):
    w = x.shape[-1]
    r = lax.broadcasted_iota(jnp.int32, (w, w), 0) // seg
    c = lax.broadcasted_iota(jnp.int32, (w, w), 1) // seg
    ones_bd = (r == c).astype(BF16)
    hi, lo = _split2(x * x)
    d = functools.partial(jnp.dot, preferred_element_type=F32)
    return (d(hi, ones_bd) + d(lo, ones_bd)) * (1.0 / n_real)


def _rms_rows(x, g):
    ms = jnp.mean(x * x, axis=-1, keepdims=True)
    return x * lax.rsqrt(ms + EPS) * g


def _sigmoid(x):
    return 1.0 / (1.0 + jnp.exp(-x))


def _silu(x):
    return x * _sigmoid(x)


def _softplus(x):
    return jnp.maximum(x, 0.0) + jnp.log1p(jnp.exp(-jnp.abs(x)))


def _rope(x, c, s1, s2, half):
    return x * c + pltpu.roll(x, LANES - half, 1) * s1 + pltpu.roll(x, half, 1) * s2


def _chunk_tri(n, reverse):
    i = lax.broadcasted_iota(jnp.int32, (n, n), 0)
    j = lax.broadcasted_iota(jnp.int32, (n, n), 1)
    same = (i // CHUNK) == (j // CHUNK)
    tri = (j >= i) if reverse else (j <= i)
    return (same & tri).astype(F32)


def _ada_kernel(c_ref, w_ref, b_ref, o_ref):
    sc = _silu(c_ref[...])
    o_ref[...] = _dot(sc, w_ref[...]) + b_ref[...]


def _ada(cvec, ada_w, ada_b):
    depth, d, nm = ada_w.shape
    tn = 1024
    return pl.pallas_call(
        _ada_kernel,
        grid=(depth, nm // tn),
        in_specs=[
            pl.BlockSpec(cvec.shape, lambda i, j: (0, 0)),
            pl.BlockSpec((None, d, tn), lambda i, j: (i, 0, j)),
            pl.BlockSpec((None, 1, tn), lambda i, j: (i, 0, j)),
        ],
        out_specs=pl.BlockSpec((None, cvec.shape[0], tn), lambda i, j: (i, 0, j)),
        out_shape=jax.ShapeDtypeStruct((depth, cvec.shape[0], nm), F32),
        compiler_params=pltpu.CompilerParams(vmem_limit_bytes=VMEM_LIMIT),
        name="ada_mod",
    )(cvec, ada_w, ada_b.reshape(depth, 1, nm))


def _const_spec(shape):
    nd = len(shape)
    return pl.BlockSpec(shape, lambda *_: (0,) * nd, pipeline_mode=pl.Buffered(1))


def _tok_spec(width, skip=0):
    return pl.BlockSpec((None, TM, width), lambda b, t: (b, t + skip, 0))


def _mod_spec(ctx_tiles, skip=0):
    return pl.BlockSpec((None, None, N_MOD, D_MODEL),
                        lambda b, t: (b, jnp.where(t + skip >= ctx_tiles, 1, 0), 0, 0))


def _params():
    return pltpu.CompilerParams(dimension_semantics=("parallel", "arbitrary"), vmem_limit_bytes=VMEM_LIMIT)


def _ffn_kernel(x_ref, mod_ref, g_ref, wg_ref, wu_ref, wd_ref, o_ref, *, r0, n_ctx):
    x = x_ref[...]
    m = mod_ref[...]
    tm = x.shape[0]
    if n_ctx:
        is_ctx = pl.program_id(1) * tm + lax.broadcasted_iota(jnp.int32, (tm, 1), 0) < n_ctx
        row = lambda r: jnp.where(is_ctx, m[0, r:r + 1], m[1, r:r + 1])
    else:
        row = lambda r: m[1, r:r + 1]
    h = _rms_rows(x, g_ref[...]) * (1.0 + row(r0 + 1)) + row(r0)
    hb = h.astype(BF16)
    a = jnp.dot(hb, wg_ref[...], preferred_element_type=F32)
    u = jnp.dot(hb, wu_ref[...], preferred_element_type=F32)
    act = (_silu(a) * u).astype(BF16)
    y = jnp.dot(act, wd_ref[...], preferred_element_type=F32)
    o_ref[...] = x + (FFN_RES * row(r0 + 2)) * y


def _ffn(xs, mod, g, wg, wu, wd, *, r0, n_ctx):
    b, rows, d = xs.shape
    tm = next(t for t in FFN_TILES if rows % t == 0)
    tok = pl.BlockSpec((None, tm, d), lambda b_, t: (b_, t, 0))
    return pl.pallas_call(
        functools.partial(_ffn_kernel, r0=r0, n_ctx=n_ctx),
        grid=(b, rows // tm),
        in_specs=[
            tok, pl.BlockSpec((None, 2, N_MOD, d), lambda b_, t: (b_, 0, 0, 0)), _const_spec((1, d)),
            _const_spec(wg.shape), _const_spec(wu.shape), _const_spec(wd.shape),
        ],
        out_specs=tok,
        out_shape=jax.ShapeDtypeStruct((b, rows, d), F32),
        compiler_params=_params(),
        name="ffn_half_step",
    )(xs, mod, g.reshape(1, d), wg, wu, wd)


def _inproj_even_kernel(x_ref, mod_ref, g_ref, win_ref, qan_ref, wq_ref, kvan_ref, wkk_ref, wkv_ref,
                        qn_ref, kn_ref, rc_ref, rs1_ref, rs2_ref,
                        q_ref, k_ref, v_ref, zg_ref, gate_ref, small_ref):
    x = x_ref[...]
    m = mod_ref[...]
    h = _rms_rows(x, g_ref[...]) * (1.0 + m[4:5]) + m[3:4]
    z = jnp.dot(h.astype(BF16), win_ref[...], preferred_element_type=F32)
    o = 0
    cq = z[:, o:o + MLA_Q_RANK]; o += MLA_Q_RANK
    ckv = z[:, o:o + MLA_KV_RANK]; o += MLA_KV_RANK
    kr = z[:, o:o + LANES]; o += LANES
    zg_ref[...] = z[:, o:o + 3 * GDN_QK]; o += 3 * GDN_QK
    gate_ref[...] = z[:, o:o + GDN_VW]; o += GDN_VW
    small_ref[...] = z[:, o:o + LANES]

    rc, rs1, rs2 = rc_ref[...], rs1_ref[...], rs2_ref[...]
    qn, kn = qn_ref[...], kn_ref[...]
    qf = jnp.dot(_rms_rows(cq, qan_ref[...]).astype(BF16), wq_ref[...], preferred_element_type=F32)
    ckvn = _rms_rows(ckv, kvan_ref[...]).astype(BF16)
    kf = jnp.dot(ckvn, wkk_ref[...], preferred_element_type=F32)
    v_ref[...] = jnp.dot(ckvn, wkv_ref[...], preferred_element_type=F32).astype(v_ref.dtype)
    for hd in range(MLA_HEADS):
        sl = slice(hd * HEAD_PAD, (hd + 1) * HEAD_PAD)
        qh = qf[:, sl]
        qh = qh * lax.rsqrt(jnp.sum(qh * qh, axis=-1, keepdims=True) * (1.0 / MLA_QK) + EPS) * qn
        q_ref[:, sl] = _rope(qh, rc, rs1, rs2, MLA_ROPE // 2).astype(q_ref.dtype)
        kh = kf[:, sl] + kr
        kh = kh * lax.rsqrt(jnp.sum(kh * kh, axis=-1, keepdims=True) * (1.0 / MLA_QK) + EPS) * kn
        k_ref[:, sl] = _rope(kh, rc, rs1, rs2, MLA_ROPE // 2).astype(k_ref.dtype)


def _inproj_even(xs, mod, g, p, rope):
    b, rows, d = xs.shape
    ctx_tiles = p["ctx_tiles"]
    consts = [g.reshape(1, d), p["w_in"], p["q_a_norm"], p["w_q"], p["kv_a_norm"], p["w_kk"], p["w_kv"],
              p["q_norm"], p["k_norm"]]
    rope_spec = pl.BlockSpec((TM, LANES), lambda b_, t: (t, 0))
    widths = [(MLA_HEADS * HEAD_PAD, BF16), (MLA_HEADS * HEAD_PAD, BF16), (MLA_VW, BF16),
              (3 * GDN_QK, F32), (GDN_VW, F32), (LANES, F32)]
    return pl.pallas_call(
        _inproj_even_kernel,
        grid=(b, rows // TM),
        in_specs=[_tok_spec(d), _mod_spec(ctx_tiles)] + [_const_spec(c.shape) for c in consts] + [rope_spec] * 3,
        out_specs=[_tok_spec(w) for w, _ in widths],
        out_shape=[jax.ShapeDtypeStruct((b, rows, w), dt) for w, dt in widths],
        compiler_params=_params(),
        name="inproj_even",
    )(xs, mod, *consts, *rope)


def _mla_kernel(q_ref, k_ref, v_ref, o_ref, *, n_ctx, n_all, ctx_tiles):
    t = pl.program_id(2)
    scale = MLA_QK ** -0.5
    lo = lax.broadcasted_iota(jnp.int32, (TM, LANES), 1) < MLA_V

    def attend(nk):
        q = q_ref[...]
        v = v_ref[0:nk, :]
        outs = []
        for hh in range(2):
            sl = slice(hh * HEAD_PAD, (hh + 1) * HEAD_PAD)
            s = lax.dot_general(q[:, sl], k_ref[0:nk, sl], (((1,), (1,)), ((), ())),
                                preferred_element_type=F32) * scale
            p = jnp.exp(s - jnp.max(s, axis=-1, keepdims=True))
            den = jnp.sum(p, axis=-1, keepdims=True)
            outs.append(jnp.dot(p.astype(BF16), v, preferred_element_type=F32) / den)
        o_ref[...] = jnp.where(lo, outs[0], outs[1]).astype(o_ref.dtype)

    @pl.when(t < ctx_tiles)
    def _():
        attend(n_ctx)

    @pl.when(t >= ctx_tiles)
    def _():
        attend(n_all)


def _mla_attention(q, k, v, *, n_ctx):
    b, rows, _ = q.shape
    pairs = MLA_HEADS // 2
    return pl.pallas_call(
        functools.partial(_mla_kernel, n_ctx=n_ctx, n_all=rows, ctx_tiles=n_ctx // TM),
        grid=(b, pairs, rows // TM),
        in_specs=[
            pl.BlockSpec((None, TM, 2 * HEAD_PAD), lambda b_, h, t: (b_, t, h)),
            pl.BlockSpec((None, rows, 2 * HEAD_PAD), lambda b_, h, t: (b_, 0, h)),
            pl.BlockSpec((None, rows, 2 * MLA_V), lambda b_, h, t: (b_, 0, h)),
        ],
        out_specs=pl.BlockSpec((None, TM, 2 * MLA_V), lambda b_, h, t: (b_, t, h)),
        out_shape=jax.ShapeDtypeStruct((b, rows, MLA_VW), BF16),
        compiler_params=pltpu.CompilerParams(dimension_semantics=("parallel", "parallel", "arbitrary"),
                                             vmem_limit_bytes=VMEM_LIMIT),
        name="mla_attention",
    )(q, k, v)


def _gdn_prep_kernel(z_ref, zp_ref, zn_ref, cw_ref, sm_ref, nega_ref, dtb_ref,
                     q_ref, k_ref, v_ref, gf_ref, gb_ref, *, ctx_tiles, n_tiles):
    t = pl.program_id(1)
    first = (t == 0) | (t == ctx_tiles)
    last = (t == ctx_tiles - 1) | (t == n_tiles - 1)
    z = z_ref[...]
    half = GDN_CONV // 2
    prev = jnp.where(first, 0.0, zp_ref[...][8 - half:, :])
    nxt = jnp.where(last, 0.0, zn_ref[...][:half, :])
    ext = jnp.concatenate([prev, z, nxt], axis=0)
    cw = cw_ref[...]
    acc = ext[0:TM] * cw[0:1]
    for j in range(1, GDN_CONV):
        acc = acc + ext[j:j + TM] * cw[j:j + 1]
    qkv = _silu(acc)
    q = qkv[:, :GDN_QK]
    k = qkv[:, GDN_QK:2 * GDN_QK]
    q_ref[...] = q * lax.rsqrt(_seg_mean_sq(q, GDN_DK, 1.0) + EPS) * (GDN_DK ** -0.5)
    k_ref[...] = k * lax.rsqrt(_seg_mean_sq(k, GDN_DK, 1.0) + EPS)
    v_ref[...] = qkv[:, 2 * GDN_QK:]

    sm = sm_ref[...]
    lane = lax.broadcasted_iota(jnp.int32, sm.shape, 1)
    g = nega_ref[...] * _softplus(sm + dtb_ref[...])
    g = jnp.where(lane < 2 * GDN_HEADS, g, 0.0)
    gc_f = _dot_sel(_chunk_tri(TM, False), g)
    gc_b = _dot_sel(_chunk_tri(TM, True), g)
    gates = jnp.where(lane < GDN_HEADS, gc_f, jnp.where(lane < 2 * GDN_HEADS, gc_b, _sigmoid(sm)))
    row = lax.broadcasted_iota(jnp.int32, (LANES, GDN_QK), 0)
    head = lax.broadcasted_iota(jnp.int32, (LANES, GDN_QK), 1) // GDN_DK
    for d, ref in enumerate((gf_ref, gb_ref)):
        ref[:, 0:GDN_QK] = _dot_sel_r(gates, (row == head + d * GDN_HEADS).astype(F32))
        ref[:, GDN_QK:2 * GDN_QK] = _dot_sel_r(gates, (row == head + (2 + d) * GDN_HEADS).astype(F32))


def _gdn_prep(zg, small, conv_w, nega, dtb, *, ctx_tiles):
    b, rows, w = zg.shape
    n_tiles = rows // TM
    hb = TM // 8
    n_hblk = rows // 8
    return pl.pallas_call(
        functools.partial(_gdn_prep_kernel, ctx_tiles=ctx_tiles, n_tiles=n_tiles),
        grid=(b, n_tiles),
        in_specs=[
            _tok_spec(w),
            pl.BlockSpec((None, 8, w), lambda b_, t: (b_, jnp.maximum(t * hb - 1, 0), 0)),
            pl.BlockSpec((None, 8, w), lambda b_, t: (b_, jnp.minimum((t + 1) * hb, n_hblk - 1), 0)),
            _const_spec(conv_w.shape), _tok_spec(LANES), _const_spec((1, LANES)), _const_spec((1, LANES)),
        ],
        out_specs=[_tok_spec(GDN_QK), _tok_spec(GDN_QK), _tok_spec(GDN_VW), _tok_spec(2 * GDN_QK),
                   _tok_spec(2 * GDN_QK)],
        out_shape=[jax.ShapeDtypeStruct((b, rows, GDN_QK), F32), jax.ShapeDtypeStruct((b, rows, GDN_QK), F32),
                   jax.ShapeDtypeStruct((b, rows, GDN_VW), F32), jax.ShapeDtypeStruct((b, rows, 2 * GDN_QK), F32),
                   jax.ShapeDtypeStruct((b, rows, 2 * GDN_QK), F32)],
        compiler_params=_params(),
        name="gdn_prep",
    )(zg, zg, zg, conv_w, small, nega, dtb)


def _bd_halves(y, bdm):
    yb = y.astype(BF16)
    out = []
    for s in range(2):
        t = jnp.concatenate([yb[:, s * GDN_HALF:(s + 1) * GDN_HALF]] * (GDN_HALF // GDN_DK), axis=0)
        out.append(jnp.where(bdm, t, jnp.zeros_like(t)))
    return out


def _hprod(x, bd):
    xb = x.astype(BF16)
    d = functools.partial(jnp.dot, preferred_element_type=F32)
    return jnp.concatenate([d(xb[:, 0:GDN_HALF], bd[0]), d(xb[:, GDN_HALF:], bd[1])], axis=1)


def _hprod3(x, y, bdm):
    xh, xl = _split2(x)
    yh, yl = _split2(y)
    bh, bl = _bd_halves(yh, bdm), _bd_halves(yl, bdm)
    return _hprod(xh, bh) + (_hprod(xh, bl) + _hprod(xl, bh))


def _gdn_chunks(chains):
    ii = lax.broadcasted_iota(jnp.int32, (CHUNK, GDN_QK), 0)
    jj = lax.broadcasted_iota(jnp.int32, (CHUNK, GDN_QK), 1) % CHUNK
    r = lax.broadcasted_iota(jnp.int32, (GDN_HALF, GDN_HALF), 0) // GDN_DK
    c = lax.broadcasted_iota(jnp.int32, (GDN_HALF, GDN_HALF), 1) // GDN_DK
    bdm = r == c
    diag = ii == jj
    eye = diag.astype(F32)
    n = len(chains)
    qs, ks, vs, gs, srefs, revs = (list(t) for t in zip(*chains))
    each = lambda f, *ls: [f(*a) for a in zip(*ls)]
    bd = lambda ys: [_bd_halves(y, bdm) for y in ys]

    incl = [(ii <= jj) if rv else (ii >= jj) for rv in revs]
    strict = [(ii < jj) if rv else (ii > jj) for rv in revs]
    last = [0 if rv else CHUNK - 1 for rv in revs]
    gc = [g[:, 0:GDN_QK] for g in gs]
    beta = [g[:, GDN_QK:2 * GDN_QK] for g in gs]
    gc_row = [jnp.sum(jnp.where(diag, x, 0.0), axis=0, keepdims=True) for x in gc]
    decay = each(lambda x, xr, ic: jnp.exp(jnp.where(ic, x - xr, -jnp.inf)), gc, gc_row, incl)

    def kt_bd(k):
        kt = k.T.astype(BF16)
        out = []
        for s in range(2):
            t = jnp.concatenate([kt[s * GDN_HALF:(s + 1) * GDN_HALF]] * (GDN_HALF // CHUNK), axis=1)
            out.append(jnp.where(bdm, t, jnp.zeros_like(t)))
        return out

    kkqk = each(lambda k, q, rk: _hprod(jnp.concatenate([k, q], axis=0), rk), ks, qs, [kt_bd(k) for k in ks])
    a = each(lambda st, b, x, dc: jnp.where(st, b * x[0:CHUNK] * dc, 0.0), strict, beta, kkqk, decay)
    qk = each(lambda x, dc: x[CHUNK:2 * CHUNK] * dc, kkqk, decay)

    def level_mask(blk, rv):
        same = (ii // (2 * blk)) == (jj // (2 * blk))
        off = (ii // blk) < (jj // blk) if rv else (ii // blk) > (jj // blk)
        return same & off

    m = each(lambda x, rv: eye - jnp.where(level_mask(1, rv), x, 0.0), a, revs)
    blk = 2
    while blk < CHUNK:
        am = each(lambda x, rv: jnp.where(level_mask(blk, rv), x, 0.0), a, revs)
        x = each(_hprod, am, bd(m))
        m = each(lambda mm, y, ybd: mm - _hprod(mm, ybd), m, x, bd(x))
        blk *= 2
    am = each(lambda x, mm: _hprod3(x, mm, bdm), a, m)
    resid = each(lambda mm, y: eye - mm - y, m, am)
    m = each(lambda mm, rbd: mm + _hprod(mm, rbd), m, bd(resid))

    egc = [jnp.exp(x) for x in gc]
    u = each(_hprod, m, bd(each(lambda v, b: v * b, vs, beta)))
    w = each(_hprod, m, bd(each(lambda k, b, e: k * (b * e), ks, beta, egc)))
    g_last = each(lambda x, l: x[l:l + 1, :], gc, last)
    s0 = [sr[0] for sr in srefs]
    s1 = [sr[1] for sr in srefs]

    def sprod(x, a0, a1):
        return jnp.concatenate([_dot(x[:, 0:GDN_HALF], a0), _dot(x[:, GDN_HALF:], a1)], axis=1)

    ws = each(lambda ww, q, e, a0, a1: sprod(jnp.concatenate([ww, q * e], axis=0), a0, a1), w, qs, egc, s0, s1)
    v_new = each(lambda uu, x: uu - x[0:CHUNK], u, ws)
    o = each(lambda x, y, vbd: x[CHUNK:2 * CHUNK] + _hprod(y, vbd), ws, qk, bd(v_new))
    k_dec = each(lambda k, gl, x: k * jnp.exp(gl - x), ks, g_last, gc)
    e_last = [jnp.exp(gl) for gl in g_last]
    for i in range(n):
        for s, st in enumerate((s0[i], s1[i])):
            sl = slice(s * GDN_HALF, (s + 1) * GDN_HALF)
            srefs[i][s] = st * e_last[i][:, sl] + jnp.where(bdm, _dot_tn(k_dec[i][:, sl], v_new[i][:, sl]), 0.0)
    return o


def _gdn_scan_kernel(qf_ref, kf_ref, vf_ref, gf_ref, qb_ref, kb_ref, vb_ref, gb_ref, of_ref, ob_ref, sf_ref, sb_ref):
    @pl.when(pl.program_id(1) == 0)
    def _():
        sf_ref[...] = jnp.zeros_like(sf_ref)
        sb_ref[...] = jnp.zeros_like(sb_ref)

    nb = qf_ref.shape[0]
    chains = []
    for i in range(nb):
        chains.append((qf_ref[i], kf_ref[i], vf_ref[i], gf_ref[i], sf_ref.at[i], False))
        chains.append((qb_ref[i], kb_ref[i], vb_ref[i], gb_ref[i], sb_ref.at[i], True))
    outs = _gdn_chunks(chains)
    for i in range(nb):
        of_ref[i] = outs[2 * i]
        ob_ref[i] = outs[2 * i + 1]


def _scan_chunk_maps(ctx_chunks, n_chunks):
    fwd = lambda s: s
    bwd = lambda s: jnp.where(s < ctx_chunks, ctx_chunks - 1 - s, n_chunks - 1 - (s - ctx_chunks))
    return fwd, bwd


def _gdn_scan(q, k, v, g_f, g_b, *, n_ctx):
    b, rows, _ = q.shape
    n_chunks = rows // CHUNK
    fwd, bwd = _scan_chunk_maps(n_ctx // CHUNK, n_chunks)

    nb = GDN_SCAN_SAMPLES if b % GDN_SCAN_SAMPLES == 0 else 1

    def specs(cm):
        tok = lambda w: pl.BlockSpec((nb, CHUNK, w), lambda b_, s: (b_, cm(s), 0))
        return [tok(GDN_QK), tok(GDN_QK), tok(GDN_VW), tok(2 * GDN_QK)]

    return pl.pallas_call(
        _gdn_scan_kernel,
        grid=(b // nb, n_chunks),
        in_specs=specs(fwd) + specs(bwd),
        out_specs=[pl.BlockSpec((nb, CHUNK, GDN_VW), lambda b_, s: (b_, fwd(s), 0)),
                   pl.BlockSpec((nb, CHUNK, GDN_VW), lambda b_, s: (b_, bwd(s), 0))],
        out_shape=[jax.ShapeDtypeStruct((b, rows, GDN_VW), F32)] * 2,
        scratch_shapes=[pltpu.VMEM((nb, 2, GDN_HALF, GDN_HALF), F32)] * 2,
        compiler_params=_params(),
        name="gdn_scan",
    )(q, k, v, g_f, q, k, v, g_b)


def _outproj_kernel(x_ref, mod_ref, oa_ref, of_ref, ob_ref, gate_ref, gn_ref, w_ref, o_ref, *, seg, rec_first):
    o = of_ref[...] + ob_ref[...]
    y = o * lax.rsqrt(_seg_mean_sq(o, seg, float(seg)) + EPS) * gn_ref[...] * _silu(gate_ref[...])
    wr = y.shape[-1]
    wa = oa_ref.shape[-1]
    if rec_first:
        out = jnp.dot(y.astype(BF16), w_ref[0:wr, :], preferred_element_type=F32)
        out = out + jnp.dot(oa_ref[...], w_ref[wr:wr + wa, :], preferred_element_type=F32)
    else:
        out = jnp.dot(oa_ref[...], w_ref[0:wa, :], preferred_element_type=F32)
        out = out + jnp.dot(y.astype(BF16), w_ref[wa:wa + wr, :], preferred_element_type=F32)
    o_ref[...] = x_ref[...] + mod_ref[...][5:6] * out


def _outproj(xs, mod, o_att, o_f, o_b, gate, gn, w_out, *, seg, rec_first, ctx_tiles, skip=0, att_skip=0):
    b, rows, d = xs.shape
    out_rows = rows - skip * TM
    return pl.pallas_call(
        functools.partial(_outproj_kernel, seg=seg, rec_first=rec_first),
        grid=(b, out_rows // TM),
        in_specs=[_tok_spec(d, skip), _mod_spec(ctx_tiles, skip), _tok_spec(o_att.shape[-1], att_skip),
                  _tok_spec(o_f.shape[-1], skip), _tok_spec(o_b.shape[-1], skip), _tok_spec(gate.shape[-1], skip),
                  _const_spec(gn.shape), _const_spec(w_out.shape)],
        out_specs=_tok_spec(d),
        out_shape=jax.ShapeDtypeStruct((b, out_rows, d), F32),
        compiler_params=_params(),
        name="outproj",
    )(xs, mod, o_att, o_f, o_b, gate, gn, w_out)


def _inproj_odd_kernel(x_ref, mod_ref, g_ref, win_ref, w2_ref, b2_ref, qn_ref, kn_ref, rc_ref, rs1_ref, rs2_ref,
                       gq_ref, gk_ref, gv_ref, rg_ref, bc_ref, sq_ref, sk_ref, sv_ref):
    x = x_ref[...]
    m = mod_ref[...]
    h = _rms_rows(x, g_ref[...]) * (1.0 + m[4:5]) + m[3:4]
    z = jnp.dot(h.astype(BF16), win_ref[...], preferred_element_type=F32)
    o = 0
    gq_ref[...] = z[:, o:o + GLA_QK]; o += GLA_QK
    gk_ref[...] = z[:, o:o + GLA_QK]; o += GLA_QK
    gv_ref[...] = z[:, o:o + GLA_VW]; o += GLA_VW
    rg_ref[...] = z[:, o:o + GLA_VW]; o += GLA_VW
    sq = z[:, o:o + SWA_QW]; o += SWA_QW
    sk = z[:, o:o + SWA_KW]; o += SWA_KW
    sv_ref[...] = z[:, o:o + SWA_KW].astype(sv_ref.dtype); o += SWA_KW
    lowrank = z[:, o:o + LANES]

    logit = _dot(lowrank, w2_ref[...]) + b2_ref[...]
    log_a = (jnp.minimum(logit, 0.0) - jnp.log1p(jnp.exp(-jnp.abs(logit)))) * (1.0 / GLA_TAU)
    bc_ref[:, 0:GLA_QK] = _dot_sel(_chunk_tri(TM, False), log_a[:, 0:GLA_QK])
    bc_ref[:, GLA_QK:2 * GLA_QK] = _dot_sel(_chunk_tri(TM, True), log_a[:, GLA_QK:2 * GLA_QK])

    rc, rs1, rs2 = rc_ref[...], rs1_ref[...], rs2_ref[...]
    sqn = sq * lax.rsqrt(_seg_mean_sq(sq, SWA_DH, float(SWA_DH)) + EPS) * qn_ref[...]
    for s in range(SWA_QW // LANES):
        sl = slice(s * LANES, (s + 1) * LANES)
        sq_ref[:, sl] = _rope(sqn[:, sl], rc, rs1, rs2, SWA_DH // 2).astype(sq_ref.dtype)
    skn = sk * lax.rsqrt(_seg_mean_sq(sk, SWA_DH, float(SWA_DH)) + EPS) * kn_ref[...]
    sk_ref[...] = _rope(skn, rc, rs1, rs2, SWA_DH // 2).astype(sk_ref.dtype)


def _inproj_odd(xs, mod, g, p, rope):
    b, rows, d = xs.shape
    consts = [g.reshape(1, d), p["w_in"], p["w2"], p["b2"], p["q_norm"], p["k_norm"]]
    rope_spec = pl.BlockSpec((TM, LANES), lambda b_, t: (t, 0))
    widths = [(GLA_QK, F32), (GLA_QK, F32), (GLA_VW, F32), (GLA_VW, F32), (2 * GLA_QK, F32),
              (SWA_QW, BF16), (SWA_KW, BF16), (SWA_KW, BF16)]
    return pl.pallas_call(
        _inproj_odd_kernel,
        grid=(b, rows // TM),
        in_specs=[_tok_spec(d), _mod_spec(p["ctx_tiles"])] + [_const_spec(c.shape) for c in consts] + [rope_spec] * 3,
        out_specs=[_tok_spec(w) for w, _ in widths],
        out_shape=[jax.ShapeDtypeStruct((b, rows, w), dt) for w, dt in widths],
        compiler_params=_params(),
        name="inproj_odd",
    )(xs, mod, *consts, *rope)


def _gla_chunk(q, k, v, bc, st_ref, reverse):
    ii = lax.broadcasted_iota(jnp.int32, (CHUNK, CHUNK), 0)
    jj = lax.broadcasted_iota(jnp.int32, (CHUNK, CHUNK), 1)
    incl = (ii <= jj) if reverse else (ii >= jj)
    last = 0 if reverse else CHUNK - 1
    q_dec = (q * (GLA_DK ** -0.5)) * jnp.exp(bc)
    k_inv = k * jnp.exp(-bc)
    b_last = bc[last:last + 1, :]
    k_dec = k * jnp.exp(b_last - bc)
    e_last = jnp.exp(b_last)
    head = lax.broadcasted_iota(jnp.int32, (CHUNK, GLA_QK), 1) // GLA_DK
    outs = []
    for h in range(GLA_HEADS):
        hm = head == h
        attn = jnp.where(incl, _dot_nt(q_dec, jnp.where(hm, k_inv, 0.0)), 0.0)
        vh = v[:, h * GLA_DV:(h + 1) * GLA_DV]
        st = st_ref[h]
        outs.append(_dot(attn, vh) + _dot_nt(q_dec, st))
        st_ref[h] = st * e_last + _dot_tn(vh, jnp.where(hm, k_dec, 0.0))
    return jnp.concatenate(outs, axis=-1)


def _gla_scan_kernel(qf_ref, kf_ref, vf_ref, bf_ref, qb_ref, kb_ref, vb_ref, bb_ref, of_ref, ob_ref, sf_ref, sb_ref):
    @pl.when(pl.program_id(1) == 0)
    def _():
        sf_ref[...] = jnp.zeros_like(sf_ref)
        sb_ref[...] = jnp.zeros_like(sb_ref)

    for i in range(qf_ref.shape[0]):
        of_ref[i] = _gla_chunk(qf_ref[i], kf_ref[i], vf_ref[i], bf_ref[i][:, 0:GLA_QK], sf_ref.at[i], False)
        ob_ref[i] = _gla_chunk(qb_ref[i], kb_ref[i], vb_ref[i], bb_ref[i][:, GLA_QK:2 * GLA_QK], sb_ref.at[i], True)


def _gla_scan(q, k, v, bc, *, n_ctx):
    b, rows, _ = q.shape
    n_chunks = rows // CHUNK
    fwd, bwd = _scan_chunk_maps(n_ctx // CHUNK, n_chunks)
    nb = GLA_SCAN_SAMPLES if b % GLA_SCAN_SAMPLES == 0 else 1

    def specs(cm):
        tok = lambda w: pl.BlockSpec((nb, CHUNK, w), lambda b_, s: (b_, cm(s), 0))
        return [tok(GLA_QK), tok(GLA_QK), tok(GLA_VW), tok(2 * GLA_QK)]

    return pl.pallas_call(
        _gla_scan_kernel,
        grid=(b // nb, n_chunks),
        in_specs=specs(fwd) + specs(bwd),
        out_specs=[pl.BlockSpec((nb, CHUNK, GLA_VW), lambda b_, s: (b_, fwd(s), 0)),
                   pl.BlockSpec((nb, CHUNK, GLA_VW), lambda b_, s: (b_, bwd(s), 0))],
        out_shape=[jax.ShapeDtypeStruct((b, rows, GLA_VW), F32)] * 2,
        scratch_shapes=[pltpu.VMEM((nb, GLA_HEADS, GLA_DV, GLA_QK), F32)] * 2,
        compiler_params=_params(),
        name="gla_scan",
    )(q, k, v, bc, q, k, v, bc)


def _swa_kernel(q_ref, k_ref, v_ref, sink_ref, o_ref, *, n_ctx, n_lat):
    w = SWA_WINDOW
    n = pl.program_id(1)
    nb = n_lat // w
    scale = SWA_DH ** -0.5

    def rows(ref, blk):
        return ref[pl.ds(pl.multiple_of(n_ctx + blk * w, w), w), :]

    pb = jnp.maximum(n - 1, 0)
    xb = jnp.minimum(n + 1, nb - 1)
    k_cat = jnp.concatenate([rows(k_ref, pb), rows(k_ref, n), rows(k_ref, xb), k_ref[0:n_ctx, :]], axis=0)
    v_cat = jnp.concatenate([rows(v_ref, pb), rows(v_ref, n), rows(v_ref, xb), v_ref[0:n_ctx, :]], axis=0)
    ii = lax.broadcasted_iota(jnp.int32, (w, w), 0)
    jj = lax.broadcasted_iota(jnp.int32, (w, w), 1)
    ninf = -jnp.inf
    bias = jnp.concatenate([
        jnp.where((jj >= ii) & (n > 0), 0.0, ninf),
        jnp.zeros((w, w), F32),
        jnp.where((jj <= ii) & (n < nb - 1), 0.0, ninf),
        jnp.zeros((w, n_ctx), F32)], axis=1)
    half = SWA_HEADS // 2
    lo = lax.broadcasted_iota(jnp.int32, (half * w, LANES), 1) < SWA_DH
    q = q_ref[...]
    qst = jnp.concatenate([q[:, j * LANES:(j + 1) * LANES] for j in range(half)], axis=0)
    bias4 = jnp.concatenate([bias] * half, axis=0)
    sink = sink_ref[...]
    res = []
    for g in range(SWA_KV_HEADS):
        qm = jnp.where(lo if g == 0 else jnp.logical_not(lo), qst, jnp.zeros_like(qst))
        s = lax.dot_general(qm, k_cat, (((1,), (1,)), ((), ())), preferred_element_type=F32) * scale + bias4
        sk = jnp.concatenate([jnp.broadcast_to(sink[0:1, half * g + j:half * g + j + 1], (w, 1)) for j in range(half)],
                             axis=0)
        mx = jnp.maximum(jnp.max(s, axis=-1, keepdims=True), sk)
        p = jnp.exp(s - mx)
        den = jnp.sum(p, axis=-1, keepdims=True) + jnp.exp(sk - mx)
        res.append(jnp.dot(p.astype(BF16), v_cat, preferred_element_type=F32) / den)
    o = jnp.where(lo, res[0], res[1]).astype(o_ref.dtype)
    for j in range(half):
        o_ref[:, j * LANES:(j + 1) * LANES] = o[j * w:(j + 1) * w, :]


def _swa_attention(q, k, v, sink, *, n_ctx):
    b, rows, _ = q.shape
    n_lat = rows - n_ctx
    w = SWA_WINDOW
    skip = n_ctx // w
    return pl.pallas_call(
        functools.partial(_swa_kernel, n_ctx=n_ctx, n_lat=n_lat),
        grid=(b, n_lat // w),
        in_specs=[
            pl.BlockSpec((None, w, SWA_QW), lambda b_, n: (b_, n + skip, 0)),
            pl.BlockSpec((None, rows, SWA_KW), lambda b_, n: (b_, 0, 0)),
            pl.BlockSpec((None, rows, SWA_KW), lambda b_, n: (b_, 0, 0)),
            _const_spec((1, LANES)),
        ],
        out_specs=pl.BlockSpec((None, w, SWA_QW), lambda b_, n: (b_, n, 0)),
        out_shape=jax.ShapeDtypeStruct((b, n_lat, SWA_QW), BF16),
        compiler_params=_params(),
        name="swa_attention",
    )(q, k, v, sink)


def _rope_tables(n_lat, n_ctx, rot_dim):
    t = jnp.arange(n_lat)
    row = (t // GRID_W).astype(F32)
    col = (t % GRID_W).astype(F32)
    n_freq = rot_dim // 4
    inv = ROPE_THETA ** (-jnp.arange(n_freq, dtype=F32) / n_freq)
    ang = jnp.concatenate([row[:, None] * inv, col[:, None] * inv], axis=-1)
    half = rot_dim // 2
    cos = jnp.concatenate([jnp.ones((n_ctx, half), F32), jnp.cos(ang)], axis=0)
    sin = jnp.concatenate([jnp.zeros((n_ctx, half), F32), jnp.sin(ang)], axis=0)
    rows = n_ctx + n_lat
    one = lambda w: jnp.ones((rows, w), F32)
    zero = lambda w: jnp.zeros((rows, w), F32)
    if rot_dim == MLA_ROPE:
        c = jnp.concatenate([one(MLA_NOPE), cos, cos, one(HEAD_PAD - MLA_QK)], axis=1)
        s1 = jnp.concatenate([zero(MLA_NOPE), -sin, zero(half), zero(HEAD_PAD - MLA_QK)], axis=1)
        s2 = jnp.concatenate([zero(MLA_NOPE), zero(half), sin, zero(HEAD_PAD - MLA_QK)], axis=1)
    else:
        c = jnp.concatenate([cos, cos, cos, cos], axis=1)
        s1 = jnp.concatenate([-sin, zero(half), -sin, zero(half)], axis=1)
        s2 = jnp.concatenate([zero(half), sin, zero(half), sin], axis=1)
    return c, s1, s2


def _even_params(j, n_ctx, ev_w_in, ev_q_a_norm, ev_w_q_up, ev_kv_a_norm, ev_w_kv_up, ev_mla_q_norm, ev_mla_k_norm,
                 ev_gdn_conv, ev_gdn_a_log, ev_gdn_dt_bias, ev_gdn_out_norm, ev_w_out):
    w = ev_w_in[j]
    d = w.shape[0]
    z = lambda n: jnp.zeros((d, n), F32)
    o_kr = MLA_Q_RANK + MLA_KV_RANK
    o_g = o_kr + MLA_ROPE
    o_small = o_g + 3 * GDN_QK
    o_gate = o_small + 4 * GDN_HEADS
    w_in = jnp.concatenate([
        w[:, :o_kr], z(MLA_NOPE), w[:, o_kr:o_g], z(HEAD_PAD - MLA_QK),
        w[:, o_g:o_small], w[:, o_gate:o_gate + GDN_VW],
        w[:, o_small:o_gate], z(LANES - 4 * GDN_HEADS)], axis=1).astype(BF16)
    pad_h = HEAD_PAD - MLA_QK
    w_q = jnp.pad(ev_w_q_up[j].reshape(MLA_Q_RANK, MLA_HEADS, MLA_QK), ((0, 0), (0, 0), (0, pad_h)))
    wkv = ev_w_kv_up[j].reshape(MLA_KV_RANK, MLA_HEADS, MLA_NOPE + MLA_V)
    w_kk = jnp.pad(wkv[:, :, :MLA_NOPE], ((0, 0), (0, 0), (0, HEAD_PAD - MLA_NOPE)))
    lane_row = lambda vec: jnp.pad(vec, (0, LANES - vec.shape[0])).reshape(1, LANES)
    fb = lambda a: jnp.concatenate([a[0], a[1]])
    return {
        "ctx_tiles": n_ctx // TM,
        "w_in": w_in,
        "q_a_norm": ev_q_a_norm[j].reshape(1, -1),
        "w_q": w_q.reshape(MLA_Q_RANK, MLA_HEADS * HEAD_PAD).astype(BF16),
        "kv_a_norm": ev_kv_a_norm[j].reshape(1, -1),
        "w_kk": w_kk.reshape(MLA_KV_RANK, MLA_HEADS * HEAD_PAD).astype(BF16),
        "w_kv": wkv[:, :, MLA_NOPE:].reshape(MLA_KV_RANK, MLA_VW).astype(BF16),
        "q_norm": lane_row(ev_mla_q_norm[j]),
        "k_norm": lane_row(ev_mla_k_norm[j]),
        "conv_w": ev_gdn_conv[j],
        "neg_a": lane_row(-jnp.exp(fb(ev_gdn_a_log[j]))),
        "dt_bias": lane_row(fb(ev_gdn_dt_bias[j])),
        "out_norm": jnp.tile(ev_gdn_out_norm[j], GDN_HEADS).reshape(1, GDN_VW),
        "w_out": ev_w_out[j].astype(BF16),
    }


def _swa_head_perm():
    half = SWA_HEADS // 2
    heads = [h for j in range(half) for h in (j, half + j)]
    return jnp.concatenate([jnp.arange(SWA_DH) + h * SWA_DH for h in heads])


def _odd_params(j, n_ctx, od_w_in, od_gla_gate_w2, od_gla_gate_b, od_gla_out_norm, od_swa_q_norm, od_swa_k_norm,
                od_swa_sink, od_w_out):
    w = od_w_in[j]
    d = w.shape[0]
    o_gate = 2 * GLA_QK + GLA_VW
    o_rg = o_gate + 2 * GLA_RANK
    o_sq = o_rg + GLA_VW
    o_sk = o_sq + SWA_QW
    perm = _swa_head_perm()
    w_in = jnp.concatenate([
        w[:, :o_gate], w[:, o_rg:o_sq], w[:, o_sq:o_sk][:, perm], w[:, o_sk:],
        w[:, o_gate:o_rg], jnp.zeros((d, LANES - 2 * GLA_RANK), F32)], axis=1).astype(BF16)
    w2 = jnp.zeros((LANES, 2 * GLA_QK), F32)
    w2 = w2.at[0:GLA_RANK, 0:GLA_QK].set(od_gla_gate_w2[j, 0])
    w2 = w2.at[GLA_RANK:2 * GLA_RANK, GLA_QK:].set(od_gla_gate_w2[j, 1])
    wo = od_w_out[j]
    w_out = jnp.concatenate([wo[:GLA_VW], wo[GLA_VW:][perm]], axis=0).astype(BF16)
    return {
        "ctx_tiles": n_ctx // TM,
        "w_in": w_in,
        "w2": w2.astype(BF16),
        "b2": jnp.concatenate([od_gla_gate_b[j, 0], od_gla_gate_b[j, 1]]).reshape(1, 2 * GLA_QK),
        "q_norm": jnp.tile(od_swa_q_norm[j], SWA_HEADS).reshape(1, SWA_QW),
        "k_norm": jnp.tile(od_swa_k_norm[j], SWA_KV_HEADS).reshape(1, SWA_KW),
        "sink": jnp.pad(od_swa_sink[j], (0, LANES - SWA_HEADS)).reshape(1, LANES),
        "out_norm": jnp.tile(od_gla_out_norm[j], GLA_HEADS).reshape(1, GLA_VW),
        "w_out": w_out,
    }


def kernel(x, c, ctx, c_ctx, ada_w, ada_b, norm_g, ffn_w_gate, ffn_w_up, ffn_w_down, ev_w_in, ev_q_a_norm, ev_w_q_up, ev_kv_a_norm, ev_w_kv_up, ev_mla_q_norm, ev_mla_k_norm, ev_gdn_conv, ev_gdn_a_log, ev_gdn_dt_bias, ev_gdn_out_norm, ev_w_out, od_w_in, od_gla_gate_w2, od_gla_gate_b, od_gla_out_norm, od_swa_q_norm, od_swa_k_norm, od_swa_sink, od_w_out):
    b, n_lat, d = x.shape
    n_ctx = ctx.shape[1]
    depth = ada_w.shape[0]
    assert d == D_MODEL and n_ctx % TM == 0 and n_lat % TM == 0 and n_lat % GRID_W == 0
    assert depth % 2 == 0, "the last layer must be an odd (GLA/SWA) layer: context outputs of that mixer are not built"
    ctx_tiles = n_ctx // TM

    cvec = jnp.concatenate([c, c_ctx[None, :], jnp.zeros((16 - b - 1, d), F32)], axis=0)
    mod_all = _ada(cvec, ada_w, ada_b).reshape(depth, 16, N_MOD, d)
    rope_mla = _rope_tables(n_lat, n_ctx, MLA_ROPE)
    rope_swa = _rope_tables(n_lat, n_ctx, SWA_DH)

    xs = jnp.concatenate([ctx, x], axis=1)
    for i in range(depth):
        last = i == depth - 1
        mod = jnp.stack([jnp.broadcast_to(mod_all[i, b][None], (b, N_MOD, d)), mod_all[i, :b]], axis=1)
        wg, wu, wd = (t.astype(BF16) for t in (ffn_w_gate[i], ffn_w_up[i], ffn_w_down[i]))
        xs = _ffn(xs, mod, norm_g[i, 0], wg[0], wu[0], wd[0], r0=0, n_ctx=n_ctx)
        j = i // 2
        skip = ctx_tiles if last else 0
        if i % 2 == 0:
            p = _even_params(j, n_ctx, ev_w_in, ev_q_a_norm, ev_w_q_up, ev_kv_a_norm, ev_w_kv_up, ev_mla_q_norm,
                             ev_mla_k_norm, ev_gdn_conv, ev_gdn_a_log, ev_gdn_dt_bias, ev_gdn_out_norm, ev_w_out)
            q, k, v, zg, gate, small = _inproj_even(xs, mod, norm_g[i, 1], p, rope_mla)
            o_att = _mla_attention(q, k, v, n_ctx=n_ctx)
            gq, gk, gv, g_f, g_b = _gdn_prep(zg, small, p["conv_w"], p["neg_a"], p["dt_bias"], ctx_tiles=ctx_tiles)
            o_f, o_b = _gdn_scan(gq, gk, gv, g_f, g_b, n_ctx=n_ctx)
            xs = _outproj(xs, mod, o_att, o_f, o_b, gate, p["out_norm"], p["w_out"], seg=GDN_DV, rec_first=False,
                          ctx_tiles=ctx_tiles, skip=skip, att_skip=skip)
        else:
            assert last, "odd layers that must also produce context outputs are not built"
            p = _odd_params(j, n_ctx, od_w_in, od_gla_gate_w2, od_gla_gate_b, od_gla_out_norm, od_swa_q_norm,
                            od_swa_k_norm, od_swa_sink, od_w_out)
            gq, gk, gv, rg, bc, sq, sk, sv = _inproj_odd(xs, mod, norm_g[i, 1], p, rope_swa)
            o_f, o_b = _gla_scan(gq, gk, gv, bc, n_ctx=n_ctx)
            o_att = _swa_attention(sq, sk, sv, p["sink"], n_ctx=n_ctx)
            xs = _outproj(xs, mod, o_att, o_f, o_b, rg, p["out_norm"], p["w_out"], seg=GLA_DV, rec_first=True,
                          ctx_tiles=ctx_tiles, skip=skip, att_skip=0)
        xs = _ffn(xs, mod, norm_g[i, 2], wg[1], wu[1], wd[1], r0=6, n_ctx=0 if last else n_ctx)
    return xs
```

```python
import functools
import math

import jax
import jax.numpy as jnp
from jax import lax
from jax.experimental import pallas as pl
from jax.experimental.pallas import tpu as pltpu

F32 = jnp.float32
BF16 = jnp.bfloat16

D_MODEL = 1024
GRID_W = 64
D_FF = 2816
FFN_RES = 0.5
N_MOD = 9
EPS = 1e-6
ROPE_THETA = 10000.0
CHUNK = 64

MLA_HEADS = 8
MLA_NOPE = 64
MLA_ROPE = 32
MLA_QK = MLA_NOPE + MLA_ROPE
MLA_V = 64
MLA_Q_RANK = 384
MLA_KV_RANK = 256
MLA_VW = MLA_HEADS * MLA_V
MLA_STEP_HEADS = 4
MLA_Q_PRESCALE =MLA_QK ** -0.5 * math.log2(math.e)

GDN_HEADS = 8
GDN_DK = 64
GDN_DV = 64
GDN_CONV = 5
GDN_QK = GDN_HEADS * GDN_DK
GDN_VW = GDN_HEADS * GDN_DV
GDN_HALF = 256
GDN_SCAN_SAMPLES = 4

GLA_HEADS = 4
GLA_DK = 64
GLA_DV = 128
GLA_RANK = 16
GLA_TAU = 16.0
GLA_QK = GLA_HEADS * GLA_DK
GLA_VW = GLA_HEADS * GLA_DV
GLA_SCAN_SAMPLES = 4

SWA_HEADS = 8
SWA_KV_HEADS = 2
SWA_DH = 64
SWA_WINDOW = 128
SWA_QW = SWA_HEADS * SWA_DH
SWA_KW = SWA_KV_HEADS * SWA_DH

LANES = 128
HEAD_PAD = 128
TM = 256
FFN_TILES = (576, 512, 256, 128)
VMEM_LIMIT = 56 * 1024 * 1024

EVEN_COLS = MLA_Q_RANK + MLA_KV_RANK + LANES + 2 * GDN_QK + GDN_VW + GDN_VW + LANES
ODD_COLS = 2 * GLA_QK + GLA_VW + GLA_VW + SWA_QW + 2 * SWA_KW + LANES


def _dot(a, b):
    return jnp.dot(a.astype(BF16), b.astype(BF16), preferred_element_type=F32)


def _dot_nt(a, b):
    return lax.dot_general(a.astype(BF16), b.astype(BF16), (((1,), (1,)), ((), ())), preferred_element_type=F32)


def _dot_tn(a, b):
    return lax.dot_general(a.astype(BF16), b.astype(BF16), (((0,), (0,)), ((), ())), preferred_element_type=F32)


def _split2(x):
    hi = x.astype(BF16)
    lo = (x - hi.astype(F32)).astype(BF16)
    return hi, lo


def _split3(x):
    hi = x.astype(BF16)
    r = x - hi.astype(F32)
    mid = r.astype(BF16)
    lo = (r - mid.astype(F32)).astype(BF16)
    return hi, mid, lo


def _dot3(a, b):
    ah, al = _split2(a)
    bh, bl = _split2(b)
    d = functools.partial(jnp.dot, preferred_element_type=F32)
    return d(ah, bh) + (d(ah, bl) + d(al, bh))


def _dot_sel(sel, x):
    s = sel.astype(BF16)
    hi, mid, lo = _split3(x)
    d = functools.partial(jnp.dot, preferred_element_type=F32)
    return d(s, hi) + (d(s, mid) + d(s, lo))


def _dot_sel_r(x, sel):
    s = sel.astype(BF16)
    hi, mid, lo = _split3(x)
    d = functools.partial(jnp.dot, preferred_element_type=F32)
    return d(hi, s) + (d(mid, s) + d(lo, s))


def _seg_mean_sq(x, seg, n_real):
    w = x.shape[-1]
    r = lax.broadcasted_iota(jnp.int32, (w, w), 0) // seg
    c = lax.broadcasted_iota(jnp.int32, (w, w), 1) // seg
    ones_bd = (r == c).astype(BF16)
    hi, lo = _split2(x * x)
    d = functools.partial(jnp.dot, preferred_element_type=F32)
    return (d(hi, ones_bd) + d(lo, ones_bd)) * (1.0 / n_real)


def _rms_rows(x, g):
    ms = jnp.mean(x * x, axis=-1, keepdims=True)
    return x * lax.rsqrt(ms + EPS) * g


def _sigmoid(x):
    return 1.0 / (1.0 + jnp.exp(-x))


def _silu(x):
    return x * _sigmoid(x)


def _softplus(x):
    return jnp.maximum(x, 0.0) + jnp.log1p(jnp.exp(-jnp.abs(x)))


def _rope(x, c, s1, s2, half):
    return x * c + pltpu.roll(x, LANES - half, 1) * s1 + pltpu.roll(x, half, 1) * s2


def _chunk_tri(n, reverse):
    i = lax.broadcasted_iota(jnp.int32, (n, n), 0)
    j = lax.broadcasted_iota(jnp.int32, (n, n), 1)
    same = (i // CHUNK) == (j // CHUNK)
    tri = (j >= i) if reverse else (j <= i)
    return (same & tri).astype(F32)


def _ada_kernel(c_ref, w_ref, b_ref, o_ref):
    sc = _silu(c_ref[...])
    o_ref[...] = _dot(sc, w_ref[...]) + b_ref[...]


def _ada(cvec, ada_w, ada_b):
    depth, d, nm = ada_w.shape
    tn = 1024
    return pl.pallas_call(
        _ada_kernel,
        grid=(depth, nm // tn),
        in_specs=[
            pl.BlockSpec(cvec.shape, lambda i, j: (0, 0)),
            pl.BlockSpec((None, d, tn), lambda i, j: (i, 0, j)),
            pl.BlockSpec((None, 1, tn), lambda i, j: (i, 0, j)),
        ],
        out_specs=pl.BlockSpec((None, cvec.shape[0], tn), lambda i, j: (i, 0, j)),
        out_shape=jax.ShapeDtypeStruct((depth, cvec.shape[0], nm), F32),
        compiler_params=pltpu.CompilerParams(vmem_limit_bytes=VMEM_LIMIT),
        name="ada_mod",
    )(cvec, ada_w, ada_b.reshape(depth, 1, nm))


def _const_spec(shape):
    nd = len(shape)
    return pl.BlockSpec(shape, lambda *_: (0,) * nd, pipeline_mode=pl.Buffered(1))


def _tok_spec(width, skip=0):
    return pl.BlockSpec((None, TM, width), lambda b, t: (b, t + skip, 0))


def _mod_spec(ctx_tiles, skip=0):
    return pl.BlockSpec((None, None, N_MOD, D_MODEL),
                        lambda b, t: (b, jnp.where(t + skip >= ctx_tiles, 1, 0), 0, 0))


def _params():
    return pltpu.CompilerParams(dimension_semantics=("parallel", "arbitrary"), vmem_limit_bytes=VMEM_LIMIT)


def _ffn_kernel(*refs, r0, n_ctx, split):
    if split:
        c_ref, x_ref, mod_ref, g_ref, wg_ref, wu_ref, wd_ref, o_ref = refs
    else:
        x_ref, mod_ref, g_ref, wg_ref, wu_ref, wd_ref, o_ref = refs
    x = x_ref[...]
    m = mod_ref[...]
    tm = x.shape[0]
    if n_ctx:
        is_ctx = pl.program_id(1) * tm + lax.broadcasted_iota(jnp.int32, (tm, 1), 0) < n_ctx
        row = lambda r: jnp.where(is_ctx, m[0, r:r + 1], m[1, r:r + 1])
        if split:
            x = jnp.where(is_ctx, c_ref[...], x)
    else:
        row = lambda r: m[1, r:r + 1]
    h = _rms_rows(x, g_ref[...]) * (1.0 + row(r0 + 1)) + row(r0)
    hb = h.astype(BF16)
    a = jnp.dot(hb, wg_ref[...], preferred_element_type=F32)
    u = jnp.dot(hb, wu_ref[...], preferred_element_type=F32)
    act = (_silu(a) * u).astype(BF16)
    y = jnp.dot(act, wd_ref[...], preferred_element_type=F32)
    o_ref[...] = x + (FFN_RES * row(r0 + 2)) * y


def _ffn(xs, mod, g, wg, wu, wd, *, layer, half, r0, n_ctx, ctx=None):
    b, rows, d = xs.shape
    split = ctx is not None
    if split:
        tm = TM
        ctx_tiles = n_ctx // tm
        rows += n_ctx
        toks = [pl.BlockSpec((None, tm, d), lambda b_, t: (b_, jnp.minimum(t, ctx_tiles - 1), 0)),
                pl.BlockSpec((None, tm, d), lambda b_, t: (b_, jnp.maximum(t - ctx_tiles, 0), 0))]
        data = [ctx, xs]
    else:
        tm = next(t for t in FFN_TILES if rows % t == 0)
        toks = [pl.BlockSpec((None, tm, d), lambda b_, t: (b_, t, 0))]
        data = [xs]
    wspec = lambda w: pl.BlockSpec((None, None) + w.shape[2:], lambda b_, t: (layer, half, 0, 0),
                                   pipeline_mode=pl.Buffered(1))
    return pl.pallas_call(
        functools.partial(_ffn_kernel, r0=r0, n_ctx=n_ctx, split=split),
        grid=(b, rows // tm),
        in_specs=toks + [pl.BlockSpec((None, 2, N_MOD, d), lambda b_, t: (b_, 0, 0, 0)), _const_spec((1, d)),
                         wspec(wg), wspec(wu), wspec(wd)],
        out_specs=pl.BlockSpec((None, tm, d), lambda b_, t: (b_, t, 0)),
        out_shape=jax.ShapeDtypeStruct((b, rows, d), F32),
        compiler_params=_params(),
        name="ffn_half_step",
    )(*data, mod, g.reshape(1, d), wg, wu, wd)


def _inproj_even_kernel(x_ref, mod_ref, g_ref, win_ref, qan_ref, wq_ref, kvan_ref, wkk_ref, wkv_ref,
                        qn_ref, kn_ref, rc_ref, rs1_ref, rs2_ref,
                        q_ref, k_ref, v_ref, zg_ref, gate_ref, small_ref):
    x = x_ref[...]
    m = mod_ref[...]
    h = _rms_rows(x, g_ref[...]) * (1.0 + m[4:5]) + m[3:4]
    z = jnp.dot(h.astype(BF16), win_ref[...], preferred_element_type=F32)
    o = 0
    cq = z[:, o:o + MLA_Q_RANK]; o += MLA_Q_RANK
    ckv = z[:, o:o + MLA_KV_RANK]; o += MLA_KV_RANK
    kr = z[:, o:o + LANES]; o += LANES
    zg_ref[...] = z[:, o:o + 3 * GDN_QK]; o += 3 * GDN_QK
    gate_ref[...] = z[:, o:o + GDN_VW]; o += GDN_VW
    small_ref[...] = z[:, o:o + LANES]

    rc, rs1, rs2 = rc_ref[...], rs1_ref[...], rs2_ref[...]
    qn, kn = qn_ref[...], kn_ref[...]
    qf = jnp.dot(_rms_rows(cq, qan_ref[...]).astype(BF16), wq_ref[...], preferred_element_type=F32)
    ckvn = _rms_rows(ckv, kvan_ref[...]).astype(BF16)
    kf = jnp.dot(ckvn, wkk_ref[...], preferred_element_type=F32)
    lane = lax.broadcasted_iota(jnp.int32, (1, MLA_HEADS * HEAD_PAD), 1)
    ones_cols = (lane % HEAD_PAD >= MLA_V).astype(F32)
    v_ref[...] = (jnp.dot(ckvn, wkv_ref[...], preferred_element_type=F32) + ones_cols).astype(v_ref.dtype)
    slabs = [slice(hd * HEAD_PAD, (hd + 1) * HEAD_PAD) for hd in range(MLA_HEADS)]
    xs = [qf[:, sl] for sl in slabs] + [kf[:, sl] + kr for sl in slabs]
    gains = [qn * MLA_Q_PRESCALE] * MLA_HEADS + [kn] * MLA_HEADS
    ss = [jnp.sum(x * x, axis=-1, keepdims=True) for x in xs]
    xs = [x * lax.rsqrt(s * (1.0 / MLA_QK) + EPS) * g for x, s, g in zip(xs, ss, gains)]
    xs = [_rope(x, rc, rs1, rs2, MLA_ROPE // 2) for x in xs]
    for hd, sl in enumerate(slabs):
        q_ref[:, sl] = xs[hd].astype(q_ref.dtype)
        k_ref[:, sl] = xs[MLA_HEADS + hd].astype(k_ref.dtype)


def _inproj_even(xs, mod, g, p, rope):
    b, rows, d = xs.shape
    ctx_tiles = p["ctx_tiles"]
    consts = [g.reshape(1, d), p["w_in"], p["q_a_norm"], p["w_q"], p["kv_a_norm"], p["w_kk"], p["w_kv"],
              p["q_norm"], p["k_norm"]]
    rope_spec = pl.BlockSpec((TM, LANES), lambda b_, t: (t, 0))
    widths = [(MLA_HEADS * HEAD_PAD, BF16), (MLA_HEADS * HEAD_PAD, BF16), (MLA_HEADS * HEAD_PAD, BF16),
              (3 * GDN_QK, F32), (GDN_VW, F32), (LANES, F32)]
    return pl.pallas_call(
        _inproj_even_kernel,
        grid=(b, rows // TM),
        in_specs=[_tok_spec(d), _mod_spec(ctx_tiles)] + [_const_spec(c.shape) for c in consts] + [rope_spec] * 3,
        out_specs=[_tok_spec(w) for w, _ in widths],
        out_shape=[jax.ShapeDtypeStruct((b, rows, w), dt) for w, dt in widths],
        compiler_params=_params(),
        name="inproj_even",
    )(xs, mod, *consts, *rope)


def _mla_kernel(q_ref, k_ref, v_ref, o_ref, *, n_ctx, n_all, ctx_tiles):
    t = pl.program_id(2)
    lo = lax.broadcasted_iota(jnp.int32, (TM, LANES), 1) < MLA_V
    n_heads = q_ref.shape[-1] // HEAD_PAD

    def attend(nk):
        q = q_ref[...]
        sls = [slice(hh * HEAD_PAD, (hh + 1) * HEAD_PAD) for hh in range(n_heads)]
        s = [lax.dot_general(q[:, sl], k_ref[0:nk, sl], (((1,), (1,)), ((), ())), preferred_element_type=F32)
             for sl in sls]
        mx = [jnp.max(x, axis=-1, keepdims=True) for x in s]
        p = [jnp.exp2(x - m).astype(BF16) for x, m in zip(s, mx)]
        r = [jnp.dot(x, v_ref[0:nk, sl], preferred_element_type=F32) for x, sl in zip(p, sls)]
        outs = [x / x[:, MLA_V:MLA_V + 1] for x in r]
        for pr in range(n_heads // 2):
            o_ref[:, pr * LANES:(pr + 1) * LANES] = jnp.where(
                lo, outs[2 * pr], pltpu.roll(outs[2 * pr + 1], MLA_V, 1)).astype(o_ref.dtype)

    @pl.when(t < ctx_tiles)
    def _():
        attend(n_ctx)

    @pl.when(t >= ctx_tiles)
    def _():
        attend(n_all)


def _mla_attention(q, k, v, *, n_ctx):
    b, rows, _ = q.shape
    hs = MLA_STEP_HEADS
    return pl.pallas_call(
        functools.partial(_mla_kernel, n_ctx=n_ctx, n_all=rows, ctx_tiles=n_ctx // TM),
        grid=(b, MLA_HEADS // hs, rows // TM),
        in_specs=[
            pl.BlockSpec((None, TM, hs * HEAD_PAD), lambda b_, h, t: (b_, t, h)),
            pl.BlockSpec((None, rows, hs * HEAD_PAD), lambda b_, h, t: (b_, 0, h)),
            pl.BlockSpec((None, rows, hs * HEAD_PAD), lambda b_, h, t: (b_, 0, h)),
        ],
        out_specs=pl.BlockSpec((None, TM, hs * MLA_V), lambda b_, h, t: (b_, t, h)),
        out_shape=jax.ShapeDtypeStruct((b, rows, MLA_VW), BF16),
        compiler_params=pltpu.CompilerParams(dimension_semantics=("parallel", "parallel", "arbitrary"),
                                             vmem_limit_bytes=VMEM_LIMIT),
        name="mla_attention",
    )(q, k, v)


def _gdn_prep_kernel(z_ref, zp_ref, zn_ref, cw_ref, sm_ref, nega_ref, dtb_ref,
                     q_ref, k_ref, v_ref, gf_ref, gb_ref, *, ctx_tiles, n_tiles):
    t = pl.program_id(1)
    first = (t == 0) | (t == ctx_tiles)
    last = (t == ctx_tiles - 1) | (t == n_tiles - 1)
    z = z_ref[...]
    half = GDN_CONV // 2
    prev = jnp.where(first, 0.0, zp_ref[...][8 - half:, :])
    nxt = jnp.where(last, 0.0, zn_ref[...][:half, :])
    ext = jnp.concatenate([prev, z, nxt], axis=0)
    cw = cw_ref[...]
    acc = ext[0:TM] * cw[0:1]
    for j in range(1, GDN_CONV):
        acc = acc + ext[j:j + TM] * cw[j:j + 1]
    qkv = _silu(acc)
    q = qkv[:, :GDN_QK]
    k = qkv[:, GDN_QK:2 * GDN_QK]
    q_ref[...] = q * lax.rsqrt(_seg_mean_sq(q, GDN_DK, 1.0) + EPS) * (GDN_DK ** -0.5)
    k_ref[...] = k * lax.rsqrt(_seg_mean_sq(k, GDN_DK, 1.0) + EPS)
    v_ref[...] = qkv[:, 2 * GDN_QK:]

    sm = sm_ref[...]
    lane = lax.broadcasted_iota(jnp.int32, sm.shape, 1)
    g = nega_ref[...] * _softplus(sm + dtb_ref[...])
    g = jnp.where(lane < 2 * GDN_HEADS, g, 0.0)
    gc_f = _dot_sel(_chunk_tri(TM, False), g)
    gc_b = _dot_sel(_chunk_tri(TM, True), g)
    gates = jnp.where(lane < GDN_HEADS, gc_f, jnp.where(lane < 2 * GDN_HEADS, gc_b, _sigmoid(sm)))
    row = lax.broadcasted_iota(jnp.int32, (LANES, GDN_QK), 0)
    head = lax.broadcasted_iota(jnp.int32, (LANES, GDN_QK), 1) // GDN_DK
    for d, ref in enumerate((gf_ref, gb_ref)):
        ref[:, 0:GDN_QK] = _dot_sel_r(gates, (row == head + d * GDN_HEADS).astype(F32))
        ref[:, GDN_QK:2 * GDN_QK] = _dot_sel_r(gates, (row == head + (2 + d) * GDN_HEADS).astype(F32))


def _gdn_prep(zg, small, conv_w, nega, dtb, *, ctx_tiles):
    b, rows, w = zg.shape
    n_tiles = rows // TM
    hb = TM // 8
    n_hblk = rows // 8
    return pl.pallas_call(
        functools.partial(_gdn_prep_kernel, ctx_tiles=ctx_tiles, n_tiles=n_tiles),
        grid=(b, n_tiles),
        in_specs=[
            _tok_spec(w),
            pl.BlockSpec((None, 8, w), lambda b_, t: (b_, jnp.maximum(t * hb - 1, 0), 0)),
            pl.BlockSpec((None, 8, w), lambda b_, t: (b_, jnp.minimum((t + 1) * hb, n_hblk - 1), 0)),
            _const_spec(conv_w.shape), _tok_spec(LANES), _const_spec((1, LANES)), _const_spec((1, LANES)),
        ],
        out_specs=[_tok_spec(GDN_QK), _tok_spec(GDN_QK), _tok_spec(GDN_VW), _tok_spec(2 * GDN_QK),
                   _tok_spec(2 * GDN_QK)],
        out_shape=[jax.ShapeDtypeStruct((b, rows, GDN_QK), F32), jax.ShapeDtypeStruct((b, rows, GDN_QK), F32),
                   jax.ShapeDtypeStruct((b, rows, GDN_VW), F32), jax.ShapeDtypeStruct((b, rows, 2 * GDN_QK), F32),
                   jax.ShapeDtypeStruct((b, rows, 2 * GDN_QK), F32)],
        compiler_params=_params(),
        name="gdn_prep",
    )(zg, zg, zg, conv_w, small, nega, dtb)


def _bd_halves(y, bdm):
    yb = y.astype(BF16)
    out = []
    for s in range(2):
        t = jnp.concatenate([yb[:, s * GDN_HALF:(s + 1) * GDN_HALF]] * (GDN_HALF // GDN_DK), axis=0)
        out.append(jnp.where(bdm, t, jnp.zeros_like(t)))
    return out


def _hprod(x, bd):
    xb = x.astype(BF16)
    d = functools.partial(jnp.dot, preferred_element_type=F32)
    return jnp.concatenate([d(xb[:, 0:GDN_HALF], bd[0]), d(xb[:, GDN_HALF:], bd[1])], axis=1)


def _hprod3(x, y, bdm):
    xh, xl = _split2(x)
    yh, yl = _split2(y)
    bh, bl = _bd_halves(yh, bdm), _bd_halves(yl, bdm)
    return _hprod(xh, bh) + (_hprod(xh, bl) + _hprod(xl, bh))


def _gdn_chunks(chains):
    ii = lax.broadcasted_iota(jnp.int32, (CHUNK, GDN_QK), 0)
    jj = lax.broadcasted_iota(jnp.int32, (CHUNK, GDN_QK), 1) % CHUNK
    r = lax.broadcasted_iota(jnp.int32, (GDN_HALF, GDN_HALF), 0) // GDN_DK
    c = lax.broadcasted_iota(jnp.int32, (GDN_HALF, GDN_HALF), 1) // GDN_DK
    bdm = r == c
    diag = ii == jj
    eye = diag.astype(F32)
    n = len(chains)
    qs, ks, vs, gs, srefs, revs = (list(t) for t in zip(*chains))
    each = lambda f, *ls: [f(*a) for a in zip(*ls)]
    bd = lambda ys: [_bd_halves(y, bdm) for y in ys]

    incl = [(ii <= jj) if rv else (ii >= jj) for rv in revs]
    strict = [(ii < jj) if rv else (ii > jj) for rv in revs]
    last = [0 if rv else CHUNK - 1 for rv in revs]
    gc = [g[:, 0:GDN_QK] for g in gs]
    beta = [g[:, GDN_QK:2 * GDN_QK] for g in gs]
    gc_row = [jnp.sum(jnp.where(diag, x, 0.0), axis=0, keepdims=True) for x in gc]
    decay = each(lambda x, xr, ic: jnp.exp(jnp.where(ic, x - xr, -jnp.inf)), gc, gc_row, incl)

    def kt_bd(k):
        kt = k.T.astype(BF16)
        out = []
        for s in range(2):
            t = jnp.concatenate([kt[s * GDN_HALF:(s + 1) * GDN_HALF]] * (GDN_HALF // CHUNK), axis=1)
            out.append(jnp.where(bdm, t, jnp.zeros_like(t)))
        return out

    kkqk = each(lambda k, q, rk: _hprod(jnp.concatenate([k, q], axis=0), rk), ks, qs, [kt_bd(k) for k in ks])
    a = each(lambda st, b, x, dc: jnp.where(st, b * x[0:CHUNK] * dc, 0.0), strict, beta, kkqk, decay)
    qk = each(lambda x, dc: x[CHUNK:2 * CHUNK] * dc, kkqk, decay)

    def level_mask(blk, rv):
        same = (ii // (2 * blk)) == (jj // (2 * blk))
        off = (ii // blk) < (jj // blk) if rv else (ii // blk) > (jj // blk)
        return same & off

    m = each(lambda x, rv: eye - jnp.where(level_mask(1, rv), x, 0.0), a, revs)
    blk = 2
    while blk < CHUNK:
        am = each(lambda x, rv: jnp.where(level_mask(blk, rv), x, 0.0), a, revs)
        x = each(_hprod, am, bd(m))
        m = each(lambda mm, y, ybd: mm - _hprod(mm, ybd), m, x, bd(x))
        blk *= 2
    am = each(lambda x, mm: _hprod3(x, mm, bdm), a, m)
    resid = each(lambda mm, y: eye - mm - y, m, am)
    m = each(lambda mm, rbd: mm + _hprod(mm, rbd), m, bd(resid))

    egc = [jnp.exp(x) for x in gc]
    u = each(_hprod, m, bd(each(lambda v, b: v * b, vs, beta)))
    w = each(_hprod, m, bd(each(lambda k, b, e: k * (b * e), ks, beta, egc)))
    g_last = each(lambda x, l: x[l:l + 1, :], gc, last)
    s0 = [sr[0] for sr in srefs]
    s1 = [sr[1] for sr in srefs]

    def sprod(x, a0, a1):
        return jnp.concatenate([_dot(x[:, 0:GDN_HALF], a0), _dot(x[:, GDN_HALF:], a1)], axis=1)

    ws = each(lambda ww, q, e, a0, a1: sprod(jnp.concatenate([ww, q * e], axis=0), a0, a1), w, qs, egc, s0, s1)
    v_new = each(lambda uu, x: uu - x[0:CHUNK], u, ws)
    o = each(lambda x, y, vbd: x[CHUNK:2 * CHUNK] + _hprod(y, vbd), ws, qk, bd(v_new))
    k_dec = each(lambda k, gl, x: k * jnp.exp(gl - x), ks, g_last, gc)
    e_last = [jnp.exp(gl) for gl in g_last]
    for i in range(n):
        for s, st in enumerate((s0[i], s1[i])):
            sl = slice(s * GDN_HALF, (s + 1) * GDN_HALF)
            srefs[i][s] = st * e_last[i][:, sl] + jnp.where(bdm, _dot_tn(k_dec[i][:, sl], v_new[i][:, sl]), 0.0)
    return o


def _gdn_scan_kernel(qf_ref, kf_ref, vf_ref, gf_ref, qb_ref, kb_ref, vb_ref, gb_ref, of_ref, ob_ref, sf_ref, sb_ref):
    @pl.when(pl.program_id(1) == 0)
    def _():
        sf_ref[...] = jnp.zeros_like(sf_ref)
        sb_ref[...] = jnp.zeros_like(sb_ref)

    nb = qf_ref.shape[0]
    chains = []
    for i in range(nb):
        chains.append((qf_ref[i], kf_ref[i], vf_ref[i], gf_ref[i], sf_ref.at[i], False))
        chains.append((qb_ref[i], kb_ref[i], vb_ref[i], gb_ref[i], sb_ref.at[i], True))
    outs = _gdn_chunks(chains)
    for i in range(nb):
        of_ref[i] = outs[2 * i]
        ob_ref[i] = outs[2 * i + 1]


def _scan_chunk_maps(ctx_chunks, n_chunks):
    fwd = lambda s: s
    bwd = lambda s: jnp.where(s < ctx_chunks, ctx_chunks - 1 - s, n_chunks - 1 - (s - ctx_chunks))
    return fwd, bwd


def _gdn_scan(q, k, v, g_f, g_b, *, n_ctx):
    b, rows, _ = q.shape
    n_chunks = rows // CHUNK
    fwd, bwd = _scan_chunk_maps(n_ctx // CHUNK, n_chunks)

    nb = GDN_SCAN_SAMPLES if b % GDN_SCAN_SAMPLES == 0 else 1

    def specs(cm):
        tok = lambda w: pl.BlockSpec((nb, CHUNK, w), lambda b_, s: (b_, cm(s), 0))
        return [tok(GDN_QK), tok(GDN_QK), tok(GDN_VW), tok(2 * GDN_QK)]

    return pl.pallas_call(
        _gdn_scan_kernel,
        grid=(b // nb, n_chunks),
        in_specs=specs(fwd) + specs(bwd),
        out_specs=[pl.BlockSpec((nb, CHUNK, GDN_VW), lambda b_, s: (b_, fwd(s), 0)),
                   pl.BlockSpec((nb, CHUNK, GDN_VW), lambda b_, s: (b_, bwd(s), 0))],
        out_shape=[jax.ShapeDtypeStruct((b, rows, GDN_VW), F32)] * 2,
        scratch_shapes=[pltpu.VMEM((nb, 2, GDN_HALF, GDN_HALF), F32)] * 2,
        compiler_params=_params(),
        name="gdn_scan",
    )(q, k, v, g_f, q, k, v, g_b)


def _outproj_kernel(x_ref, mod_ref, oa_ref, of_ref, ob_ref, gate_ref, gn_ref, w_ref, o_ref, *, seg, rec_first):
    o = of_ref[...] + ob_ref[...]
    y = o * lax.rsqrt(_seg_mean_sq(o, seg, float(seg)) + EPS) * gn_ref[...] * _silu(gate_ref[...])
    wr = y.shape[-1]
    wa = oa_ref.shape[-1]
    if rec_first:
        out = jnp.dot(y.astype(BF16), w_ref[0:wr, :], preferred_element_type=F32)
        out = out + jnp.dot(oa_ref[...], w_ref[wr:wr + wa, :], preferred_element_type=F32)
    else:
        out = jnp.dot(oa_ref[...], w_ref[0:wa, :], preferred_element_type=F32)
        out = out + jnp.dot(y.astype(BF16), w_ref[wa:wa + wr, :], preferred_element_type=F32)
    o_ref[...] = x_ref[...] + mod_ref[...][5:6] * out


def _outproj(xs, mod, o_att, o_f, o_b, gate, gn, w_out, *, seg, rec_first, ctx_tiles, skip=0, att_skip=0):
    b, rows, d = xs.shape
    out_rows = rows - skip * TM
    return pl.pallas_call(
        functools.partial(_outproj_kernel, seg=seg, rec_first=rec_first),
        grid=(b, out_rows // TM),
        in_specs=[_tok_spec(d, skip), _mod_spec(ctx_tiles, skip), _tok_spec(o_att.shape[-1], att_skip),
                  _tok_spec(o_f.shape[-1], skip), _tok_spec(o_b.shape[-1], skip), _tok_spec(gate.shape[-1], skip),
                  _const_spec(gn.shape), _const_spec(w_out.shape)],
        out_specs=_tok_spec(d),
        out_shape=jax.ShapeDtypeStruct((b, out_rows, d), F32),
        compiler_params=_params(),
        name="outproj",
    )(xs, mod, o_att, o_f, o_b, gate, gn, w_out)


def _inproj_odd_kernel(x_ref, mod_ref, g_ref, win_ref, w2_ref, b2_ref, qn_ref, kn_ref, rc_ref, rs1_ref, rs2_ref,
                       gq_ref, gk_ref, gv_ref, rg_ref, bc_ref, sq_ref, sk_ref, sv_ref):
    x = x_ref[...]
    m = mod_ref[...]
    h = _rms_rows(x, g_ref[...]) * (1.0 + m[4:5]) + m[3:4]
    z = jnp.dot(h.astype(BF16), win_ref[...], preferred_element_type=F32)
    o = 0
    gq_ref[...] = z[:, o:o + GLA_QK]; o += GLA_QK
    gk_ref[...] = z[:, o:o + GLA_QK]; o += GLA_QK
    gv_ref[...] = z[:, o:o + GLA_VW]; o += GLA_VW
    rg_ref[...] = z[:, o:o + GLA_VW]; o += GLA_VW
    sq = z[:, o:o + SWA_QW]; o += SWA_QW
    sk = z[:, o:o + SWA_KW]; o += SWA_KW
    sv_ref[...] = z[:, o:o + SWA_KW].astype(sv_ref.dtype); o += SWA_KW
    lowrank = z[:, o:o + LANES]

    logit = _dot(lowrank, w2_ref[...]) + b2_ref[...]
    log_a = (jnp.minimum(logit, 0.0) - jnp.log1p(jnp.exp(-jnp.abs(logit)))) * (1.0 / GLA_TAU)
    bc_ref[:, 0:GLA_QK] = _dot_sel(_chunk_tri(TM, False), log_a[:, 0:GLA_QK])
    bc_ref[:, GLA_QK:2 * GLA_QK] = _dot_sel(_chunk_tri(TM, True), log_a[:, GLA_QK:2 * GLA_QK])

    rc, rs1, rs2 = rc_ref[...], rs1_ref[...], rs2_ref[...]
    sqn = sq * lax.rsqrt(_seg_mean_sq(sq, SWA_DH, float(SWA_DH)) + EPS) * qn_ref[...]
    for s in range(SWA_QW // LANES):
        sl = slice(s * LANES, (s + 1) * LANES)
        sq_ref[:, sl] = _rope(sqn[:, sl], rc, rs1, rs2, SWA_DH // 2).astype(sq_ref.dtype)
    skn = sk * lax.rsqrt(_seg_mean_sq(sk, SWA_DH, float(SWA_DH)) + EPS) * kn_ref[...]
    sk_ref[...] = _rope(skn, rc, rs1, rs2, SWA_DH // 2).astype(sk_ref.dtype)


def _inproj_odd(xs, mod, g, p, rope):
    b, rows, d = xs.shape
    consts = [g.reshape(1, d), p["w_in"], p["w2"], p["b2"], p["q_norm"], p["k_norm"]]
    rope_spec = pl.BlockSpec((TM, LANES), lambda b_, t: (t, 0))
    widths = [(GLA_QK, F32), (GLA_QK, F32), (GLA_VW, F32), (GLA_VW, F32), (2 * GLA_QK, F32),
              (SWA_QW, BF16), (SWA_KW, BF16), (SWA_KW, BF16)]
    return pl.pallas_call(
        _inproj_odd_kernel,
        grid=(b, rows // TM),
        in_specs=[_tok_spec(d), _mod_spec(p["ctx_tiles"])] + [_const_spec(c.shape) for c in consts] + [rope_spec] * 3,
        out_specs=[_tok_spec(w) for w, _ in widths],
        out_shape=[jax.ShapeDtypeStruct((b, rows, w), dt) for w, dt in widths],
        compiler_params=_params(),
        name="inproj_odd",
    )(xs, mod, *consts, *rope)


def _gla_chunk(q, k, v, bc, st_ref, reverse):
    def blk(shape, rdiv, cdiv):
        return (lax.broadcasted_iota(jnp.int32, shape, 0) // rdiv) == (lax.broadcasted_iota(jnp.int32, shape, 1) // cdiv)

    ii = lax.broadcasted_iota(jnp.int32, (CHUNK, GLA_QK), 0)
    jj = lax.broadcasted_iota(jnp.int32, (CHUNK, GLA_QK), 1) % CHUNK
    incl = (ii <= jj) if reverse else (ii >= jj)
    last = 0 if reverse else CHUNK - 1
    q_dec = (q * (GLA_DK ** -0.5)) * jnp.exp(bc)
    k_inv = k * jnp.exp(-bc)
    b_last = bc[last:last + 1, :]
    k_dec = k * jnp.exp(b_last - bc)
    e_last = jnp.exp(b_last)
    kt = jnp.concatenate([k_inv.T.astype(BF16)] * GLA_HEADS, axis=1)
    rk = jnp.where(blk((GLA_QK, GLA_QK), GLA_DK, CHUNK), kt, jnp.zeros_like(kt))
    attn = jnp.where(incl, jnp.dot(q_dec.astype(BF16), rk, preferred_element_type=F32), 0.0)
    vt = jnp.concatenate([v.astype(BF16)] * GLA_HEADS, axis=0)
    vbd = jnp.where(blk((GLA_QK, GLA_VW), CHUNK, GLA_DV), vt, jnp.zeros_like(vt))
    st = st_ref[...]
    o = jnp.dot(attn.astype(BF16), vbd, preferred_element_type=F32) + _dot_nt(q_dec, st)
    st_ref[...] = st * e_last + jnp.where(blk((GLA_VW, GLA_QK), GLA_DV, GLA_DK), _dot_tn(v, k_dec), 0.0)
    return o


def _gla_scan_kernel(qf_ref, kf_ref, vf_ref, bf_ref, qb_ref, kb_ref, vb_ref, bb_ref, of_ref, ob_ref, sf_ref, sb_ref):
    @pl.when(pl.program_id(1) == 0)
    def _():
        sf_ref[...] = jnp.zeros_like(sf_ref)
        sb_ref[...] = jnp.zeros_like(sb_ref)

    for i in range(qf_ref.shape[0]):
        of_ref[i] = _gla_chunk(qf_ref[i], kf_ref[i], vf_ref[i], bf_ref[i][:, 0:GLA_QK], sf_ref.at[i], False)
        ob_ref[i] = _gla_chunk(qb_ref[i], kb_ref[i], vb_ref[i], bb_ref[i][:, GLA_QK:2 * GLA_QK], sb_ref.at[i], True)


def _gla_scan(q, k, v, bc, *, n_ctx):
    b, rows, _ = q.shape
    n_chunks = rows // CHUNK
    fwd, bwd = _scan_chunk_maps(n_ctx // CHUNK, n_chunks)
    nb = GLA_SCAN_SAMPLES if b % GLA_SCAN_SAMPLES == 0 else 1

    def specs(cm):
        tok = lambda w: pl.BlockSpec((nb, CHUNK, w), lambda b_, s: (b_, cm(s), 0))
        return [tok(GLA_QK), tok(GLA_QK), tok(GLA_VW), tok(2 * GLA_QK)]

    return pl.pallas_call(
        _gla_scan_kernel,
        grid=(b // nb, n_chunks),
        in_specs=specs(fwd) + specs(bwd),
        out_specs=[pl.BlockSpec((nb, CHUNK, GLA_VW), lambda b_, s: (b_, fwd(s), 0)),
                   pl.BlockSpec((nb, CHUNK, GLA_VW), lambda b_, s: (b_, bwd(s), 0))],
        out_shape=[jax.ShapeDtypeStruct((b, rows, GLA_VW), F32)] * 2,
        scratch_shapes=[pltpu.VMEM((nb, GLA_VW, GLA_QK), F32)] * 2,
        compiler_params=_params(),
        name="gla_scan",
    )(q, k, v, bc, q, k, v, bc)


def _swa_kernel(q_ref, k_ref, v_ref, sink_ref, o_ref, *, n_ctx, n_lat):
    w = SWA_WINDOW
    n = pl.program_id(1)
    nb = n_lat // w
    scale = SWA_DH ** -0.5

    def rows(ref, blk):
        return ref[pl.ds(pl.multiple_of(n_ctx + blk * w, w), w), :]

    pb = jnp.maximum(n - 1, 0)
    xb = jnp.minimum(n + 1, nb - 1)
    k_cat = jnp.concatenate([rows(k_ref, pb), rows(k_ref, n), rows(k_ref, xb), k_ref[0:n_ctx, :]], axis=0)
    v_cat = jnp.concatenate([rows(v_ref, pb), rows(v_ref, n), rows(v_ref, xb), v_ref[0:n_ctx, :]], axis=0)
    ii = lax.broadcasted_iota(jnp.int32, (w, w), 0)
    jj = lax.broadcasted_iota(jnp.int32, (w, w), 1)
    ninf = -jnp.inf
    bias = jnp.concatenate([
        jnp.where((jj >= ii) & (n > 0), 0.0, ninf),
        jnp.zeros((w, w), F32),
        jnp.where((jj <= ii) & (n < nb - 1), 0.0, ninf),
        jnp.zeros((w, n_ctx), F32)], axis=1)
    half = SWA_HEADS // 2
    lo = lax.broadcasted_iota(jnp.int32, (half * w, LANES), 1) < SWA_DH
    q = q_ref[...]
    qst = jnp.concatenate([q[:, j * LANES:(j + 1) * LANES] for j in range(half)], axis=0)
    bias4 = jnp.concatenate([bias] * half, axis=0)
    sink = sink_ref[...]
    res = []
    for g in range(SWA_KV_HEADS):
        qm = jnp.where(lo if g == 0 else jnp.logical_not(lo), qst, jnp.zeros_like(qst))
        s = lax.dot_general(qm, k_cat, (((1,), (1,)), ((), ())), preferred_element_type=F32) * scale + bias4
        sk = jnp.concatenate([jnp.broadcast_to(sink[0:1, half * g + j:half * g + j + 1], (w, 1)) for j in range(half)],
                             axis=0)
        mx = jnp.maximum(jnp.max(s, axis=-1, keepdims=True), sk)
        p = jnp.exp(s - mx)
        den = jnp.sum(p, axis=-1, keepdims=True) + jnp.exp(sk - mx)
        res.append(jnp.dot(p.astype(BF16), v_cat, preferred_element_type=F32) / den)
    o = jnp.where(lo, res[0], res[1]).astype(o_ref.dtype)
    for j in range(half):
        o_ref[:, j * LANES:(j + 1) * LANES] = o[j * w:(j + 1) * w, :]


def _swa_attention(q, k, v, sink, *, n_ctx):
    b, rows, _ = q.shape
    n_lat = rows - n_ctx
    w = SWA_WINDOW
    skip = n_ctx // w
    return pl.pallas_call(
        functools.partial(_swa_kernel, n_ctx=n_ctx, n_lat=n_lat),
        grid=(b, n_lat // w),
        in_specs=[
            pl.BlockSpec((None, w, SWA_QW), lambda b_, n: (b_, n + skip, 0)),
            pl.BlockSpec((None, rows, SWA_KW), lambda b_, n: (b_, 0, 0)),
            pl.BlockSpec((None, rows, SWA_KW), lambda b_, n: (b_, 0, 0)),
            _const_spec((1, LANES)),
        ],
        out_specs=pl.BlockSpec((None, w, SWA_QW), lambda b_, n: (b_, n, 0)),
        out_shape=jax.ShapeDtypeStruct((b, n_lat, SWA_QW), BF16),
        compiler_params=_params(),
        name="swa_attention",
    )(q, k, v, sink)


def _rope_tables(n_lat, n_ctx, rot_dim):
    t = jnp.arange(n_lat)
    row = (t // GRID_W).astype(F32)
    col = (t % GRID_W).astype(F32)
    n_freq = rot_dim // 4
    inv = ROPE_THETA ** (-jnp.arange(n_freq, dtype=F32) / n_freq)
    ang = jnp.concatenate([row[:, None] * inv, col[:, None] * inv], axis=-1)
    half = rot_dim // 2
    cos = jnp.concatenate([jnp.ones((n_ctx, half), F32), jnp.cos(ang)], axis=0)
    sin = jnp.concatenate([jnp.zeros((n_ctx, half), F32), jnp.sin(ang)], axis=0)
    rows = n_ctx + n_lat
    one = lambda w: jnp.ones((rows, w), F32)
    zero = lambda w: jnp.zeros((rows, w), F32)
    if rot_dim == MLA_ROPE:
        c = jnp.concatenate([one(MLA_NOPE), cos, cos, one(HEAD_PAD - MLA_QK)], axis=1)
        s1 = jnp.concatenate([zero(MLA_NOPE), -sin, zero(half), zero(HEAD_PAD - MLA_QK)], axis=1)
        s2 = jnp.concatenate([zero(MLA_NOPE), zero(half), sin, zero(HEAD_PAD - MLA_QK)], axis=1)
    else:
        c = jnp.concatenate([cos, cos, cos, cos], axis=1)
        s1 = jnp.concatenate([-sin, zero(half), -sin, zero(half)], axis=1)
        s2 = jnp.concatenate([zero(half), sin, zero(half), sin], axis=1)
    return c, s1, s2


def _even_params(j, n_ctx, ev_w_in, ev_q_a_norm, ev_w_q_up, ev_kv_a_norm, ev_w_kv_up, ev_mla_q_norm, ev_mla_k_norm,
                 ev_gdn_conv, ev_gdn_a_log, ev_gdn_dt_bias, ev_gdn_out_norm, ev_w_out):
    w = ev_w_in[j]
    d = w.shape[0]
    z = lambda n: jnp.zeros((d, n), F32)
    o_kr = MLA_Q_RANK + MLA_KV_RANK
    o_g = o_kr + MLA_ROPE
    o_small = o_g + 3 * GDN_QK
    o_gate = o_small + 4 * GDN_HEADS
    w_in = jnp.concatenate([
        w[:, :o_kr], z(MLA_NOPE), w[:, o_kr:o_g], z(HEAD_PAD - MLA_QK),
        w[:, o_g:o_small], w[:, o_gate:o_gate + GDN_VW],
        w[:, o_small:o_gate], z(LANES - 4 * GDN_HEADS)], axis=1).astype(BF16)
    pad_h = HEAD_PAD - MLA_QK
    w_q = jnp.pad(ev_w_q_up[j].reshape(MLA_Q_RANK, MLA_HEADS, MLA_QK), ((0, 0), (0, 0), (0, pad_h)))
    wkv = ev_w_kv_up[j].reshape(MLA_KV_RANK, MLA_HEADS, MLA_NOPE + MLA_V)
    w_kk = jnp.pad(wkv[:, :, :MLA_NOPE], ((0, 0), (0, 0), (0, HEAD_PAD - MLA_NOPE)))
    lane_row = lambda vec: jnp.pad(vec, (0, LANES - vec.shape[0])).reshape(1, LANES)
    fb = lambda a: jnp.concatenate([a[0], a[1]])
    return {
        "ctx_tiles": n_ctx // TM,
        "w_in": w_in,
        "q_a_norm": ev_q_a_norm[j].reshape(1, -1),
        "w_q": w_q.reshape(MLA_Q_RANK, MLA_HEADS * HEAD_PAD).astype(BF16),
        "kv_a_norm": ev_kv_a_norm[j].reshape(1, -1),
        "w_kk": w_kk.reshape(MLA_KV_RANK, MLA_HEADS * HEAD_PAD).astype(BF16),
        "w_kv": jnp.pad(wkv[:, :, MLA_NOPE:], ((0, 0), (0, 0), (0, HEAD_PAD - MLA_V))
                        ).reshape(MLA_KV_RANK, MLA_HEADS * HEAD_PAD).astype(BF16),
        "q_norm": lane_row(ev_mla_q_norm[j]),
        "k_norm": lane_row(ev_mla_k_norm[j]),
        "conv_w": ev_gdn_conv[j],
        "neg_a": lane_row(-jnp.exp(fb(ev_gdn_a_log[j]))),
        "dt_bias": lane_row(fb(ev_gdn_dt_bias[j])),
        "out_norm": jnp.tile(ev_gdn_out_norm[j], GDN_HEADS).reshape(1, GDN_VW),
        "w_out": ev_w_out[j].astype(BF16),
    }


def _swa_head_perm():
    half = SWA_HEADS // 2
    heads = [h for j in range(half) for h in (j, half + j)]
    return jnp.concatenate([jnp.arange(SWA_DH) + h * SWA_DH for h in heads])


def _odd_params(j, n_ctx, od_w_in, od_gla_gate_w2, od_gla_gate_b, od_gla_out_norm, od_swa_q_norm, od_swa_k_norm,
                od_swa_sink, od_w_out):
    w = od_w_in[j]
    d = w.shape[0]
    o_gate = 2 * GLA_QK + GLA_VW
    o_rg = o_gate + 2 * GLA_RANK
    o_sq = o_rg + GLA_VW
    o_sk = o_sq + SWA_QW
    perm = _swa_head_perm()
    w_in = jnp.concatenate([
        w[:, :o_gate], w[:, o_rg:o_sq], w[:, o_sq:o_sk][:, perm], w[:, o_sk:],
        w[:, o_gate:o_rg], jnp.zeros((d, LANES - 2 * GLA_RANK), F32)], axis=1).astype(BF16)
    w2 = jnp.zeros((LANES, 2 * GLA_QK), F32)
    w2 = w2.at[0:GLA_RANK, 0:GLA_QK].set(od_gla_gate_w2[j, 0])
    w2 = w2.at[GLA_RANK:2 * GLA_RANK, GLA_QK:].set(od_gla_gate_w2[j, 1])
    wo = od_w_out[j]
    w_out = jnp.concatenate([wo[:GLA_VW], wo[GLA_VW:][perm]], axis=0).astype(BF16)
    return {
        "ctx_tiles": n_ctx // TM,
        "w_in": w_in,
        "w2": w2.astype(BF16),
        "b2": jnp.concatenate([od_gla_gate_b[j, 0], od_gla_gate_b[j, 1]]).reshape(1, 2 * GLA_QK),
        "q_norm": jnp.tile(od_swa_q_norm[j], SWA_HEADS).reshape(1, SWA_QW),
        "k_norm": jnp.tile(od_swa_k_norm[j], SWA_KV_HEADS).reshape(1, SWA_KW),
        "sink": jnp.pad(od_swa_sink[j], (0, LANES - SWA_HEADS)).reshape(1, LANES),
        "out_norm": jnp.tile(od_gla_out_norm[j], GLA_HEADS).reshape(1, GLA_VW),
        "w_out": w_out,
    }


def kernel(x, c, ctx, c_ctx, ada_w, ada_b, norm_g, ffn_w_gate, ffn_w_up, ffn_w_down, ev_w_in, ev_q_a_norm, ev_w_q_up, ev_kv_a_norm, ev_w_kv_up, ev_mla_q_norm, ev_mla_k_norm, ev_gdn_conv, ev_gdn_a_log, ev_gdn_dt_bias, ev_gdn_out_norm, ev_w_out, od_w_in, od_gla_gate_w2, od_gla_gate_b, od_gla_out_norm, od_swa_q_norm, od_swa_k_norm, od_swa_sink, od_w_out):
    b, n_lat, d = x.shape
    n_ctx = ctx.shape[1]
    depth = ada_w.shape[0]
    assert d == D_MODEL and n_ctx % TM == 0 and n_lat % TM == 0 and n_lat % GRID_W == 0
    assert depth % 2 == 0, "the last layer must be an odd (GLA/SWA) layer: context outputs of that mixer are not built"
    ctx_tiles = n_ctx // TM

    cvec = jnp.concatenate([c, c_ctx[None, :], jnp.zeros((16 - b - 1, d), F32)], axis=0)
    mod_all = _ada(cvec, ada_w, ada_b).reshape(depth, 16, N_MOD, d)
    rope_mla = _rope_tables(n_lat, n_ctx, MLA_ROPE)
    rope_swa = _rope_tables(n_lat, n_ctx, SWA_DH)

    wg, wu, wd = (t.astype(BF16) for t in (ffn_w_gate, ffn_w_up, ffn_w_down))
    xs = x
    for i in range(depth):
        last = i == depth - 1
        mod = jnp.stack([jnp.broadcast_to(mod_all[i, b][None], (b, N_MOD, d)), mod_all[i, :b]], axis=1)
        xs = _ffn(xs, mod, norm_g[i, 0], wg, wu, wd, layer=i, half=0, r0=0, n_ctx=n_ctx, ctx=ctx if i == 0 else None)
        j = i // 2
        skip = ctx_tiles if last else 0
        if i % 2 == 0:
            p = _even_params(j, n_ctx, ev_w_in, ev_q_a_norm, ev_w_q_up, ev_kv_a_norm, ev_w_kv_up, ev_mla_q_norm,
                             ev_mla_k_norm, ev_gdn_conv, ev_gdn_a_log, ev_gdn_dt_bias, ev_gdn_out_norm, ev_w_out)
            q, k, v, zg, gate, small = _inproj_even(xs, mod, norm_g[i, 1], p, rope_mla)
            o_att = _mla_attention(q, k, v, n_ctx=n_ctx)
            gq, gk, gv, g_f, g_b = _gdn_prep(zg, small, p["conv_w"], p["neg_a"], p["dt_bias"], ctx_tiles=ctx_tiles)
            o_f, o_b = _gdn_scan(gq, gk, gv, g_f, g_b, n_ctx=n_ctx)
            xs = _outproj(xs, mod, o_att, o_f, o_b, gate, p["out_norm"], p["w_out"], seg=GDN_DV, rec_first=False,
                          ctx_tiles=ctx_tiles, skip=skip, att_skip=skip)
        else:
            assert last, "odd layers that must also produce context outputs are not built"
            p = _odd_params(j, n_ctx, od_w_in, od_gla_gate_w2, od_gla_gate_b, od_gla_out_norm, od_swa_q_norm,
                            od_swa_k_norm, od_swa_sink, od_w_out)
            gq, gk, gv, rg, bc, sq, sk, sv = _inproj_odd(xs, mod, norm_g[i, 1], p, rope_swa)
            o_f, o_b = _gla_scan(gq, gk, gv, bc, n_ctx=n_ctx)
            o_att = _swa_attention(sq, sk, sv, p["sink"], n_ctx=n_ctx)
            xs = _outproj(xs, mod, o_att, o_f, o_b, rg, p["out_norm"], p["w_out"], seg=GLA_DV, rec_first=True,
                          ctx_tiles=ctx_tiles, skip=skip, att_skip=0)
        xs = _ffn(xs, mod, norm_g[i, 2], wg, wu, wd, layer=i, half=1, r0=6, n_ctx=0 if last else n_ctx)
    return xs
```

```python
import functools
import math

import jax
import jax.numpy as jnp
from jax import lax
from jax.experimental import pallas as pl
from jax.experimental.pallas import tpu as pltpu

F32 = jnp.float32
BF16 = jnp.bfloat16

D_MODEL = 1024
GRID_W = 64
D_FF = 2816
FFN_RES = 0.5
N_MOD = 9
EPS = 1e-6
ROPE_THETA = 10000.0
CHUNK = 64

MLA_HEADS = 8
MLA_NOPE = 64
MLA_ROPE = 32
MLA_QK = MLA_NOPE + MLA_ROPE
MLA_V = 64
MLA_Q_RANK = 384
MLA_KV_RANK = 256
MLA_VW = MLA_HEADS * MLA_V
MLA_STEP_HEADS = 4
MLA_Q_PRESCALE =MLA_QK ** -0.5 * math.log2(math.e)

GDN_HEADS = 8
GDN_DK = 64
GDN_DV = 64
GDN_CONV = 5
GDN_QK = GDN_HEADS * GDN_DK
GDN_VW = GDN_HEADS * GDN_DV
GDN_HALF = 256
GDN_SCAN_SAMPLES = 8

GLA_HEADS = 4
GLA_DK = 64
GLA_DV = 128
GLA_RANK = 16
GLA_TAU = 16.0
GLA_QK = GLA_HEADS * GLA_DK
GLA_VW = GLA_HEADS * GLA_DV
GLA_SCAN_SAMPLES = 4

SWA_HEADS = 8
SWA_KV_HEADS = 2
SWA_DH = 64
SWA_WINDOW = 128
SWA_QW = SWA_HEADS * SWA_DH
SWA_KW = SWA_KV_HEADS * SWA_DH
SWA_Q_PRESCALE = SWA_DH ** -0.5 * math.log2(math.e)

LANES = 128
HEAD_PAD = 128
TM = 256
FFN_TILES = (576, 512, 256, 128)
VMEM_LIMIT = 56 * 1024 * 1024

EVEN_COLS = MLA_Q_RANK + MLA_KV_RANK + LANES + 2 * GDN_QK + GDN_VW + GDN_VW + LANES
ODD_COLS = 2 * GLA_QK + GLA_VW + GLA_VW + SWA_QW + 2 * SWA_KW + LANES


def _dot(a, b):
    return jnp.dot(a.astype(BF16), b.astype(BF16), preferred_element_type=F32)


def _dot_nt(a, b):
    return lax.dot_general(a.astype(BF16), b.astype(BF16), (((1,), (1,)), ((), ())), preferred_element_type=F32)


def _dot_tn(a, b):
    return lax.dot_general(a.astype(BF16), b.astype(BF16), (((0,), (0,)), ((), ())), preferred_element_type=F32)


def _split2(x):
    hi = x.astype(BF16)
    lo = (x - hi.astype(F32)).astype(BF16)
    return hi, lo


def _split3(x):
    hi = x.astype(BF16)
    r = x - hi.astype(F32)
    mid = r.astype(BF16)
    lo = (r - mid.astype(F32)).astype(BF16)
    return hi, mid, lo


def _dot3(a, b):
    ah, al = _split2(a)
    bh, bl = _split2(b)
    d = functools.partial(jnp.dot, preferred_element_type=F32)
    return d(ah, bh) + (d(ah, bl) + d(al, bh))


def _dot_sel(sel, x):
    s = sel.astype(BF16)
    hi, mid, lo = _split3(x)
    d = functools.partial(jnp.dot, preferred_element_type=F32)
    return d(s, hi) + (d(s, mid) + d(s, lo))


def _dot_sel_r(x, sel):
    s = sel.astype(BF16)
    hi, mid, lo = _split3(x)
    d = functools.partial(jnp.dot, preferred_element_type=F32)
    return d(hi, s) + (d(mid, s) + d(lo, s))


def _seg_mean_sq(x, seg, n_real):
    w = x.shape[-1]
    r = lax.broadcasted_iota(jnp.int32, (w, w), 0) // seg
    c = lax.broadcasted_iota(jnp.int32, (w, w), 1) // seg
    ones_bd = (r == c).astype(BF16)
    hi, lo = _split2(x * x)
    d = functools.partial(jnp.dot, preferred_element_type=F32)
    return (d(hi, ones_bd) + d(lo, ones_bd)) * (1.0 / n_real)


def _rms_rows(x, g):
    ms = jnp.mean(x * x, axis=-1, keepdims=True)
    return x * lax.rsqrt(ms + EPS) * g


def _sigmoid(x):
    return 1.0 / (1.0 + jnp.exp(-x))


def _silu(x):
    return x * _sigmoid(x)


def _softplus(x):
    return jnp.maximum(x, 0.0) + jnp.log1p(jnp.exp(-jnp.abs(x)))


def _rope(x, c, s1, s2, half):
    return x * c + pltpu.roll(x, LANES - half, 1) * s1 + pltpu.roll(x, half, 1) * s2


def _chunk_tri(n, reverse):
    i = lax.broadcasted_iota(jnp.int32, (n, n), 0)
    j = lax.broadcasted_iota(jnp.int32, (n, n), 1)
    same = (i // CHUNK) == (j // CHUNK)
    tri = (j >= i) if reverse else (j <= i)
    return (same & tri).astype(F32)


def _ada_kernel(c_ref, w_ref, b_ref, o_ref):
    sc = _silu(c_ref[...])
    o_ref[...] = _dot(sc, w_ref[...]) + b_ref[...]


def _ada(cvec, ada_w, ada_b):
    depth, d, nm = ada_w.shape
    tn = 1024
    return pl.pallas_call(
        _ada_kernel,
        grid=(depth, nm // tn),
        in_specs=[
            pl.BlockSpec(cvec.shape, lambda i, j: (0, 0)),
            pl.BlockSpec((None, d, tn), lambda i, j: (i, 0, j)),
            pl.BlockSpec((None, 1, tn), lambda i, j: (i, 0, j)),
        ],
        out_specs=pl.BlockSpec((None, cvec.shape[0], tn), lambda i, j: (i, 0, j)),
        out_shape=jax.ShapeDtypeStruct((depth, cvec.shape[0], nm), F32),
        compiler_params=pltpu.CompilerParams(vmem_limit_bytes=VMEM_LIMIT),
        name="ada_mod",
    )(cvec, ada_w, ada_b.reshape(depth, 1, nm))


def _const_spec(shape):
    nd = len(shape)
    return pl.BlockSpec(shape, lambda *_: (0,) * nd, pipeline_mode=pl.Buffered(1))


def _tok_spec(width, skip=0):
    return pl.BlockSpec((None, TM, width), lambda b, t: (b, t + skip, 0))


def _mod_spec(ctx_tiles, skip=0):
    return pl.BlockSpec((None, None, N_MOD, D_MODEL),
                        lambda b, t: (b, jnp.where(t + skip >= ctx_tiles, 1, 0), 0, 0))


def _params():
    return pltpu.CompilerParams(dimension_semantics=("parallel", "arbitrary"), vmem_limit_bytes=VMEM_LIMIT)


def _ffn_kernel(*refs, r0, n_ctx, split):
    if split:
        c_ref, x_ref, mod_ref, g_ref, wg_ref, wu_ref, wd_ref, o_ref = refs
    else:
        x_ref, mod_ref, g_ref, wg_ref, wu_ref, wd_ref, o_ref = refs
    x = x_ref[...]
    m = mod_ref[...]
    tm = x.shape[0]
    if n_ctx:
        is_ctx = pl.program_id(1) * tm + lax.broadcasted_iota(jnp.int32, (tm, 1), 0) < n_ctx
        row = lambda r: jnp.where(is_ctx, m[0, r:r + 1], m[1, r:r + 1])
        if split:
            x = jnp.where(is_ctx, c_ref[...], x)
    else:
        row = lambda r: m[1, r:r + 1]
    h = _rms_rows(x, g_ref[...]) * (1.0 + row(r0 + 1)) + row(r0)
    hb = h.astype(BF16)
    a = jnp.dot(hb, wg_ref[...], preferred_element_type=F32)
    u = jnp.dot(hb, wu_ref[...], preferred_element_type=F32)
    act = (_silu(a) * u).astype(BF16)
    y = jnp.dot(act, wd_ref[...], preferred_element_type=F32)
    o_ref[...] = x + (FFN_RES * row(r0 + 2)) * y


def _ffn(xs, mod, g, wg, wu, wd, *, layer, half, r0, n_ctx, ctx=None):
    b, rows, d = xs.shape
    split = ctx is not None
    if split:
        tm = TM
        ctx_tiles = n_ctx // tm
        rows += n_ctx
        toks = [pl.BlockSpec((None, tm, d), lambda b_, t: (b_, jnp.minimum(t, ctx_tiles - 1), 0)),
                pl.BlockSpec((None, tm, d), lambda b_, t: (b_, jnp.maximum(t - ctx_tiles, 0), 0))]
        data = [ctx, xs]
    else:
        tm = next(t for t in FFN_TILES if rows % t == 0)
        toks = [pl.BlockSpec((None, tm, d), lambda b_, t: (b_, t, 0))]
        data = [xs]
    wspec = lambda w: pl.BlockSpec((None, None) + w.shape[2:], lambda b_, t: (layer, half, 0, 0),
                                   pipeline_mode=pl.Buffered(1))
    return pl.pallas_call(
        functools.partial(_ffn_kernel, r0=r0, n_ctx=n_ctx, split=split),
        grid=(b, rows // tm),
        in_specs=toks + [pl.BlockSpec((None, 2, N_MOD, d), lambda b_, t: (b_, 0, 0, 0)), _const_spec((1, d)),
                         wspec(wg), wspec(wu), wspec(wd)],
        out_specs=pl.BlockSpec((None, tm, d), lambda b_, t: (b_, t, 0)),
        out_shape=jax.ShapeDtypeStruct((b, rows, d), F32),
        compiler_params=_params(),
        name="ffn_half_step",
    )(*data, mod, g.reshape(1, d), wg, wu, wd)


def _inproj_even_kernel(x_ref, mod_ref, g_ref, win_ref, qan_ref, wq_ref, kvan_ref, wkk_ref, wkv_ref,
                        qn_ref, kn_ref, rc_ref, rs1_ref, rs2_ref,
                        q_ref, k_ref, v_ref, zg_ref, gate_ref, small_ref):
    x = x_ref[...]
    m = mod_ref[...]
    h = _rms_rows(x, g_ref[...]) * (1.0 + m[4:5]) + m[3:4]
    z = jnp.dot(h.astype(BF16), win_ref[...], preferred_element_type=F32)
    o = 0
    cq = z[:, o:o + MLA_Q_RANK]; o += MLA_Q_RANK
    ckv = z[:, o:o + MLA_KV_RANK]; o += MLA_KV_RANK
    kr = z[:, o:o + LANES]; o += LANES
    zg_ref[...] = z[:, o:o + 3 * GDN_QK]; o += 3 * GDN_QK
    gate_ref[...] = z[:, o:o + GDN_VW]; o += GDN_VW
    small_ref[...] = z[:, o:o + LANES]

    rc, rs1, rs2 = rc_ref[...], rs1_ref[...], rs2_ref[...]
    qn, kn = qn_ref[...], kn_ref[...]
    qf = jnp.dot(_rms_rows(cq, qan_ref[...]).astype(BF16), wq_ref[...], preferred_element_type=F32)
    ckvn = _rms_rows(ckv, kvan_ref[...]).astype(BF16)
    kf = jnp.dot(ckvn, wkk_ref[...], preferred_element_type=F32)
    lane = lax.broadcasted_iota(jnp.int32, (1, MLA_HEADS * HEAD_PAD), 1)
    ones_cols = (lane % HEAD_PAD >= MLA_V).astype(F32)
    v_ref[...] = (jnp.dot(ckvn, wkv_ref[...], preferred_element_type=F32) + ones_cols).astype(v_ref.dtype)
    slabs = [slice(hd * HEAD_PAD, (hd + 1) * HEAD_PAD) for hd in range(MLA_HEADS)]
    xs = [qf[:, sl] for sl in slabs] + [kf[:, sl] + kr for sl in slabs]
    gains = [qn * MLA_Q_PRESCALE] * MLA_HEADS + [kn] * MLA_HEADS
    ss = [jnp.sum(x * x, axis=-1, keepdims=True) for x in xs]
    xs = [x * lax.rsqrt(s * (1.0 / MLA_QK) + EPS) * g for x, s, g in zip(xs, ss, gains)]
    xs = [_rope(x, rc, rs1, rs2, MLA_ROPE // 2) for x in xs]
    for hd, sl in enumerate(slabs):
        q_ref[:, sl] = xs[hd].astype(q_ref.dtype)
        k_ref[:, sl] = xs[MLA_HEADS + hd].astype(k_ref.dtype)


def _inproj_even(xs, mod, g, p, rope):
    b, rows, d = xs.shape
    ctx_tiles = p["ctx_tiles"]
    consts = [g.reshape(1, d), p["w_in"], p["q_a_norm"], p["w_q"], p["kv_a_norm"], p["w_kk"], p["w_kv"],
              p["q_norm"], p["k_norm"]]
    rope_spec = pl.BlockSpec((TM, LANES), lambda b_, t: (t, 0))
    widths = [(MLA_HEADS * HEAD_PAD, BF16), (MLA_HEADS * HEAD_PAD, BF16), (MLA_HEADS * HEAD_PAD, BF16),
              (3 * GDN_QK, F32), (GDN_VW, F32), (LANES, F32)]
    return pl.pallas_call(
        _inproj_even_kernel,
        grid=(b, rows // TM),
        in_specs=[_tok_spec(d), _mod_spec(ctx_tiles)] + [_const_spec(c.shape) for c in consts] + [rope_spec] * 3,
        out_specs=[_tok_spec(w) for w, _ in widths],
        out_shape=[jax.ShapeDtypeStruct((b, rows, w), dt) for w, dt in widths],
        compiler_params=_params(),
        name="inproj_even",
    )(xs, mod, *consts, *rope)


def _mla_kernel(q_ref, k_ref, v_ref, o_ref, *, n_ctx, n_all, ctx_tiles):
    t = pl.program_id(2)
    lo = lax.broadcasted_iota(jnp.int32, (TM, LANES), 1) < MLA_V
    n_heads = q_ref.shape[-1] // HEAD_PAD

    def attend(nk):
        q = q_ref[...]
        sls = [slice(hh * HEAD_PAD, (hh + 1) * HEAD_PAD) for hh in range(n_heads)]
        s = [lax.dot_general(q[:, sl], k_ref[0:nk, sl], (((1,), (1,)), ((), ())), preferred_element_type=F32)
             for sl in sls]
        mx = [jnp.max(x, axis=-1, keepdims=True) for x in s]
        p = [jnp.exp2(x - m).astype(BF16) for x, m in zip(s, mx)]
        r = [jnp.dot(x, v_ref[0:nk, sl], preferred_element_type=F32) for x, sl in zip(p, sls)]
        outs = [x / x[:, MLA_V:MLA_V + 1] for x in r]
        for pr in range(n_heads // 2):
            o_ref[:, pr * LANES:(pr + 1) * LANES] = jnp.where(
                lo, outs[2 * pr], pltpu.roll(outs[2 * pr + 1], MLA_V, 1)).astype(o_ref.dtype)

    @pl.when(t < ctx_tiles)
    def _():
        attend(n_ctx)

    @pl.when(t >= ctx_tiles)
    def _():
        attend(n_all)


def _mla_attention(q, k, v, *, n_ctx):
    b, rows, _ = q.shape
    hs = MLA_STEP_HEADS
    return pl.pallas_call(
        functools.partial(_mla_kernel, n_ctx=n_ctx, n_all=rows, ctx_tiles=n_ctx // TM),
        grid=(b, MLA_HEADS // hs, rows // TM),
        in_specs=[
            pl.BlockSpec((None, TM, hs * HEAD_PAD), lambda b_, h, t: (b_, t, h)),
            pl.BlockSpec((None, rows, hs * HEAD_PAD), lambda b_, h, t: (b_, 0, h)),
            pl.BlockSpec((None, rows, hs * HEAD_PAD), lambda b_, h, t: (b_, 0, h)),
        ],
        out_specs=pl.BlockSpec((None, TM, hs * MLA_V), lambda b_, h, t: (b_, t, h)),
        out_shape=jax.ShapeDtypeStruct((b, rows, MLA_VW), BF16),
        compiler_params=pltpu.CompilerParams(dimension_semantics=("parallel", "parallel", "arbitrary"),
                                             vmem_limit_bytes=VMEM_LIMIT),
        name="mla_attention",
    )(q, k, v)


def _gdn_prep_kernel(z_ref, zp_ref, zn_ref, cw_ref, sm_ref, nega_ref, dtb_ref,
                     q_ref, k_ref, v_ref, gf_ref, gb_ref, *, ctx_tiles, n_tiles):
    t = pl.program_id(1)
    first = (t == 0) | (t == ctx_tiles)
    last = (t == ctx_tiles - 1) | (t == n_tiles - 1)
    z = z_ref[...]
    half = GDN_CONV // 2
    zp = jnp.where(first, 0.0, zp_ref[...])
    zn = jnp.where(last, 0.0, zn_ref[...])
    row8 = lax.broadcasted_iota(jnp.int32, (8, 1), 0)

    def shifted(k):
        r = pltpu.roll(z, k % TM, 0)
        if k > 0:
            fix = jnp.where(row8 < k, pltpu.roll(zp, k, 0), r[0:8])
            return jnp.concatenate([fix, r[8:]], axis=0)
        fix = jnp.where(row8 >= 8 + k, pltpu.roll(zn, k % 8, 0), r[TM - 8:])
        return jnp.concatenate([r[:TM - 8], fix], axis=0)

    cw = cw_ref[...]
    acc = z * cw[half:half + 1]
    for j in range(GDN_CONV):
        if j != half:
            acc = acc + shifted(half - j) * cw[j:j + 1]
    qkv = _silu(acc)
    q = qkv[:, :GDN_QK]
    k = qkv[:, GDN_QK:2 * GDN_QK]
    q_ref[...] = q * lax.rsqrt(_seg_mean_sq(q, GDN_DK, 1.0) + EPS) * (GDN_DK ** -0.5)
    k_ref[...] = k * lax.rsqrt(_seg_mean_sq(k, GDN_DK, 1.0) + EPS)
    v_ref[...] = qkv[:, 2 * GDN_QK:]

    sm = sm_ref[...]
    lane = lax.broadcasted_iota(jnp.int32, sm.shape, 1)
    g = nega_ref[...] * _softplus(sm + dtb_ref[...])
    g = jnp.where(lane < 2 * GDN_HEADS, g, 0.0)
    gc_f = _dot_sel(_chunk_tri(TM, False), g)
    gc_b = _dot_sel(_chunk_tri(TM, True), g)
    gates = jnp.where(lane < GDN_HEADS, gc_f, jnp.where(lane < 2 * GDN_HEADS, gc_b, _sigmoid(sm)))
    row = lax.broadcasted_iota(jnp.int32, (LANES, GDN_QK), 0)
    head = lax.broadcasted_iota(jnp.int32, (LANES, GDN_QK), 1) // GDN_DK
    for d, ref in enumerate((gf_ref, gb_ref)):
        ref[:, 0:GDN_QK] = _dot_sel_r(gates, (row == head + d * GDN_HEADS).astype(F32))
        ref[:, GDN_QK:2 * GDN_QK] = _dot_sel_r(gates, (row == head + (2 + d) * GDN_HEADS).astype(F32))


def _gdn_prep(zg, small, conv_w, nega, dtb, *, ctx_tiles):
    b, rows, w = zg.shape
    n_tiles = rows // TM
    hb = TM // 8
    n_hblk = rows // 8
    return pl.pallas_call(
        functools.partial(_gdn_prep_kernel, ctx_tiles=ctx_tiles, n_tiles=n_tiles),
        grid=(b, n_tiles),
        in_specs=[
            _tok_spec(w),
            pl.BlockSpec((None, 8, w), lambda b_, t: (b_, jnp.maximum(t * hb - 1, 0), 0)),
            pl.BlockSpec((None, 8, w), lambda b_, t: (b_, jnp.minimum((t + 1) * hb, n_hblk - 1), 0)),
            _const_spec(conv_w.shape), _tok_spec(LANES), _const_spec((1, LANES)), _const_spec((1, LANES)),
        ],
        out_specs=[_tok_spec(GDN_QK), _tok_spec(GDN_QK), _tok_spec(GDN_VW), _tok_spec(2 * GDN_QK),
                   _tok_spec(2 * GDN_QK)],
        out_shape=[jax.ShapeDtypeStruct((b, rows, GDN_QK), F32), jax.ShapeDtypeStruct((b, rows, GDN_QK), F32),
                   jax.ShapeDtypeStruct((b, rows, GDN_VW), F32), jax.ShapeDtypeStruct((b, rows, 2 * GDN_QK), F32),
                   jax.ShapeDtypeStruct((b, rows, 2 * GDN_QK), F32)],
        compiler_params=_params(),
        name="gdn_prep",
    )(zg, zg, zg, conv_w, small, nega, dtb)


def _bd_halves(y, bdm):
    yb = y.astype(BF16)
    out = []
    for s in range(2):
        t = jnp.concatenate([yb[:, s * GDN_HALF:(s + 1) * GDN_HALF]] * (GDN_HALF // GDN_DK), axis=0)
        out.append(jnp.where(bdm, t, jnp.zeros_like(t)))
    return out


def _hprod(x, bd):
    xb = x.astype(BF16)
    d = functools.partial(jnp.dot, preferred_element_type=F32)
    return jnp.concatenate([d(xb[:, 0:GDN_HALF], bd[0]), d(xb[:, GDN_HALF:], bd[1])], axis=1)


def _hprod3(x, y, bdm):
    xh, xl = _split2(x)
    yh, yl = _split2(y)
    bh, bl = _bd_halves(yh, bdm), _bd_halves(yl, bdm)
    return _hprod(xh, bh) + (_hprod(xh, bl) + _hprod(xl, bh))


def _gdn_chunks(chains):
    ii = lax.broadcasted_iota(jnp.int32, (CHUNK, GDN_QK), 0)
    jj = lax.broadcasted_iota(jnp.int32, (CHUNK, GDN_QK), 1) % CHUNK
    r = lax.broadcasted_iota(jnp.int32, (GDN_HALF, GDN_HALF), 0) // GDN_DK
    c = lax.broadcasted_iota(jnp.int32, (GDN_HALF, GDN_HALF), 1) // GDN_DK
    bdm = r == c
    diag = ii == jj
    eye = diag.astype(F32)
    n = len(chains)
    qs, ks, vs, gs, srefs, revs = (list(t) for t in zip(*chains))
    each = lambda f, *ls: [f(*a) for a in zip(*ls)]
    bd = lambda ys: [_bd_halves(y, bdm) for y in ys]

    incl = [(ii <= jj) if rv else (ii >= jj) for rv in revs]
    strict = [(ii < jj) if rv else (ii > jj) for rv in revs]
    last = [0 if rv else CHUNK - 1 for rv in revs]
    gc = [g[:, 0:GDN_QK] for g in gs]
    beta = [g[:, GDN_QK:2 * GDN_QK] for g in gs]
    gc_row = [jnp.sum(jnp.where(diag, x, 0.0), axis=0, keepdims=True) for x in gc]
    decay = each(lambda x, xr, ic: jnp.exp(jnp.where(ic, x - xr, -jnp.inf)), gc, gc_row, incl)

    def kt_bd(k):
        kt = k.T.astype(BF16)
        out = []
        for s in range(2):
            t = jnp.concatenate([kt[s * GDN_HALF:(s + 1) * GDN_HALF]] * (GDN_HALF // CHUNK), axis=1)
            out.append(jnp.where(bdm, t, jnp.zeros_like(t)))
        return out

    kkqk = each(lambda k, q, rk: _hprod(jnp.concatenate([k, q], axis=0), rk), ks, qs, [kt_bd(k) for k in ks])
    a = each(lambda st, b, x, dc: jnp.where(st, b * x[0:CHUNK] * dc, 0.0), strict, beta, kkqk, decay)
    qk = each(lambda x, dc: x[CHUNK:2 * CHUNK] * dc, kkqk, decay)

    def level_mask(blk, rv):
        same = (ii // (2 * blk)) == (jj // (2 * blk))
        off = (ii // blk) < (jj // blk) if rv else (ii // blk) > (jj // blk)
        return same & off

    m = each(lambda x, rv: eye - jnp.where(level_mask(1, rv), x, 0.0), a, revs)
    blk = 2
    while blk < CHUNK:
        am = each(lambda x, rv: jnp.where(level_mask(blk, rv), x, 0.0), a, revs)
        x = each(_hprod, am, bd(m))
        m = each(lambda mm, y, ybd: mm - _hprod(mm, ybd), m, x, bd(x))
        blk *= 2
    am = each(lambda x, mm: _hprod3(x, mm, bdm), a, m)
    resid = each(lambda mm, y: eye - mm - y, m, am)
    m = each(lambda mm, rbd: mm + _hprod(mm, rbd), m, bd(resid))

    egc = [jnp.exp(x) for x in gc]
    u = each(_hprod, m, bd(each(lambda v, b: v * b, vs, beta)))
    w = each(_hprod, m, bd(each(lambda k, b, e: k * (b * e), ks, beta, egc)))
    g_last = each(lambda x, l: x[l:l + 1, :], gc, last)
    s0 = [sr[0] for sr in srefs]
    s1 = [sr[1] for sr in srefs]

    def sprod(x, a0, a1):
        return jnp.concatenate([_dot(x[:, 0:GDN_HALF], a0), _dot(x[:, GDN_HALF:], a1)], axis=1)

    ws = each(lambda ww, q, e, a0, a1: sprod(jnp.concatenate([ww, q * e], axis=0), a0, a1), w, qs, egc, s0, s1)
    v_new = each(lambda uu, x: uu - x[0:CHUNK], u, ws)
    o = each(lambda x, y, vbd: x[CHUNK:2 * CHUNK] + _hprod(y, vbd), ws, qk, bd(v_new))
    k_dec = each(lambda k, gl, x: k * jnp.exp(gl - x), ks, g_last, gc)
    e_last = [jnp.exp(gl) for gl in g_last]
    for i in range(n):
        for s, st in enumerate((s0[i], s1[i])):
            sl = slice(s * GDN_HALF, (s + 1) * GDN_HALF)
            srefs[i][s] = st * e_last[i][:, sl] + jnp.where(bdm, _dot_tn(k_dec[i][:, sl], v_new[i][:, sl]), 0.0)
    return o


def _gdn_scan_kernel(qf_ref, kf_ref, vf_ref, gf_ref, qb_ref, kb_ref, vb_ref, gb_ref, of_ref, ob_ref, sf_ref, sb_ref):
    @pl.when(pl.program_id(1) == 0)
    def _():
        sf_ref[...] = jnp.zeros_like(sf_ref)
        sb_ref[...] = jnp.zeros_like(sb_ref)

    nb = qf_ref.shape[0]
    chains = []
    for i in range(nb):
        chains.append((qf_ref[i], kf_ref[i], vf_ref[i], gf_ref[i], sf_ref.at[i], False))
        chains.append((qb_ref[i], kb_ref[i], vb_ref[i], gb_ref[i], sb_ref.at[i], True))
    outs = _gdn_chunks(chains)
    for i in range(nb):
        of_ref[i] = outs[2 * i]
        ob_ref[i] = outs[2 * i + 1]


def _scan_chunk_maps(ctx_chunks, n_chunks):
    fwd = lambda s: s
    bwd = lambda s: jnp.where(s < ctx_chunks, ctx_chunks - 1 - s, n_chunks - 1 - (s - ctx_chunks))
    return fwd, bwd


def _gdn_scan(q, k, v, g_f, g_b, *, n_ctx):
    b, rows, _ = q.shape
    n_chunks = rows // CHUNK
    fwd, bwd = _scan_chunk_maps(n_ctx // CHUNK, n_chunks)

    nb = GDN_SCAN_SAMPLES if b % GDN_SCAN_SAMPLES == 0 else 1

    def specs(cm):
        tok = lambda w: pl.BlockSpec((nb, CHUNK, w), lambda b_, s: (b_, cm(s), 0))
        return [tok(GDN_QK), tok(GDN_QK), tok(GDN_VW), tok(2 * GDN_QK)]

    return pl.pallas_call(
        _gdn_scan_kernel,
        grid=(b // nb, n_chunks),
        in_specs=specs(fwd) + specs(bwd),
        out_specs=[pl.BlockSpec((nb, CHUNK, GDN_VW), lambda b_, s: (b_, fwd(s), 0)),
                   pl.BlockSpec((nb, CHUNK, GDN_VW), lambda b_, s: (b_, bwd(s), 0))],
        out_shape=[jax.ShapeDtypeStruct((b, rows, GDN_VW), F32)] * 2,
        scratch_shapes=[pltpu.VMEM((nb, 2, GDN_HALF, GDN_HALF), F32)] * 2,
        compiler_params=_params(),
        name="gdn_scan",
    )(q, k, v, g_f, q, k, v, g_b)


def _outproj_ffn_kernel(x_ref, mod_ref, oa_ref, of_ref, ob_ref, gate_ref, gn_ref, w_ref, g_ref, wg_ref, wu_ref, wd_ref,
                        o_ref, *, seg, rec_first):
    m = mod_ref[...]
    o = of_ref[...] + ob_ref[...]
    y = o * lax.rsqrt(_seg_mean_sq(o, seg, float(seg)) + EPS) * gn_ref[...] * _silu(gate_ref[...])
    wr = y.shape[-1]
    wa = oa_ref.shape[-1]
    if rec_first:
        out = jnp.dot(y.astype(BF16), w_ref[0:wr, :], preferred_element_type=F32)
        out = out + jnp.dot(oa_ref[...], w_ref[wr:wr + wa, :], preferred_element_type=F32)
    else:
        out = jnp.dot(oa_ref[...], w_ref[0:wa, :], preferred_element_type=F32)
        out = out + jnp.dot(y.astype(BF16), w_ref[wa:wa + wr, :], preferred_element_type=F32)
    x1 = x_ref[...] + m[5:6] * out
    hb = (_rms_rows(x1, g_ref[...]) * (1.0 + m[7:8]) + m[6:7]).astype(BF16)
    a = jnp.dot(hb, wg_ref[...], preferred_element_type=F32)
    u = jnp.dot(hb, wu_ref[...], preferred_element_type=F32)
    act = (_silu(a) * u).astype(BF16)
    o_ref[...] = x1 + (FFN_RES * m[8:9]) * jnp.dot(act, wd_ref[...], preferred_element_type=F32)


def _outproj_ffn(xs, mod, o_att, o_f, o_b, gate, gn, w_out, g, wg, wu, wd, *, layer, seg, rec_first, ctx_tiles,
                 skip=0, att_skip=0):
    b, rows, d = xs.shape
    out_rows = rows - skip * TM
    wspec = lambda w: pl.BlockSpec((None, None) + w.shape[2:], lambda b_, t: (layer, 1, 0, 0),
                                   pipeline_mode=pl.Buffered(1))
    return pl.pallas_call(
        functools.partial(_outproj_ffn_kernel, seg=seg, rec_first=rec_first),
        grid=(b, out_rows // TM),
        in_specs=[_tok_spec(d, skip), _mod_spec(ctx_tiles, skip), _tok_spec(o_att.shape[-1], att_skip),
                  _tok_spec(o_f.shape[-1], skip), _tok_spec(o_b.shape[-1], skip), _tok_spec(gate.shape[-1], skip),
                  _const_spec(gn.shape), _const_spec(w_out.shape), _const_spec((1, d)),
                  wspec(wg), wspec(wu), wspec(wd)],
        out_specs=_tok_spec(d),
        out_shape=jax.ShapeDtypeStruct((b, out_rows, d), F32),
        compiler_params=_params(),
        name="outproj_ffn",
    )(xs, mod, o_att, o_f, o_b, gate, gn, w_out, g.reshape(1, d), wg, wu, wd)


def _inproj_odd_kernel(x_ref, mod_ref, g_ref, win_ref, w2_ref, b2_ref, qn_ref, kn_ref, rc_ref, rs1_ref, rs2_ref,
                       gq_ref, gk_ref, gv_ref, rg_ref, bc_ref, sq_ref, sk_ref, sv_ref):
    x = x_ref[...]
    m = mod_ref[...]
    h = _rms_rows(x, g_ref[...]) * (1.0 + m[4:5]) + m[3:4]
    z = jnp.dot(h.astype(BF16), win_ref[...], preferred_element_type=F32)
    o = 0
    gq_ref[...] = z[:, o:o + GLA_QK]; o += GLA_QK
    gk_ref[...] = z[:, o:o + GLA_QK]; o += GLA_QK
    gv_ref[...] = z[:, o:o + GLA_VW]; o += GLA_VW
    rg_ref[...] = z[:, o:o + GLA_VW]; o += GLA_VW
    sq = z[:, o:o + SWA_QW]; o += SWA_QW
    sk = z[:, o:o + SWA_KW]; o += SWA_KW
    sv_ref[...] = z[:, o:o + SWA_KW].astype(sv_ref.dtype); o += SWA_KW
    lowrank = z[:, o:o + LANES]

    logit = _dot(lowrank, w2_ref[...]) + b2_ref[...]
    log_a = (jnp.minimum(logit, 0.0) - jnp.log1p(jnp.exp(-jnp.abs(logit)))) * (1.0 / GLA_TAU)
    bc_ref[:, 0:GLA_QK] = _dot_sel(_chunk_tri(TM, False), log_a[:, 0:GLA_QK])
    bc_ref[:, GLA_QK:2 * GLA_QK] = _dot_sel(_chunk_tri(TM, True), log_a[:, GLA_QK:2 * GLA_QK])

    rc, rs1, rs2 = rc_ref[...], rs1_ref[...], rs2_ref[...]
    sqn = sq * lax.rsqrt(_seg_mean_sq(sq, SWA_DH, float(SWA_DH)) + EPS) * (qn_ref[...] * SWA_Q_PRESCALE)
    for s in range(SWA_QW // LANES):
        sl = slice(s * LANES, (s + 1) * LANES)
        sq_ref[:, sl] = _rope(sqn[:, sl], rc, rs1, rs2, SWA_DH // 2).astype(sq_ref.dtype)
    skn = sk * lax.rsqrt(_seg_mean_sq(sk, SWA_DH, float(SWA_DH)) + EPS) * kn_ref[...]
    sk_ref[...] = _rope(skn, rc, rs1, rs2, SWA_DH // 2).astype(sk_ref.dtype)


def _inproj_odd(xs, mod, g, p, rope):
    b, rows, d = xs.shape
    consts = [g.reshape(1, d), p["w_in"], p["w2"], p["b2"], p["q_norm"], p["k_norm"]]
    rope_spec = pl.BlockSpec((TM, LANES), lambda b_, t: (t, 0))
    widths = [(GLA_QK, F32), (GLA_QK, F32), (GLA_VW, F32), (GLA_VW, F32), (2 * GLA_QK, F32),
              (SWA_QW, BF16), (SWA_KW, BF16), (SWA_KW, BF16)]
    return pl.pallas_call(
        _inproj_odd_kernel,
        grid=(b, rows // TM),
        in_specs=[_tok_spec(d), _mod_spec(p["ctx_tiles"])] + [_const_spec(c.shape) for c in consts] + [rope_spec] * 3,
        out_specs=[_tok_spec(w) for w, _ in widths],
        out_shape=[jax.ShapeDtypeStruct((b, rows, w), dt) for w, dt in widths],
        compiler_params=_params(),
        name="inproj_odd",
    )(xs, mod, *consts, *rope)


def _gla_chunk(q, k, v, bc, st_ref, reverse):
    def blk(shape, rdiv, cdiv):
        return (lax.broadcasted_iota(jnp.int32, shape, 0) // rdiv) == (lax.broadcasted_iota(jnp.int32, shape, 1) // cdiv)

    ii = lax.broadcasted_iota(jnp.int32, (CHUNK, GLA_QK), 0)
    jj = lax.broadcasted_iota(jnp.int32, (CHUNK, GLA_QK), 1) % CHUNK
    incl = (ii <= jj) if reverse else (ii >= jj)
    last = 0 if reverse else CHUNK - 1
    q_dec = (q * (GLA_DK ** -0.5)) * jnp.exp(bc)
    k_inv = k * jnp.exp(-bc)
    b_last = bc[last:last + 1, :]
    k_dec = k * jnp.exp(b_last - bc)
    e_last = jnp.exp(b_last)
    kt = jnp.concatenate([k_inv.T.astype(BF16)] * GLA_HEADS, axis=1)
    rk = jnp.where(blk((GLA_QK, GLA_QK), GLA_DK, CHUNK), kt, jnp.zeros_like(kt))
    attn = jnp.where(incl, jnp.dot(q_dec.astype(BF16), rk, preferred_element_type=F32), 0.0)
    vt = jnp.concatenate([v.astype(BF16)] * GLA_HEADS, axis=0)
    vbd = jnp.where(blk((GLA_QK, GLA_VW), CHUNK, GLA_DV), vt, jnp.zeros_like(vt))
    st = st_ref[...]
    o = jnp.dot(attn.astype(BF16), vbd, preferred_element_type=F32) + _dot_nt(q_dec, st)
    st_ref[...] = st * e_last + jnp.where(blk((GLA_VW, GLA_QK), GLA_DV, GLA_DK), _dot_tn(v, k_dec), 0.0)
    return o


def _gla_scan_kernel(qf_ref, kf_ref, vf_ref, bf_ref, qb_ref, kb_ref, vb_ref, bb_ref, of_ref, ob_ref, sf_ref, sb_ref):
    @pl.when(pl.program_id(1) == 0)
    def _():
        sf_ref[...] = jnp.zeros_like(sf_ref)
        sb_ref[...] = jnp.zeros_like(sb_ref)

    for i in range(qf_ref.shape[0]):
        of_ref[i] = _gla_chunk(qf_ref[i], kf_ref[i], vf_ref[i], bf_ref[i][:, 0:GLA_QK], sf_ref.at[i], False)
        ob_ref[i] = _gla_chunk(qb_ref[i], kb_ref[i], vb_ref[i], bb_ref[i][:, GLA_QK:2 * GLA_QK], sb_ref.at[i], True)


def _gla_scan(q, k, v, bc, *, n_ctx):
    b, rows, _ = q.shape
    n_chunks = rows // CHUNK
    fwd, bwd = _scan_chunk_maps(n_ctx // CHUNK, n_chunks)
    nb = GLA_SCAN_SAMPLES if b % GLA_SCAN_SAMPLES == 0 else 1

    def specs(cm):
        tok = lambda w: pl.BlockSpec((nb, CHUNK, w), lambda b_, s: (b_, cm(s), 0))
        return [tok(GLA_QK), tok(GLA_QK), tok(GLA_VW), tok(2 * GLA_QK)]

    return pl.pallas_call(
        _gla_scan_kernel,
        grid=(b // nb, n_chunks),
        in_specs=specs(fwd) + specs(bwd),
        out_specs=[pl.BlockSpec((nb, CHUNK, GLA_VW), lambda b_, s: (b_, fwd(s), 0)),
                   pl.BlockSpec((nb, CHUNK, GLA_VW), lambda b_, s: (b_, bwd(s), 0))],
        out_shape=[jax.ShapeDtypeStruct((b, rows, GLA_VW), F32)] * 2,
        scratch_shapes=[pltpu.VMEM((nb, GLA_VW, GLA_QK), F32)] * 2,
        compiler_params=_params(),
        name="gla_scan",
    )(q, k, v, bc, q, k, v, bc)


def _swa_kernel(q_ref, k_ref, v_ref, sink_ref, o_ref, *, n_ctx, n_lat):
    w = SWA_WINDOW
    n = pl.program_id(1)
    nb = n_lat // w

    def rows(ref, blk):
        return ref[pl.ds(pl.multiple_of(n_ctx + blk * w, w), w), :]

    pb = jnp.maximum(n - 1, 0)
    xb = jnp.minimum(n + 1, nb - 1)
    k_cat = jnp.concatenate([rows(k_ref, pb), rows(k_ref, n), rows(k_ref, xb), k_ref[0:n_ctx, :]], axis=0)
    v_cat = jnp.concatenate([rows(v_ref, pb), rows(v_ref, n), rows(v_ref, xb), v_ref[0:n_ctx, :]], axis=0)
    v_one = jnp.concatenate([v_cat, jnp.ones_like(v_cat)], axis=1)
    half = SWA_HEADS // 2
    ii = lax.broadcasted_iota(jnp.int32, (half * w, w), 0) % w
    jj = lax.broadcasted_iota(jnp.int32, (half * w, w), 1)
    ninf = -jnp.inf
    bias_prev = jnp.where((jj >= ii) & (n > 0), 0.0, ninf)
    bias_next = jnp.where((jj <= ii) & (n < nb - 1), 0.0, ninf)
    lo = lax.broadcasted_iota(jnp.int32, (half * w, LANES), 1) < SWA_DH
    q = q_ref[...]
    qst = jnp.concatenate([q[:, j * LANES:(j + 1) * LANES] for j in range(half)], axis=0)
    sink = sink_ref[...]
    dims = (((1,), (1,)), ((), ()))
    qm = [jnp.where(lo if g == 0 else jnp.logical_not(lo), qst, jnp.zeros_like(qst)) for g in range(SWA_KV_HEADS)]
    s = [lax.dot_general(x, k_cat, dims, preferred_element_type=F32) for x in qm]
    parts = [[x[:, 0:w] + bias_prev, x[:, w:2 * w], x[:, 2 * w:3 * w] + bias_next, x[:, 3 * w:]] for x in s]
    sk = [jnp.concatenate([jnp.broadcast_to(sink[0:1, half * g + j:half * g + j + 1], (w, 1)) for j in range(half)],
                          axis=0) for g in range(SWA_KV_HEADS)]
    def row_max(ps, k_):
        slabs = [x[:, c * LANES:(c + 1) * LANES] for x in ps for c in range(x.shape[1] // LANES)]
        return jnp.maximum(jnp.max(functools.reduce(jnp.maximum, slabs), axis=-1, keepdims=True), k_)

    mx = [row_max(ps, k_) for ps, k_ in zip(parts, sk)]
    p = [jnp.concatenate([jnp.exp2(x - m).astype(BF16) for x in ps], axis=1) for ps, m in zip(parts, mx)]
    r = [jnp.dot(x, v_one, preferred_element_type=F32) for x in p]
    res = [x[:, 0:LANES] / (x[:, LANES:2 * LANES] + jnp.exp2(k_ - m)) for x, k_, m in zip(r, sk, mx)]
    o = jnp.where(lo, res[0], res[1]).astype(o_ref.dtype)
    for j in range(half):
        o_ref[:, j * LANES:(j + 1) * LANES] = o[j * w:(j + 1) * w, :]


def _swa_attention(q, k, v, sink, *, n_ctx):
    b, rows, _ = q.shape
    n_lat = rows - n_ctx
    w = SWA_WINDOW
    skip = n_ctx // w
    return pl.pallas_call(
        functools.partial(_swa_kernel, n_ctx=n_ctx, n_lat=n_lat),
        grid=(b, n_lat // w),
        in_specs=[
            pl.BlockSpec((None, w, SWA_QW), lambda b_, n: (b_, n + skip, 0)),
            pl.BlockSpec((None, rows, SWA_KW), lambda b_, n: (b_, 0, 0)),
            pl.BlockSpec((None, rows, SWA_KW), lambda b_, n: (b_, 0, 0)),
            _const_spec((1, LANES)),
        ],
        out_specs=pl.BlockSpec((None, w, SWA_QW), lambda b_, n: (b_, n, 0)),
        out_shape=jax.ShapeDtypeStruct((b, n_lat, SWA_QW), BF16),
        compiler_params=_params(),
        name="swa_attention",
    )(q, k, v, sink)


def _rope_tables(n_lat, n_ctx, rot_dim):
    t = jnp.arange(n_lat)
    row = (t // GRID_W).astype(F32)
    col = (t % GRID_W).astype(F32)
    n_freq = rot_dim // 4
    inv = ROPE_THETA ** (-jnp.arange(n_freq, dtype=F32) / n_freq)
    ang = jnp.concatenate([row[:, None] * inv, col[:, None] * inv], axis=-1)
    half = rot_dim // 2
    cos = jnp.concatenate([jnp.ones((n_ctx, half), F32), jnp.cos(ang)], axis=0)
    sin = jnp.concatenate([jnp.zeros((n_ctx, half), F32), jnp.sin(ang)], axis=0)
    rows = n_ctx + n_lat
    one = lambda w: jnp.ones((rows, w), F32)
    zero = lambda w: jnp.zeros((rows, w), F32)
    if rot_dim == MLA_ROPE:
        c = jnp.concatenate([one(MLA_NOPE), cos, cos, one(HEAD_PAD - MLA_QK)], axis=1)
        s1 = jnp.concatenate([zero(MLA_NOPE), -sin, zero(half), zero(HEAD_PAD - MLA_QK)], axis=1)
        s2 = jnp.concatenate([zero(MLA_NOPE), zero(half), sin, zero(HEAD_PAD - MLA_QK)], axis=1)
    else:
        c = jnp.concatenate([cos, cos, cos, cos], axis=1)
        s1 = jnp.concatenate([-sin, zero(half), -sin, zero(half)], axis=1)
        s2 = jnp.concatenate([zero(half), sin, zero(half), sin], axis=1)
    return c, s1, s2


def _even_params(j, n_ctx, ev_w_in, ev_q_a_norm, ev_w_q_up, ev_kv_a_norm, ev_w_kv_up, ev_mla_q_norm, ev_mla_k_norm,
                 ev_gdn_conv, ev_gdn_a_log, ev_gdn_dt_bias, ev_gdn_out_norm, ev_w_out):
    w = ev_w_in[j]
    d = w.shape[0]
    z = lambda n: jnp.zeros((d, n), F32)
    o_kr = MLA_Q_RANK + MLA_KV_RANK
    o_g = o_kr + MLA_ROPE
    o_small = o_g + 3 * GDN_QK
    o_gate = o_small + 4 * GDN_HEADS
    w_in = jnp.concatenate([
        w[:, :o_kr], z(MLA_NOPE), w[:, o_kr:o_g], z(HEAD_PAD - MLA_QK),
        w[:, o_g:o_small], w[:, o_gate:o_gate + GDN_VW],
        w[:, o_small:o_gate], z(LANES - 4 * GDN_HEADS)], axis=1).astype(BF16)
    pad_h = HEAD_PAD - MLA_QK
    w_q = jnp.pad(ev_w_q_up[j].reshape(MLA_Q_RANK, MLA_HEADS, MLA_QK), ((0, 0), (0, 0), (0, pad_h)))
    wkv = ev_w_kv_up[j].reshape(MLA_KV_RANK, MLA_HEADS, MLA_NOPE + MLA_V)
    w_kk = jnp.pad(wkv[:, :, :MLA_NOPE], ((0, 0), (0, 0), (0, HEAD_PAD - MLA_NOPE)))
    lane_row = lambda vec: jnp.pad(vec, (0, LANES - vec.shape[0])).reshape(1, LANES)
    fb = lambda a: jnp.concatenate([a[0], a[1]])
    return {
        "ctx_tiles": n_ctx // TM,
        "w_in": w_in,
        "q_a_norm": ev_q_a_norm[j].reshape(1, -1),
        "w_q": w_q.reshape(MLA_Q_RANK, MLA_HEADS * HEAD_PAD).astype(BF16),
        "kv_a_norm": ev_kv_a_norm[j].reshape(1, -1),
        "w_kk": w_kk.reshape(MLA_KV_RANK, MLA_HEADS * HEAD_PAD).astype(BF16),
        "w_kv": jnp.pad(wkv[:, :, MLA_NOPE:], ((0, 0), (0, 0), (0, HEAD_PAD - MLA_V))
                        ).reshape(MLA_KV_RANK, MLA_HEADS * HEAD_PAD).astype(BF16),
        "q_norm": lane_row(ev_mla_q_norm[j]),
        "k_norm": lane_row(ev_mla_k_norm[j]),
        "conv_w": ev_gdn_conv[j],
        "neg_a": lane_row(-jnp.exp(fb(ev_gdn_a_log[j]))),
        "dt_bias": lane_row(fb(ev_gdn_dt_bias[j])),
        "out_norm": jnp.tile(ev_gdn_out_norm[j], GDN_HEADS).reshape(1, GDN_VW),
        "w_out": ev_w_out[j].astype(BF16),
    }


def _swa_head_perm():
    half = SWA_HEADS // 2
    heads = [h for j in range(half) for h in (j, half + j)]
    return jnp.concatenate([jnp.arange(SWA_DH) + h * SWA_DH for h in heads])


def _odd_params(j, n_ctx, od_w_in, od_gla_gate_w2, od_gla_gate_b, od_gla_out_norm, od_swa_q_norm, od_swa_k_norm,
                od_swa_sink, od_w_out):
    w = od_w_in[j]
    d = w.shape[0]
    o_gate = 2 * GLA_QK + GLA_VW
    o_rg = o_gate + 2 * GLA_RANK
    o_sq = o_rg + GLA_VW
    o_sk = o_sq + SWA_QW
    perm = _swa_head_perm()
    w_in = jnp.concatenate([
        w[:, :o_gate], w[:, o_rg:o_sq], w[:, o_sq:o_sk][:, perm], w[:, o_sk:],
        w[:, o_gate:o_rg], jnp.zeros((d, LANES - 2 * GLA_RANK), F32)], axis=1).astype(BF16)
    w2 = jnp.zeros((LANES, 2 * GLA_QK), F32)
    w2 = w2.at[0:GLA_RANK, 0:GLA_QK].set(od_gla_gate_w2[j, 0])
    w2 = w2.at[GLA_RANK:2 * GLA_RANK, GLA_QK:].set(od_gla_gate_w2[j, 1])
    wo = od_w_out[j]
    w_out = jnp.concatenate([wo[:GLA_VW], wo[GLA_VW:][perm]], axis=0).astype(BF16)
    return {
        "ctx_tiles": n_ctx // TM,
        "w_in": w_in,
        "w2": w2.astype(BF16),
        "b2": jnp.concatenate([od_gla_gate_b[j, 0], od_gla_gate_b[j, 1]]).reshape(1, 2 * GLA_QK),
        "q_norm": jnp.tile(od_swa_q_norm[j], SWA_HEADS).reshape(1, SWA_QW),
        "k_norm": jnp.tile(od_swa_k_norm[j], SWA_KV_HEADS).reshape(1, SWA_KW),
        "sink": jnp.pad(od_swa_sink[j] * math.log2(math.e), (0, LANES - SWA_HEADS)).reshape(1, LANES),
        "out_norm": jnp.tile(od_gla_out_norm[j], GLA_HEADS).reshape(1, GLA_VW),
        "w_out": w_out,
    }


def kernel(x, c, ctx, c_ctx, ada_w, ada_b, norm_g, ffn_w_gate, ffn_w_up, ffn_w_down, ev_w_in, ev_q_a_norm, ev_w_q_up, ev_kv_a_norm, ev_w_kv_up, ev_mla_q_norm, ev_mla_k_norm, ev_gdn_conv, ev_gdn_a_log, ev_gdn_dt_bias, ev_gdn_out_norm, ev_w_out, od_w_in, od_gla_gate_w2, od_gla_gate_b, od_gla_out_norm, od_swa_q_norm, od_swa_k_norm, od_swa_sink, od_w_out):
    b, n_lat, d = x.shape
    n_ctx = ctx.shape[1]
    depth = ada_w.shape[0]
    assert d == D_MODEL and n_ctx % TM == 0 and n_lat % TM == 0 and n_lat % GRID_W == 0
    assert depth % 2 == 0, "the last layer must be an odd (GLA/SWA) layer: context outputs of that mixer are not built"
    ctx_tiles = n_ctx // TM

    cvec = jnp.concatenate([c, c_ctx[None, :], jnp.zeros((16 - b - 1, d), F32)], axis=0)
    mod_all = _ada(cvec, ada_w, ada_b).reshape(depth, 16, N_MOD, d)
    rope_mla = _rope_tables(n_lat, n_ctx, MLA_ROPE)
    rope_swa = _rope_tables(n_lat, n_ctx, SWA_DH)

    wg, wu, wd = (t.astype(BF16) for t in (ffn_w_gate, ffn_w_up, ffn_w_down))
    xs = x
    for i in range(depth):
        last = i == depth - 1
        mod = jnp.stack([jnp.broadcast_to(mod_all[i, b][None], (b, N_MOD, d)), mod_all[i, :b]], axis=1)
        xs = _ffn(xs, mod, norm_g[i, 0], wg, wu, wd, layer=i, half=0, r0=0, n_ctx=n_ctx, ctx=ctx if i == 0 else None)
        j = i // 2
        skip = ctx_tiles if last else 0
        if i % 2 == 0:
            p = _even_params(j, n_ctx, ev_w_in, ev_q_a_norm, ev_w_q_up, ev_kv_a_norm, ev_w_kv_up, ev_mla_q_norm,
                             ev_mla_k_norm, ev_gdn_conv, ev_gdn_a_log, ev_gdn_dt_bias, ev_gdn_out_norm, ev_w_out)
            q, k, v, zg, gate, small = _inproj_even(xs, mod, norm_g[i, 1], p, rope_mla)
            o_att = _mla_attention(q, k, v, n_ctx=n_ctx)
            gq, gk, gv, g_f, g_b = _gdn_prep(zg, small, p["conv_w"], p["neg_a"], p["dt_bias"], ctx_tiles=ctx_tiles)
            o_f, o_b = _gdn_scan(gq, gk, gv, g_f, g_b, n_ctx=n_ctx)
            mix = dict(o_att=o_att, o_f=o_f, o_b=o_b, gate=gate, seg=GDN_DV, rec_first=False, att_skip=skip)
        else:
            assert last, "odd layers that must also produce context outputs are not built"
            p = _odd_params(j, n_ctx, od_w_in, od_gla_gate_w2, od_gla_gate_b, od_gla_out_norm, od_swa_q_norm,
                            od_swa_k_norm, od_swa_sink, od_w_out)
            gq, gk, gv, rg, bc, sq, sk, sv = _inproj_odd(xs, mod, norm_g[i, 1], p, rope_swa)
            o_f, o_b = _gla_scan(gq, gk, gv, bc, n_ctx=n_ctx)
            o_att = _swa_attention(sq, sk, sv, p["sink"], n_ctx=n_ctx)
            mix = dict(o_att=o_att, o_f=o_f, o_b=o_b, gate=rg, seg=GLA_DV, rec_first=True, att_skip=0)
        xs = _outproj_ffn(xs, mod, mix["o_att"], mix["o_f"], mix["o_b"], mix["gate"], p["out_norm"], p["w_out"],
                          norm_g[i, 2], wg, wu, wd, layer=i, seg=mix["seg"], rec_first=mix["rec_first"],
                          ctx_tiles=ctx_tiles, skip=skip, att_skip=mix["att_skip"])
    return xs
```

```python
import functools
import math

import jax
import jax.numpy as jnp
from jax import lax
from jax.experimental import pallas as pl
from jax.experimental.pallas import tpu as pltpu

F32 = jnp.float32
BF16 = jnp.bfloat16

D_MODEL = 1024
GRID_W = 64
D_FF = 2816
FFN_RES = 0.5
N_MOD = 9
EPS = 1e-6
ROPE_THETA = 10000.0
CHUNK = 64

MLA_HEADS = 8
MLA_NOPE = 64
MLA_ROPE = 32
MLA_QK = MLA_NOPE + MLA_ROPE
MLA_V = 64
MLA_Q_RANK = 384
MLA_KV_RANK = 256
MLA_VW = MLA_HEADS * MLA_V
MLA_STEP_HEADS = 4
MLA_Q_PRESCALE =MLA_QK ** -0.5 * math.log2(math.e)

GDN_HEADS = 8
GDN_DK = 64
GDN_DV = 64
GDN_CONV = 5
GDN_QK = GDN_HEADS * GDN_DK
GDN_VW = GDN_HEADS * GDN_DV
GDN_HALF = 128
GDN_TILES = GDN_QK // GDN_HALF
GDN_SCAN_SAMPLES = 8

GLA_HEADS = 4
GLA_DK = 64
GLA_DV = 128
GLA_RANK = 16
GLA_TAU = 16.0
GLA_QK = GLA_HEADS * GLA_DK
GLA_VW = GLA_HEADS * GLA_DV
GLA_SCAN_SAMPLES = 8

SWA_HEADS = 8
SWA_KV_HEADS = 2
SWA_DH = 64
SWA_WINDOW = 128
SWA_QW = SWA_HEADS * SWA_DH
SWA_KW = SWA_KV_HEADS * SWA_DH
SWA_Q_PRESCALE = SWA_DH ** -0.5 * math.log2(math.e)

LANES = 128
HEAD_PAD = 128
TM = 256
FFN_TILES = (576, 512, 256, 128)
VMEM_LIMIT = 56 * 1024 * 1024

EVEN_COLS = MLA_Q_RANK + MLA_KV_RANK + LANES + 2 * GDN_QK + GDN_VW + GDN_VW + LANES
ODD_COLS = 2 * GLA_QK + GLA_VW + GLA_VW + SWA_QW + 2 * SWA_KW + LANES


def _dot(a, b):
    return jnp.dot(a.astype(BF16), b.astype(BF16), preferred_element_type=F32)


def _dot_nt(a, b):
    return lax.dot_general(a.astype(BF16), b.astype(BF16), (((1,), (1,)), ((), ())), preferred_element_type=F32)


def _dot_tn(a, b):
    return lax.dot_general(a.astype(BF16), b.astype(BF16), (((0,), (0,)), ((), ())), preferred_element_type=F32)


def _split2(x):
    hi = x.astype(BF16)
    lo = (x - hi.astype(F32)).astype(BF16)
    return hi, lo


def _split3(x):
    hi = x.astype(BF16)
    r = x - hi.astype(F32)
    mid = r.astype(BF16)
    lo = (r - mid.astype(F32)).astype(BF16)
    return hi, mid, lo


def _dot3(a, b):
    ah, al = _split2(a)
    bh, bl = _split2(b)
    d = functools.partial(jnp.dot, preferred_element_type=F32)
    return d(ah, bh) + (d(ah, bl) + d(al, bh))


def _dot_sel(sel, x):
    s = sel.astype(BF16)
    hi, mid, lo = _split3(x)
    d = functools.partial(jnp.dot, preferred_element_type=F32)
    return d(s, hi) + (d(s, mid) + d(s, lo))


def _dot_sel_r(x, sel):
    s = sel.astype(BF16)
    hi, mid, lo = _split3(x)
    d = functools.partial(jnp.dot, preferred_element_type=F32)
    return d(hi, s) + (d(mid, s) + d(lo, s))


def _seg_mean_sq(x, seg, n_real):
    w = x.shape[-1]
    r = lax.broadcasted_iota(jnp.int32, (w, w), 0) // seg
    c = lax.broadcasted_iota(jnp.int32, (w, w), 1) // seg
    ones_bd = (r == c).astype(BF16)
    hi, lo = _split2(x * x)
    d = functools.partial(jnp.dot, preferred_element_type=F32)
    return (d(hi, ones_bd) + d(lo, ones_bd)) * (1.0 / n_real)


def _rms_rows(x, g):
    ms = jnp.mean(x * x, axis=-1, keepdims=True)
    return x * lax.rsqrt(ms + EPS) * g


def _sigmoid(x):
    return 1.0 / (1.0 + jnp.exp(-x))


def _silu(x):
    return x * _sigmoid(x)


def _softplus(x):
    return jnp.maximum(x, 0.0) + jnp.log1p(jnp.exp(-jnp.abs(x)))


def _rope(x, c, s1, s2, half):
    return x * c + pltpu.roll(x, LANES - half, 1) * s1 + pltpu.roll(x, half, 1) * s2


def _chunk_tri(n, reverse):
    i = lax.broadcasted_iota(jnp.int32, (n, n), 0)
    j = lax.broadcasted_iota(jnp.int32, (n, n), 1)
    same = (i // CHUNK) == (j // CHUNK)
    tri = (j >= i) if reverse else (j <= i)
    return (same & tri).astype(F32)


def _ada_kernel(c_ref, w_ref, b_ref, o_ref):
    sc = _silu(c_ref[...])
    o_ref[...] = _dot(sc, w_ref[...]) + b_ref[...]


def _ada(cvec, ada_w, ada_b):
    depth, d, nm = ada_w.shape
    tn = 1024
    return pl.pallas_call(
        _ada_kernel,
        grid=(depth, nm // tn),
        in_specs=[
            pl.BlockSpec(cvec.shape, lambda i, j: (0, 0)),
            pl.BlockSpec((None, d, tn), lambda i, j: (i, 0, j)),
            pl.BlockSpec((None, 1, tn), lambda i, j: (i, 0, j)),
        ],
        out_specs=pl.BlockSpec((None, cvec.shape[0], tn), lambda i, j: (i, 0, j)),
        out_shape=jax.ShapeDtypeStruct((depth, cvec.shape[0], nm), F32),
        compiler_params=pltpu.CompilerParams(vmem_limit_bytes=VMEM_LIMIT),
        name="ada_mod",
    )(cvec, ada_w, ada_b.reshape(depth, 1, nm))


def _const_spec(shape):
    nd = len(shape)
    return pl.BlockSpec(shape, lambda *_: (0,) * nd, pipeline_mode=pl.Buffered(1))


def _tok_spec(width, skip=0):
    return pl.BlockSpec((None, TM, width), lambda b, t: (b, t + skip, 0))


def _mod_spec(ctx_tiles, skip=0):
    return pl.BlockSpec((None, None, N_MOD, D_MODEL),
                        lambda b, t: (b, jnp.where(t + skip >= ctx_tiles, 1, 0), 0, 0))


def _params():
    return pltpu.CompilerParams(dimension_semantics=("parallel", "arbitrary"), vmem_limit_bytes=VMEM_LIMIT)


def _ffn_kernel(*refs, r0, n_ctx, split):
    if split:
        c_ref, x_ref, mod_ref, g_ref, wg_ref, wu_ref, wd_ref, o_ref = refs
    else:
        x_ref, mod_ref, g_ref, wg_ref, wu_ref, wd_ref, o_ref = refs
    x = x_ref[...]
    m = mod_ref[...]
    tm = x.shape[0]
    if n_ctx:
        is_ctx = pl.program_id(1) * tm + lax.broadcasted_iota(jnp.int32, (tm, 1), 0) < n_ctx
        row = lambda r: jnp.where(is_ctx, m[0, r:r + 1], m[1, r:r + 1])
        if split:
            x = jnp.where(is_ctx, c_ref[...], x)
    else:
        row = lambda r: m[1, r:r + 1]
    h = _rms_rows(x, g_ref[...]) * (1.0 + row(r0 + 1)) + row(r0)
    hb = h.astype(BF16)
    a = jnp.dot(hb, wg_ref[...], preferred_element_type=F32)
    u = jnp.dot(hb, wu_ref[...], preferred_element_type=F32)
    act = (_silu(a) * u).astype(BF16)
    y = jnp.dot(act, wd_ref[...], preferred_element_type=F32)
    o_ref[...] = x + (FFN_RES * row(r0 + 2)) * y


def _ffn(xs, mod, g, wg, wu, wd, *, layer, half, r0, n_ctx, ctx=None):
    b, rows, d = xs.shape
    split = ctx is not None
    if split:
        tm = TM
        ctx_tiles = n_ctx // tm
        rows += n_ctx
        toks = [pl.BlockSpec((None, tm, d), lambda b_, t: (b_, jnp.minimum(t, ctx_tiles - 1), 0)),
                pl.BlockSpec((None, tm, d), lambda b_, t: (b_, jnp.maximum(t - ctx_tiles, 0), 0))]
        data = [ctx, xs]
    else:
        tm = next(t for t in FFN_TILES if rows % t == 0)
        toks = [pl.BlockSpec((None, tm, d), lambda b_, t: (b_, t, 0))]
        data = [xs]
    wspec = lambda w: pl.BlockSpec((None, None) + w.shape[2:], lambda b_, t: (layer, half, 0, 0),
                                   pipeline_mode=pl.Buffered(1))
    return pl.pallas_call(
        functools.partial(_ffn_kernel, r0=r0, n_ctx=n_ctx, split=split),
        grid=(b, rows // tm),
        in_specs=toks + [pl.BlockSpec((None, 2, N_MOD, d), lambda b_, t: (b_, 0, 0, 0)), _const_spec((1, d)),
                         wspec(wg), wspec(wu), wspec(wd)],
        out_specs=pl.BlockSpec((None, tm, d), lambda b_, t: (b_, t, 0)),
        out_shape=jax.ShapeDtypeStruct((b, rows, d), F32),
        compiler_params=_params(),
        name="ffn_half_step",
    )(*data, mod, g.reshape(1, d), wg, wu, wd)


def _inproj_even_kernel(x_ref, mod_ref, g_ref, win_ref, qan_ref, wq_ref, kvan_ref, wkk_ref, wkv_ref,
                        qn_ref, kn_ref, rc_ref, rs1_ref, rs2_ref,
                        q_ref, k_ref, v_ref, zg_ref, gate_ref, small_ref):
    x = x_ref[...]
    m = mod_ref[...]
    h = _rms_rows(x, g_ref[...]) * (1.0 + m[4:5]) + m[3:4]
    z = jnp.dot(h.astype(BF16), win_ref[...], preferred_element_type=F32)
    o = 0
    cq = z[:, o:o + MLA_Q_RANK]; o += MLA_Q_RANK
    ckv = z[:, o:o + MLA_KV_RANK]; o += MLA_KV_RANK
    kr = z[:, o:o + LANES]; o += LANES
    zg_ref[...] = z[:, o:o + 3 * GDN_QK]; o += 3 * GDN_QK
    gate_ref[...] = z[:, o:o + GDN_VW]; o += GDN_VW
    small_ref[...] = z[:, o:o + LANES]

    rc, rs1, rs2 = rc_ref[...], rs1_ref[...], rs2_ref[...]
    qn, kn = qn_ref[...], kn_ref[...]
    qf = jnp.dot(_rms_rows(cq, qan_ref[...]).astype(BF16), wq_ref[...], preferred_element_type=F32)
    ckvn = _rms_rows(ckv, kvan_ref[...]).astype(BF16)
    kf = jnp.dot(ckvn, wkk_ref[...], preferred_element_type=F32)
    lane = lax.broadcasted_iota(jnp.int32, (1, MLA_HEADS * HEAD_PAD), 1)
    ones_cols = (lane % HEAD_PAD >= MLA_V).astype(F32)
    v_ref[...] = (jnp.dot(ckvn, wkv_ref[...], preferred_element_type=F32) + ones_cols).astype(v_ref.dtype)
    slabs = [slice(hd * HEAD_PAD, (hd + 1) * HEAD_PAD) for hd in range(MLA_HEADS)]
    xs = [qf[:, sl] for sl in slabs] + [kf[:, sl] + kr for sl in slabs]
    gains = [qn * MLA_Q_PRESCALE] * MLA_HEADS + [kn] * MLA_HEADS
    ss = [jnp.sum(x * x, axis=-1, keepdims=True) for x in xs]
    xs = [x * lax.rsqrt(s * (1.0 / MLA_QK) + EPS) * g for x, s, g in zip(xs, ss, gains)]
    xs = [_rope(x, rc, rs1, rs2, MLA_ROPE // 2) for x in xs]
    for hd, sl in enumerate(slabs):
        q_ref[:, sl] = xs[hd].astype(q_ref.dtype)
        k_ref[:, sl] = xs[MLA_HEADS + hd].astype(k_ref.dtype)


def _inproj_even(xs, mod, g, p, rope):
    b, rows, d = xs.shape
    ctx_tiles = p["ctx_tiles"]
    consts = [g.reshape(1, d), p["w_in"], p["q_a_norm"], p["w_q"], p["kv_a_norm"], p["w_kk"], p["w_kv"],
              p["q_norm"], p["k_norm"]]
    rope_spec = pl.BlockSpec((TM, LANES), lambda b_, t: (t, 0))
    widths = [(MLA_HEADS * HEAD_PAD, BF16), (MLA_HEADS * HEAD_PAD, BF16), (MLA_HEADS * HEAD_PAD, BF16),
              (3 * GDN_QK, F32), (GDN_VW, F32), (LANES, F32)]
    return pl.pallas_call(
        _inproj_even_kernel,
        grid=(b, rows // TM),
        in_specs=[_tok_spec(d), _mod_spec(ctx_tiles)] + [_const_spec(c.shape) for c in consts] + [rope_spec] * 3,
        out_specs=[_tok_spec(w) for w, _ in widths],
        out_shape=[jax.ShapeDtypeStruct((b, rows, w), dt) for w, dt in widths],
        compiler_params=_params(),
        name="inproj_even",
    )(xs, mod, *consts, *rope)


def _mla_kernel(q_ref, k_ref, v_ref, o_ref, *, n_ctx, n_all, ctx_tiles):
    t = pl.program_id(2)
    lo = lax.broadcasted_iota(jnp.int32, (TM, LANES), 1) < MLA_V
    n_heads = q_ref.shape[-1] // HEAD_PAD

    def attend(nk):
        q = q_ref[...]
        sls = [slice(hh * HEAD_PAD, (hh + 1) * HEAD_PAD) for hh in range(n_heads)]
        s = [lax.dot_general(q[:, sl], k_ref[0:nk, sl], (((1,), (1,)), ((), ())), preferred_element_type=F32)
             for sl in sls]
        mx = [jnp.max(x, axis=-1, keepdims=True) for x in s]
        p = [jnp.exp2(x - m).astype(BF16) for x, m in zip(s, mx)]
        r = [jnp.dot(x, v_ref[0:nk, sl], preferred_element_type=F32) for x, sl in zip(p, sls)]
        outs = [x / x[:, MLA_V:MLA_V + 1] for x in r]
        for pr in range(n_heads // 2):
            o_ref[:, pr * LANES:(pr + 1) * LANES] = jnp.where(
                lo, outs[2 * pr], pltpu.roll(outs[2 * pr + 1], MLA_V, 1)).astype(o_ref.dtype)

    @pl.when(t < ctx_tiles)
    def _():
        attend(n_ctx)

    @pl.when(t >= ctx_tiles)
    def _():
        attend(n_all)


def _mla_attention(q, k, v, *, n_ctx):
    b, rows, _ = q.shape
    hs = MLA_STEP_HEADS
    return pl.pallas_call(
        functools.partial(_mla_kernel, n_ctx=n_ctx, n_all=rows, ctx_tiles=n_ctx // TM),
        grid=(b, MLA_HEADS // hs, rows // TM),
        in_specs=[
            pl.BlockSpec((None, TM, hs * HEAD_PAD), lambda b_, h, t: (b_, t, h)),
            pl.BlockSpec((None, rows, hs * HEAD_PAD), lambda b_, h, t: (b_, 0, h)),
            pl.BlockSpec((None, rows, hs * HEAD_PAD), lambda b_, h, t: (b_, 0, h)),
        ],
        out_specs=pl.BlockSpec((None, TM, hs * MLA_V), lambda b_, h, t: (b_, t, h)),
        out_shape=jax.ShapeDtypeStruct((b, rows, MLA_VW), BF16),
        compiler_params=pltpu.CompilerParams(dimension_semantics=("parallel", "parallel", "arbitrary"),
                                             vmem_limit_bytes=VMEM_LIMIT),
        name="mla_attention",
    )(q, k, v)


def _gdn_prep_kernel(z_ref, zp_ref, zn_ref, cw_ref, sm_ref, nega_ref, dtb_ref,
                     q_ref, k_ref, v_ref, g_ref, ext_ref, *, ctx_tiles, n_tiles):
    t = pl.program_id(1)
    first = (t == 0) | (t == ctx_tiles)
    last = (t == ctx_tiles - 1) | (t == n_tiles - 1)
    half = GDN_CONV // 2
    ext_ref[0:8, :] = jnp.where(first, 0.0, zp_ref[...])
    ext_ref[8:8 + TM, :] = z_ref[...]
    ext_ref[8 + TM:16 + TM, :] = jnp.where(last, 0.0, zn_ref[...])
    cw = cw_ref[...]
    acc = z_ref[...] * cw[half:half + 1]
    for j in range(GDN_CONV):
        if j != half:
            acc = acc + ext_ref[8 - half + j:8 - half + j + TM, :] * cw[j:j + 1]
    qkv = _silu(acc)
    q = qkv[:, :GDN_QK]
    k = qkv[:, GDN_QK:2 * GDN_QK]
    q_ref[...] = q * lax.rsqrt(_seg_mean_sq(q, GDN_DK, 1.0) + EPS) * (GDN_DK ** -0.5)
    k_ref[...] = k * lax.rsqrt(_seg_mean_sq(k, GDN_DK, 1.0) + EPS)
    v_ref[...] = qkv[:, 2 * GDN_QK:]

    sm = sm_ref[...]
    lane = lax.broadcasted_iota(jnp.int32, sm.shape, 1)
    g = nega_ref[...] * _softplus(sm + dtb_ref[...])
    g = jnp.where(lane < 2 * GDN_HEADS, g, 0.0)
    gc_f = _dot_sel(_chunk_tri(TM, False), g)
    gc_b = _dot_sel(_chunk_tri(TM, True), g)
    g_ref[...] = jnp.where(lane < GDN_HEADS, gc_f, jnp.where(lane < 2 * GDN_HEADS, gc_b, _sigmoid(sm)))


def _gdn_prep(zg, small, conv_w, nega, dtb, *, ctx_tiles):
    b, rows, w = zg.shape
    n_tiles = rows // TM
    hb = TM // 8
    n_hblk = rows // 8
    return pl.pallas_call(
        functools.partial(_gdn_prep_kernel, ctx_tiles=ctx_tiles, n_tiles=n_tiles),
        grid=(b, n_tiles),
        in_specs=[
            _tok_spec(w),
            pl.BlockSpec((None, 8, w), lambda b_, t: (b_, jnp.maximum(t * hb - 1, 0), 0)),
            pl.BlockSpec((None, 8, w), lambda b_, t: (b_, jnp.minimum((t + 1) * hb, n_hblk - 1), 0)),
            _const_spec(conv_w.shape), _tok_spec(LANES), _const_spec((1, LANES)), _const_spec((1, LANES)),
        ],
        out_specs=[_tok_spec(GDN_QK), _tok_spec(GDN_QK), _tok_spec(GDN_VW), _tok_spec(LANES)],
        out_shape=[jax.ShapeDtypeStruct((b, rows, GDN_QK), F32), jax.ShapeDtypeStruct((b, rows, GDN_QK), F32),
                   jax.ShapeDtypeStruct((b, rows, GDN_VW), F32), jax.ShapeDtypeStruct((b, rows, LANES), F32)],
        scratch_shapes=[pltpu.VMEM((TM + 16, w), F32)],
        compiler_params=_params(),
        name="gdn_prep",
    )(zg, zg, zg, conv_w, small, nega, dtb)


def _bd_halves(y, bdm):
    yb = y.astype(BF16)
    out = []
    for s in range(GDN_TILES):
        t = jnp.concatenate([yb[:, s * GDN_HALF:(s + 1) * GDN_HALF]] * (GDN_HALF // GDN_DK), axis=0)
        out.append(jnp.where(bdm, t, jnp.zeros_like(t)))
    return out


def _hprod(x, bd):
    xb = x.astype(BF16)
    d = functools.partial(jnp.dot, preferred_element_type=F32)
    return jnp.concatenate([d(xb[:, s * GDN_HALF:(s + 1) * GDN_HALF], bd[s]) for s in range(GDN_TILES)], axis=1)


def _hprod3(x, y, bdm):
    xh, xl = _split2(x)
    yh, yl = _split2(y)
    bh, bl = _bd_halves(yh, bdm), _bd_halves(yl, bdm)
    return _hprod(xh, bh) + (_hprod(xh, bl) + _hprod(xl, bh))


def _gdn_chunks(chains):
    ii = lax.broadcasted_iota(jnp.int32, (CHUNK, GDN_QK), 0)
    jj = lax.broadcasted_iota(jnp.int32, (CHUNK, GDN_QK), 1) % CHUNK
    r = lax.broadcasted_iota(jnp.int32, (GDN_HALF, GDN_HALF), 0) // GDN_DK
    c = lax.broadcasted_iota(jnp.int32, (GDN_HALF, GDN_HALF), 1) // GDN_DK
    bdm = r == c
    diag = ii == jj
    eye = diag.astype(F32)
    n = len(chains)
    qs, ks, vs, gs, srefs, revs = (list(t) for t in zip(*chains))
    each = lambda f, *ls: [f(*a) for a in zip(*ls)]
    bd = lambda ys: [_bd_halves(y, bdm) for y in ys]

    incl = [(ii <= jj) if rv else (ii >= jj) for rv in revs]
    strict = [(ii < jj) if rv else (ii > jj) for rv in revs]
    last = [0 if rv else CHUNK - 1 for rv in revs]
    gc = [g[:, 0:GDN_QK] for g in gs]
    beta = [g[:, GDN_QK:2 * GDN_QK] for g in gs]
    gc_row = [jnp.sum(jnp.where(diag, x, 0.0), axis=0, keepdims=True) for x in gc]
    decay = each(lambda x, xr, ic: jnp.exp(jnp.where(ic, x - xr, -jnp.inf)), gc, gc_row, incl)

    def kt_bd(k):
        kt = k.T.astype(BF16)
        out = []
        for s in range(GDN_TILES):
            t = jnp.concatenate([kt[s * GDN_HALF:(s + 1) * GDN_HALF]] * (GDN_HALF // CHUNK), axis=1)
            out.append(jnp.where(bdm, t, jnp.zeros_like(t)))
        return out

    kkqk = each(lambda k, q, rk: _hprod(jnp.concatenate([k, q], axis=0), rk), ks, qs, [kt_bd(k) for k in ks])
    a = each(lambda st, b, x, dc: jnp.where(st, b * x[0:CHUNK] * dc, 0.0), strict, beta, kkqk, decay)
    qk = each(lambda x, dc: x[CHUNK:2 * CHUNK] * dc, kkqk, decay)

    def level_mask(blk, rv):
        same = (ii // (2 * blk)) == (jj // (2 * blk))
        off = (ii // blk) < (jj // blk) if rv else (ii // blk) > (jj // blk)
        return same & off

    m = each(lambda x, rv: eye - jnp.where(level_mask(1, rv), x, 0.0), a, revs)
    blk = 2
    while blk < CHUNK:
        am = each(lambda x, rv: jnp.where(level_mask(blk, rv), x, 0.0), a, revs)
        x = each(_hprod, am, bd(m))
        m = each(lambda mm, y, ybd: mm - _hprod(mm, ybd), m, x, bd(x))
        blk *= 2
    am = each(lambda x, mm: _hprod3(x, mm, bdm), a, m)
    resid = each(lambda mm, y: eye - mm - y, m, am)
    m = each(lambda mm, rbd: mm + _hprod(mm, rbd), m, bd(resid))

    egc = [jnp.exp(x) for x in gc]
    u = each(_hprod, m, bd(each(lambda v, b: v * b, vs, beta)))
    w = each(_hprod, m, bd(each(lambda k, b, e: k * (b * e), ks, beta, egc)))
    g_last = each(lambda x, l: x[l:l + 1, :], gc, last)
    st = [[sr[s] for s in range(GDN_TILES)] for sr in srefs]

    def sprod(x, tiles):
        return jnp.concatenate([_dot(x[:, s * GDN_HALF:(s + 1) * GDN_HALF], tiles[s]) for s in range(GDN_TILES)],
                               axis=1)

    ws = each(lambda ww, q, e, tiles: sprod(jnp.concatenate([ww, q * e], axis=0), tiles), w, qs, egc, st)
    v_new = each(lambda uu, x: uu - x[0:CHUNK], u, ws)
    o = each(lambda x, y, vbd: x[CHUNK:2 * CHUNK] + _hprod(y, vbd), ws, qk, bd(v_new))
    k_dec = each(lambda k, gl, x: k * jnp.exp(gl - x), ks, g_last, gc)
    e_last = [jnp.exp(gl) for gl in g_last]
    for i in range(n):
        for s in range(GDN_TILES):
            sl = slice(s * GDN_HALF, (s + 1) * GDN_HALF)
            srefs[i][s] = st[i][s] * e_last[i][:, sl] + jnp.where(
                bdm, _dot_tn(k_dec[i][:, sl], v_new[i][:, sl]), 0.0)
    return o


def _gdn_scan_kernel(qf_ref, kf_ref, vf_ref, gf_ref, qb_ref, kb_ref, vb_ref, gb_ref, of_ref, ob_ref, sf_ref, sb_ref):
    @pl.when(pl.program_id(1) == 0)
    def _():
        sf_ref[...] = jnp.zeros_like(sf_ref)
        sb_ref[...] = jnp.zeros_like(sb_ref)

    nb = qf_ref.shape[0]
    row = lax.broadcasted_iota(jnp.int32, (LANES, 2 * GDN_QK), 0)
    col = lax.broadcasted_iota(jnp.int32, (LANES, 2 * GDN_QK), 1)
    src = (col % GDN_QK) // GDN_DK + jnp.where(col >= GDN_QK, 2 * GDN_HEADS, 0)

    def spread(ref, d):
        stack = jnp.concatenate([ref[i] for i in range(nb)], axis=0)
        return _dot_sel_r(stack, (row == src + d * GDN_HEADS).astype(F32))

    gf, gb = spread(gf_ref, 0), spread(gb_ref, 1)
    chains = []
    for i in range(nb):
        rows = slice(i * CHUNK, (i + 1) * CHUNK)
        chains.append((qf_ref[i], kf_ref[i], vf_ref[i], gf[rows], sf_ref.at[i], False))
        chains.append((qb_ref[i], kb_ref[i], vb_ref[i], gb[rows], sb_ref.at[i], True))
    outs = _gdn_chunks(chains)
    for i in range(nb):
        of_ref[i] = outs[2 * i]
        ob_ref[i] = outs[2 * i + 1]


def _scan_chunk_maps(ctx_chunks, n_chunks):
    fwd = lambda s: s
    bwd = lambda s: jnp.where(s < ctx_chunks, ctx_chunks - 1 - s, n_chunks - 1 - (s - ctx_chunks))
    return fwd, bwd


def _gdn_scan(q, k, v, gates, *, n_ctx):
    b, rows, _ = q.shape
    n_chunks = rows // CHUNK
    fwd, bwd = _scan_chunk_maps(n_ctx // CHUNK, n_chunks)

    nb = GDN_SCAN_SAMPLES if b % GDN_SCAN_SAMPLES == 0 else 1

    def specs(cm):
        tok = lambda w: pl.BlockSpec((nb, CHUNK, w), lambda b_, s: (b_, cm(s), 0))
        return [tok(GDN_QK), tok(GDN_QK), tok(GDN_VW), tok(LANES)]

    return pl.pallas_call(
        _gdn_scan_kernel,
        grid=(b // nb, n_chunks),
        in_specs=specs(fwd) + specs(bwd),
        out_specs=[pl.BlockSpec((nb, CHUNK, GDN_VW), lambda b_, s: (b_, fwd(s), 0)),
                   pl.BlockSpec((nb, CHUNK, GDN_VW), lambda b_, s: (b_, bwd(s), 0))],
        out_shape=[jax.ShapeDtypeStruct((b, rows, GDN_VW), F32)] * 2,
        scratch_shapes=[pltpu.VMEM((nb, GDN_TILES, GDN_HALF, GDN_HALF), F32)] * 2,
        compiler_params=_params(),
        name="gdn_scan",
    )(q, k, v, gates, q, k, v, gates)


def _outproj_ffn_kernel(x_ref, mod_ref, oa_ref, of_ref, ob_ref, gate_ref, gn_ref, w_ref, g_ref, wg_ref, wu_ref, wd_ref,
                        o_ref, *, seg, rec_first):
    m = mod_ref[...]
    o = of_ref[...] + ob_ref[...]
    y = o * lax.rsqrt(_seg_mean_sq(o, seg, float(seg)) + EPS) * gn_ref[...] * _silu(gate_ref[...])
    wr = y.shape[-1]
    wa = oa_ref.shape[-1]
    if rec_first:
        out = jnp.dot(y.astype(BF16), w_ref[0:wr, :], preferred_element_type=F32)
        out = out + jnp.dot(oa_ref[...], w_ref[wr:wr + wa, :], preferred_element_type=F32)
    else:
        out = jnp.dot(oa_ref[...], w_ref[0:wa, :], preferred_element_type=F32)
        out = out + jnp.dot(y.astype(BF16), w_ref[wa:wa + wr, :], preferred_element_type=F32)
    x1 = x_ref[...] + m[5:6] * out
    hb = (_rms_rows(x1, g_ref[...]) * (1.0 + m[7:8]) + m[6:7]).astype(BF16)
    a = jnp.dot(hb, wg_ref[...], preferred_element_type=F32)
    u = jnp.dot(hb, wu_ref[...], preferred_element_type=F32)
    act = (_silu(a) * u).astype(BF16)
    o_ref[...] = x1 + (FFN_RES * m[8:9]) * jnp.dot(act, wd_ref[...], preferred_element_type=F32)


def _outproj_ffn(xs, mod, o_att, o_f, o_b, gate, gn, w_out, g, wg, wu, wd, *, layer, seg, rec_first, ctx_tiles,
                 skip=0, att_skip=0):
    b, rows, d = xs.shape
    out_rows = rows - skip * TM
    wspec = lambda w: pl.BlockSpec((None, None) + w.shape[2:], lambda b_, t: (layer, 1, 0, 0),
                                   pipeline_mode=pl.Buffered(1))
    return pl.pallas_call(
        functools.partial(_outproj_ffn_kernel, seg=seg, rec_first=rec_first),
        grid=(b, out_rows // TM),
        in_specs=[_tok_spec(d, skip), _mod_spec(ctx_tiles, skip), _tok_spec(o_att.shape[-1], att_skip),
                  _tok_spec(o_f.shape[-1], skip), _tok_spec(o_b.shape[-1], skip), _tok_spec(gate.shape[-1], skip),
                  _const_spec(gn.shape), _const_spec(w_out.shape), _const_spec((1, d)),
                  wspec(wg), wspec(wu), wspec(wd)],
        out_specs=_tok_spec(d),
        out_shape=jax.ShapeDtypeStruct((b, out_rows, d), F32),
        compiler_params=_params(),
        name="outproj_ffn",
    )(xs, mod, o_att, o_f, o_b, gate, gn, w_out, g.reshape(1, d), wg, wu, wd)


def _inproj_odd_kernel(x_ref, mod_ref, g_ref, win_ref, w2_ref, b2_ref, qn_ref, kn_ref, rc_ref, rs1_ref, rs2_ref,
                       gq_ref, gk_ref, gv_ref, rg_ref, bc_ref, sq_ref, sk_ref, sv_ref):
    x = x_ref[...]
    m = mod_ref[...]
    h = _rms_rows(x, g_ref[...]) * (1.0 + m[4:5]) + m[3:4]
    z = jnp.dot(h.astype(BF16), win_ref[...], preferred_element_type=F32)
    o = 0
    gq_ref[...] = z[:, o:o + GLA_QK]; o += GLA_QK
    gk_ref[...] = z[:, o:o + GLA_QK]; o += GLA_QK
    gv_ref[...] = z[:, o:o + GLA_VW]; o += GLA_VW
    rg_ref[...] = z[:, o:o + GLA_VW]; o += GLA_VW
    sq = z[:, o:o + SWA_QW]; o += SWA_QW
    sk = z[:, o:o + SWA_KW]; o += SWA_KW
    sv_ref[...] = z[:, o:o + SWA_KW].astype(sv_ref.dtype); o += SWA_KW
    lowrank = z[:, o:o + LANES]

    logit = _dot(lowrank, w2_ref[...]) + b2_ref[...]
    log_a = (jnp.minimum(logit, 0.0) - jnp.log1p(jnp.exp(-jnp.abs(logit)))) * (1.0 / GLA_TAU)
    bc_ref[:, 0:GLA_QK] = _dot_sel(_chunk_tri(TM, False), log_a[:, 0:GLA_QK])
    bc_ref[:, GLA_QK:2 * GLA_QK] = _dot_sel(_chunk_tri(TM, True), log_a[:, GLA_QK:2 * GLA_QK])

    rc, rs1, rs2 = rc_ref[...], rs1_ref[...], rs2_ref[...]
    sqn = sq * lax.rsqrt(_seg_mean_sq(sq, SWA_DH, float(SWA_DH)) + EPS) * (qn_ref[...] * SWA_Q_PRESCALE)
    for s in range(SWA_QW // LANES):
        sl = slice(s * LANES, (s + 1) * LANES)
        sq_ref[:, sl] = _rope(sqn[:, sl], rc, rs1, rs2, SWA_DH // 2).astype(sq_ref.dtype)
    skn = sk * lax.rsqrt(_seg_mean_sq(sk, SWA_DH, float(SWA_DH)) + EPS) * kn_ref[...]
    sk_ref[...] = _rope(skn, rc, rs1, rs2, SWA_DH // 2).astype(sk_ref.dtype)


def _inproj_odd(xs, mod, g, p, rope):
    b, rows, d = xs.shape
    consts = [g.reshape(1, d), p["w_in"], p["w2"], p["b2"], p["q_norm"], p["k_norm"]]
    rope_spec = pl.BlockSpec((TM, LANES), lambda b_, t: (t, 0))
    widths = [(GLA_QK, F32), (GLA_QK, F32), (GLA_VW, F32), (GLA_VW, F32), (2 * GLA_QK, F32),
              (SWA_QW, BF16), (SWA_KW, BF16), (SWA_KW, BF16)]
    return pl.pallas_call(
        _inproj_odd_kernel,
        grid=(b, rows // TM),
        in_specs=[_tok_spec(d), _mod_spec(p["ctx_tiles"])] + [_const_spec(c.shape) for c in consts] + [rope_spec] * 3,
        out_specs=[_tok_spec(w) for w, _ in widths],
        out_shape=[jax.ShapeDtypeStruct((b, rows, w), dt) for w, dt in widths],
        compiler_params=_params(),
        name="inproj_odd",
    )(xs, mod, *consts, *rope)


def _gla_chunk(q, k, v, bc, st_ref, reverse):
    def blk(shape, rdiv, cdiv):
        return (lax.broadcasted_iota(jnp.int32, shape, 0) // rdiv) == (lax.broadcasted_iota(jnp.int32, shape, 1) // cdiv)

    ii = lax.broadcasted_iota(jnp.int32, (CHUNK, GLA_QK), 0)
    jj = lax.broadcasted_iota(jnp.int32, (CHUNK, GLA_QK), 1) % CHUNK
    incl = (ii <= jj) if reverse else (ii >= jj)
    last = 0 if reverse else CHUNK - 1
    q_dec = (q * (GLA_DK ** -0.5)) * jnp.exp(bc)
    k_inv = k * jnp.exp(-bc)
    b_last = bc[last:last + 1, :]
    k_dec = k * jnp.exp(b_last - bc)
    e_last = jnp.exp(b_last)
    kt = jnp.concatenate([k_inv.T.astype(BF16)] * GLA_HEADS, axis=1)
    rk = jnp.where(blk((GLA_QK, GLA_QK), GLA_DK, CHUNK), kt, jnp.zeros_like(kt))
    attn = jnp.where(incl, jnp.dot(q_dec.astype(BF16), rk, preferred_element_type=F32), 0.0)
    vt = jnp.concatenate([v.astype(BF16)] * GLA_HEADS, axis=0)
    vbd = jnp.where(blk((GLA_QK, GLA_VW), CHUNK, GLA_DV), vt, jnp.zeros_like(vt))
    st = st_ref[...]
    o = jnp.dot(attn.astype(BF16), vbd, preferred_element_type=F32) + _dot_nt(q_dec, st)
    st_ref[...] = st * e_last + jnp.where(blk((GLA_VW, GLA_QK), GLA_DV, GLA_DK), _dot_tn(v, k_dec), 0.0)
    return o


def _gla_scan_kernel(qf_ref, kf_ref, vf_ref, bf_ref, qb_ref, kb_ref, vb_ref, bb_ref, of_ref, ob_ref, sf_ref, sb_ref):
    @pl.when(pl.program_id(1) == 0)
    def _():
        sf_ref[...] = jnp.zeros_like(sf_ref)
        sb_ref[...] = jnp.zeros_like(sb_ref)

    for i in range(qf_ref.shape[0]):
        of_ref[i] = _gla_chunk(qf_ref[i], kf_ref[i], vf_ref[i], bf_ref[i][:, 0:GLA_QK], sf_ref.at[i], False)
        ob_ref[i] = _gla_chunk(qb_ref[i], kb_ref[i], vb_ref[i], bb_ref[i][:, GLA_QK:2 * GLA_QK], sb_ref.at[i], True)


def _gla_scan(q, k, v, bc, *, n_ctx):
    b, rows, _ = q.shape
    n_chunks = rows // CHUNK
    fwd, bwd = _scan_chunk_maps(n_ctx // CHUNK, n_chunks)
    nb = GLA_SCAN_SAMPLES if b % GLA_SCAN_SAMPLES == 0 else 1

    def specs(cm):
        tok = lambda w: pl.BlockSpec((nb, CHUNK, w), lambda b_, s: (b_, cm(s), 0))
        return [tok(GLA_QK), tok(GLA_QK), tok(GLA_VW), tok(2 * GLA_QK)]

    return pl.pallas_call(
        _gla_scan_kernel,
        grid=(b // nb, n_chunks),
        in_specs=specs(fwd) + specs(bwd),
        out_specs=[pl.BlockSpec((nb, CHUNK, GLA_VW), lambda b_, s: (b_, fwd(s), 0)),
                   pl.BlockSpec((nb, CHUNK, GLA_VW), lambda b_, s: (b_, bwd(s), 0))],
        out_shape=[jax.ShapeDtypeStruct((b, rows, GLA_VW), F32)] * 2,
        scratch_shapes=[pltpu.VMEM((nb, GLA_VW, GLA_QK), F32)] * 2,
        compiler_params=_params(),
        name="gla_scan",
    )(q, k, v, bc, q, k, v, bc)


def _swa_kernel(q_ref, k_ref, v_ref, sink_ref, o_ref, *, n_ctx, n_lat):
    w = SWA_WINDOW
    n = pl.program_id(1)
    nb = n_lat // w

    def rows(ref, blk):
        return ref[pl.ds(pl.multiple_of(n_ctx + blk * w, w), w), :]

    pb = jnp.maximum(n - 1, 0)
    xb = jnp.minimum(n + 1, nb - 1)
    k_cat = jnp.concatenate([rows(k_ref, pb), rows(k_ref, n), rows(k_ref, xb), k_ref[0:n_ctx, :]], axis=0)
    v_cat = jnp.concatenate([rows(v_ref, pb), rows(v_ref, n), rows(v_ref, xb), v_ref[0:n_ctx, :]], axis=0)
    v_one = jnp.concatenate([v_cat, jnp.ones_like(v_cat)], axis=1)
    half = SWA_HEADS // 2
    ii = lax.broadcasted_iota(jnp.int32, (half * w, w), 0) % w
    jj = lax.broadcasted_iota(jnp.int32, (half * w, w), 1)
    ninf = -jnp.inf
    bias_prev = jnp.where((jj >= ii) & (n > 0), 0.0, ninf)
    bias_next = jnp.where((jj <= ii) & (n < nb - 1), 0.0, ninf)
    lo = lax.broadcasted_iota(jnp.int32, (half * w, LANES), 1) < SWA_DH
    q = q_ref[...]
    qst = jnp.concatenate([q[:, j * LANES:(j + 1) * LANES] for j in range(half)], axis=0)
    sink = sink_ref[...]
    dims = (((1,), (1,)), ((), ()))
    qm = [jnp.where(lo if g == 0 else jnp.logical_not(lo), qst, jnp.zeros_like(qst)) for g in range(SWA_KV_HEADS)]
    s = [lax.dot_general(x, k_cat, dims, preferred_element_type=F32) for x in qm]
    parts = [[x[:, 0:w] + bias_prev, x[:, w:2 * w], x[:, 2 * w:3 * w] + bias_next, x[:, 3 * w:]] for x in s]
    sk = [jnp.concatenate([jnp.broadcast_to(sink[0:1, half * g + j:half * g + j + 1], (w, 1)) for j in range(half)],
                          axis=0) for g in range(SWA_KV_HEADS)]
    def row_max(ps, k_):
        slabs = [x[:, c * LANES:(c + 1) * LANES] for x in ps for c in range(x.shape[1] // LANES)]
        return jnp.maximum(jnp.max(functools.reduce(jnp.maximum, slabs), axis=-1, keepdims=True), k_)

    mx = [row_max(ps, k_) for ps, k_ in zip(parts, sk)]
    p = [jnp.concatenate([jnp.exp2(x - m).astype(BF16) for x in ps], axis=1) for ps, m in zip(parts, mx)]
    r = [jnp.dot(x, v_one, preferred_element_type=F32) for x in p]
    res = [x[:, 0:LANES] / (x[:, LANES:2 * LANES] + jnp.exp2(k_ - m)) for x, k_, m in zip(r, sk, mx)]
    o = jnp.where(lo, res[0], res[1]).astype(o_ref.dtype)
    for j in range(half):
        o_ref[:, j * LANES:(j + 1) * LANES] = o[j * w:(j + 1) * w, :]


def _swa_attention(q, k, v, sink, *, n_ctx):
    b, rows, _ = q.shape
    n_lat = rows - n_ctx
    w = SWA_WINDOW
    skip = n_ctx // w
    return pl.pallas_call(
        functools.partial(_swa_kernel, n_ctx=n_ctx, n_lat=n_lat),
        grid=(b, n_lat // w),
        in_specs=[
            pl.BlockSpec((None, w, SWA_QW), lambda b_, n: (b_, n + skip, 0)),
            pl.BlockSpec((None, rows, SWA_KW), lambda b_, n: (b_, 0, 0)),
            pl.BlockSpec((None, rows, SWA_KW), lambda b_, n: (b_, 0, 0)),
            _const_spec((1, LANES)),
        ],
        out_specs=pl.BlockSpec((None, w, SWA_QW), lambda b_, n: (b_, n, 0)),
        out_shape=jax.ShapeDtypeStruct((b, n_lat, SWA_QW), BF16),
        compiler_params=_params(),
        name="swa_attention",
    )(q, k, v, sink)


def _rope_tables(n_lat, n_ctx, rot_dim):
    t = jnp.arange(n_lat)
    row = (t // GRID_W).astype(F32)
    col = (t % GRID_W).astype(F32)
    n_freq = rot_dim // 4
    inv = ROPE_THETA ** (-jnp.arange(n_freq, dtype=F32) / n_freq)
    ang = jnp.concatenate([row[:, None] * inv, col[:, None] * inv], axis=-1)
    half = rot_dim // 2
    cos = jnp.concatenate([jnp.ones((n_ctx, half), F32), jnp.cos(ang)], axis=0)
    sin = jnp.concatenate([jnp.zeros((n_ctx, half), F32), jnp.sin(ang)], axis=0)
    rows = n_ctx + n_lat
    one = lambda w: jnp.ones((rows, w), F32)
    zero = lambda w: jnp.zeros((rows, w), F32)
    if rot_dim == MLA_ROPE:
        c = jnp.concatenate([one(MLA_NOPE), cos, cos, one(HEAD_PAD - MLA_QK)], axis=1)
        s1 = jnp.concatenate([zero(MLA_NOPE), -sin, zero(half), zero(HEAD_PAD - MLA_QK)], axis=1)
        s2 = jnp.concatenate([zero(MLA_NOPE), zero(half), sin, zero(HEAD_PAD - MLA_QK)], axis=1)
    else:
        c = jnp.concatenate([cos, cos, cos, cos], axis=1)
        s1 = jnp.concatenate([-sin, zero(half), -sin, zero(half)], axis=1)
        s2 = jnp.concatenate([zero(half), sin, zero(half), sin], axis=1)
    return c, s1, s2


def _even_params(j, n_ctx, ev_w_in, ev_q_a_norm, ev_w_q_up, ev_kv_a_norm, ev_w_kv_up, ev_mla_q_norm, ev_mla_k_norm,
                 ev_gdn_conv, ev_gdn_a_log, ev_gdn_dt_bias, ev_gdn_out_norm, ev_w_out):
    w = ev_w_in[j]
    d = w.shape[0]
    z = lambda n: jnp.zeros((d, n), F32)
    o_kr = MLA_Q_RANK + MLA_KV_RANK
    o_g = o_kr + MLA_ROPE
    o_small = o_g + 3 * GDN_QK
    o_gate = o_small + 4 * GDN_HEADS
    w_in = jnp.concatenate([
        w[:, :o_kr], z(MLA_NOPE), w[:, o_kr:o_g], z(HEAD_PAD - MLA_QK),
        w[:, o_g:o_small], w[:, o_gate:o_gate + GDN_VW],
        w[:, o_small:o_gate], z(LANES - 4 * GDN_HEADS)], axis=1).astype(BF16)
    pad_h = HEAD_PAD - MLA_QK
    w_q = jnp.pad(ev_w_q_up[j].reshape(MLA_Q_RANK, MLA_HEADS, MLA_QK), ((0, 0), (0, 0), (0, pad_h)))
    wkv = ev_w_kv_up[j].reshape(MLA_KV_RANK, MLA_HEADS, MLA_NOPE + MLA_V)
    w_kk = jnp.pad(wkv[:, :, :MLA_NOPE], ((0, 0), (0, 0), (0, HEAD_PAD - MLA_NOPE)))
    lane_row = lambda vec: jnp.pad(vec, (0, LANES - vec.shape[0])).reshape(1, LANES)
    fb = lambda a: jnp.concatenate([a[0], a[1]])
    return {
        "ctx_tiles": n_ctx // TM,
        "w_in": w_in,
        "q_a_norm": ev_q_a_norm[j].reshape(1, -1),
        "w_q": w_q.reshape(MLA_Q_RANK, MLA_HEADS * HEAD_PAD).astype(BF16),
        "kv_a_norm": ev_kv_a_norm[j].reshape(1, -1),
        "w_kk": w_kk.reshape(MLA_KV_RANK, MLA_HEADS * HEAD_PAD).astype(BF16),
        "w_kv": jnp.pad(wkv[:, :, MLA_NOPE:], ((0, 0), (0, 0), (0, HEAD_PAD - MLA_V))
                        ).reshape(MLA_KV_RANK, MLA_HEADS * HEAD_PAD).astype(BF16),
        "q_norm": lane_row(ev_mla_q_norm[j]),
        "k_norm": lane_row(ev_mla_k_norm[j]),
        "conv_w": ev_gdn_conv[j],
        "neg_a": lane_row(-jnp.exp(fb(ev_gdn_a_log[j]))),
        "dt_bias": lane_row(fb(ev_gdn_dt_bias[j])),
        "out_norm": jnp.tile(ev_gdn_out_norm[j], GDN_HEADS).reshape(1, GDN_VW),
        "w_out": ev_w_out[j].astype(BF16),
    }


def _swa_head_perm():
    half = SWA_HEADS // 2
    heads = [h for j in range(half) for h in (j, half + j)]
    return jnp.concatenate([jnp.arange(SWA_DH) + h * SWA_DH for h in heads])


def _odd_params(j, n_ctx, od_w_in, od_gla_gate_w2, od_gla_gate_b, od_gla_out_norm, od_swa_q_norm, od_swa_k_norm,
                od_swa_sink, od_w_out):
    w = od_w_in[j]
    d = w.shape[0]
    o_gate = 2 * GLA_QK + GLA_VW
    o_rg = o_gate + 2 * GLA_RANK
    o_sq = o_rg + GLA_VW
    o_sk = o_sq + SWA_QW
    perm = _swa_head_perm()
    w_in = jnp.concatenate([
        w[:, :o_gate], w[:, o_rg:o_sq], w[:, o_sq:o_sk][:, perm], w[:, o_sk:],
        w[:, o_gate:o_rg], jnp.zeros((d, LANES - 2 * GLA_RANK), F32)], axis=1).astype(BF16)
    w2 = jnp.zeros((LANES, 2 * GLA_QK), F32)
    w2 = w2.at[0:GLA_RANK, 0:GLA_QK].set(od_gla_gate_w2[j, 0])
    w2 = w2.at[GLA_RANK:2 * GLA_RANK, GLA_QK:].set(od_gla_gate_w2[j, 1])
    wo = od_w_out[j]
    w_out = jnp.concatenate([wo[:GLA_VW], wo[GLA_VW:][perm]], axis=0).astype(BF16)
    return {
        "ctx_tiles": n_ctx // TM,
        "w_in": w_in,
        "w2": w2.astype(BF16),
        "b2": jnp.concatenate([od_gla_gate_b[j, 0], od_gla_gate_b[j, 1]]).reshape(1, 2 * GLA_QK),
        "q_norm": jnp.tile(od_swa_q_norm[j], SWA_HEADS).reshape(1, SWA_QW),
        "k_norm": jnp.tile(od_swa_k_norm[j], SWA_KV_HEADS).reshape(1, SWA_KW),
        "sink": jnp.pad(od_swa_sink[j] * math.log2(math.e), (0, LANES - SWA_HEADS)).reshape(1, LANES),
        "out_norm": jnp.tile(od_gla_out_norm[j], GLA_HEADS).reshape(1, GLA_VW),
        "w_out": w_out,
    }


def kernel(x, c, ctx, c_ctx, ada_w, ada_b, norm_g, ffn_w_gate, ffn_w_up, ffn_w_down, ev_w_in, ev_q_a_norm, ev_w_q_up, ev_kv_a_norm, ev_w_kv_up, ev_mla_q_norm, ev_mla_k_norm, ev_gdn_conv, ev_gdn_a_log, ev_gdn_dt_bias, ev_gdn_out_norm, ev_w_out, od_w_in, od_gla_gate_w2, od_gla_gate_b, od_gla_out_norm, od_swa_q_norm, od_swa_k_norm, od_swa_sink, od_w_out):
    b, n_lat, d = x.shape
    n_ctx = ctx.shape[1]
    depth = ada_w.shape[0]
    assert d == D_MODEL and n_ctx % TM == 0 and n_lat % TM == 0 and n_lat % GRID_W == 0
    assert depth % 2 == 0, "the last layer must be an odd (GLA/SWA) layer: context outputs of that mixer are not built"
    ctx_tiles = n_ctx // TM

    cvec = jnp.concatenate([c, c_ctx[None, :], jnp.zeros((16 - b - 1, d), F32)], axis=0)
    mod_all = _ada(cvec, ada_w, ada_b).reshape(depth, 16, N_MOD, d)
    rope_mla = _rope_tables(n_lat, n_ctx, MLA_ROPE)
    rope_swa = _rope_tables(n_lat, n_ctx, SWA_DH)

    wg, wu, wd = (t.astype(BF16) for t in (ffn_w_gate, ffn_w_up, ffn_w_down))
    xs = x
    for i in range(depth):
        last = i == depth - 1
        mod = jnp.stack([jnp.broadcast_to(mod_all[i, b][None], (b, N_MOD, d)), mod_all[i, :b]], axis=1)
        xs = _ffn(xs, mod, norm_g[i, 0], wg, wu, wd, layer=i, half=0, r0=0, n_ctx=n_ctx, ctx=ctx if i == 0 else None)
        j = i // 2
        skip = ctx_tiles if last else 0
        if i % 2 == 0:
            p = _even_params(j, n_ctx, ev_w_in, ev_q_a_norm, ev_w_q_up, ev_kv_a_norm, ev_w_kv_up, ev_mla_q_norm,
                             ev_mla_k_norm, ev_gdn_conv, ev_gdn_a_log, ev_gdn_dt_bias, ev_gdn_out_norm, ev_w_out)
            q, k, v, zg, gate, small = _inproj_even(xs, mod, norm_g[i, 1], p, rope_mla)
            o_att = _mla_attention(q, k, v, n_ctx=n_ctx)
            gq, gk, gv, gates = _gdn_prep(zg, small, p["conv_w"], p["neg_a"], p["dt_bias"], ctx_tiles=ctx_tiles)
            o_f, o_b = _gdn_scan(gq, gk, gv, gates, n_ctx=n_ctx)
            mix = dict(o_att=o_att, o_f=o_f, o_b=o_b, gate=gate, seg=GDN_DV, rec_first=False, att_skip=skip)
        else:
            assert last, "odd layers that must also produce context outputs are not built"
            p = _odd_params(j, n_ctx, od_w_in, od_gla_gate_w2, od_gla_gate_b, od_gla_out_norm, od_swa_q_norm,
                            od_swa_k_norm, od_swa_sink, od_w_out)
            gq, gk, gv, rg, bc, sq, sk, sv = _inproj_odd(xs, mod, norm_g[i, 1], p, rope_swa)
            o_f, o_b = _gla_scan(gq, gk, gv, bc, n_ctx=n_ctx)
            o_att = _swa_attention(sq, sk, sv, p["sink"], n_ctx=n_ctx)
            mix = dict(o_att=o_att, o_f=o_f, o_b=o_b, gate=rg, seg=GLA_DV, rec_first=True, att_skip=0)
        xs = _outproj_ffn(xs, mod, mix["o_att"], mix["o_f"], mix["o_b"], mix["gate"], p["out_norm"], p["w_out"],
                          norm_g[i, 2], wg, wu, wd, layer=i, seg=mix["seg"], rec_first=mix["rec_first"],
                          ctx_tiles=ctx_tiles, skip=skip, att_skip=mix["att_skip"])
    return xs
```

```python
import functools
import math

import jax
import jax.numpy as jnp
from jax import lax
from jax.experimental import pallas as pl
from jax.experimental.pallas import tpu as pltpu

F32 = jnp.float32
BF16 = jnp.bfloat16

D_MODEL = 1024
GRID_W = 64
D_FF = 2816
FFN_RES = 0.5
N_MOD = 9
EPS = 1e-6
ROPE_THETA = 10000.0
CHUNK = 64

MLA_HEADS = 8
MLA_NOPE = 64
MLA_ROPE = 32
MLA_QK = MLA_NOPE + MLA_ROPE
MLA_V = 64
MLA_Q_RANK = 384
MLA_KV_RANK = 256
MLA_VW = MLA_HEADS * MLA_V
MLA_STEP_HEADS = 4
MLA_Q_PRESCALE =MLA_QK ** -0.5 * math.log2(math.e)

GDN_HEADS = 8
GDN_DK = 64
GDN_DV = 64
GDN_CONV = 5
GDN_QK = GDN_HEADS * GDN_DK
GDN_VW = GDN_HEADS * GDN_DV
GDN_HALF = 128
GDN_TILES = GDN_QK // GDN_HALF
GDN_SCAN_SAMPLES = 8

GLA_HEADS = 4
GLA_DK = 64
GLA_DV = 128
GLA_RANK = 16
GLA_TAU = 16.0
GLA_QK = GLA_HEADS * GLA_DK
GLA_VW = GLA_HEADS * GLA_DV
GLA_SCAN_SAMPLES = 8

SWA_HEADS = 8
SWA_KV_HEADS = 2
SWA_DH = 64
SWA_WINDOW = 128
SWA_QW = SWA_HEADS * SWA_DH
SWA_KW = SWA_KV_HEADS * SWA_DH
SWA_Q_PRESCALE = SWA_DH ** -0.5 * math.log2(math.e)

LANES = 128
HEAD_PAD = 128
TM = 256
FFN_TILES = (576, 512, 256, 128)
VMEM_LIMIT =56 * 1024 * 1024

EVEN_COLS = MLA_Q_RANK + MLA_KV_RANK + LANES + 2 * GDN_QK + GDN_VW + GDN_VW + LANES
ODD_COLS = 2 * GLA_QK + GLA_VW + GLA_VW + SWA_QW + 2 * SWA_KW + LANES


def _dot(a, b):
    return jnp.dot(a.astype(BF16), b.astype(BF16), preferred_element_type=F32)


def _dot_nt(a, b):
    return lax.dot_general(a.astype(BF16), b.astype(BF16), (((1,), (1,)), ((), ())), preferred_element_type=F32)


def _dot_tn(a, b):
    return lax.dot_general(a.astype(BF16), b.astype(BF16), (((0,), (0,)), ((), ())), preferred_element_type=F32)


def _split2(x):
    hi = x.astype(BF16)
    lo = (x - hi.astype(F32)).astype(BF16)
    return hi, lo


def _split3(x):
    hi = x.astype(BF16)
    r = x - hi.astype(F32)
    mid = r.astype(BF16)
    lo = (r - mid.astype(F32)).astype(BF16)
    return hi, mid, lo


def _dot3(a, b):
    ah, al = _split2(a)
    bh, bl = _split2(b)
    d = functools.partial(jnp.dot, preferred_element_type=F32)
    return d(ah, bh) + (d(ah, bl) + d(al, bh))


def _dot_sel(sel, x):
    s = sel.astype(BF16)
    hi, mid, lo = _split3(x)
    d = functools.partial(jnp.dot, preferred_element_type=F32)
    return d(s, hi) + (d(s, mid) + d(s, lo))


def _dot_sel_r(x, sel):
    s = sel.astype(BF16)
    hi, mid, lo = _split3(x)
    d = functools.partial(jnp.dot, preferred_element_type=F32)
    return d(hi, s) + (d(mid, s) + d(lo, s))


def _seg_mean_sq(x, seg, n_real):
    w = x.shape[-1]
    r = lax.broadcasted_iota(jnp.int32, (w, w), 0) // seg
    c = lax.broadcasted_iota(jnp.int32, (w, w), 1) // seg
    ones_bd = (r == c).astype(BF16)
    hi, lo = _split2(x * x)
    d = functools.partial(jnp.dot, preferred_element_type=F32)
    return (d(hi, ones_bd) + d(lo, ones_bd)) * (1.0 / n_real)


def _rms_rows(x, g):
    ms = jnp.mean(x * x, axis=-1, keepdims=True)
    return x * lax.rsqrt(ms + EPS) * g


def _sigmoid(x):
    return 1.0 / (1.0 + jnp.exp(-x))


def _silu(x):
    return x * _sigmoid(x)


def _softplus(x):
    return jnp.maximum(x, 0.0) + jnp.log1p(jnp.exp(-jnp.abs(x)))


def _rope(x, c, s1, s2, half):
    return x * c + pltpu.roll(x, LANES - half, 1) * s1 + pltpu.roll(x, half, 1) * s2


def _chunk_tri(n, reverse):
    i = lax.broadcasted_iota(jnp.int32, (n, n), 0)
    j = lax.broadcasted_iota(jnp.int32, (n, n), 1)
    same = (i // CHUNK) == (j // CHUNK)
    tri = (j >= i) if reverse else (j <= i)
    return (same & tri).astype(F32)


def _ada_kernel(c_ref, w_ref, b_ref, o_ref):
    sc = _silu(c_ref[...])
    o_ref[...] = _dot(sc, w_ref[...]) + b_ref[...]


def _ada(cvec, ada_w, ada_b):
    depth, d, nm = ada_w.shape
    tn = 1024
    return pl.pallas_call(
        _ada_kernel,
        grid=(depth, nm // tn),
        in_specs=[
            pl.BlockSpec(cvec.shape, lambda i, j: (0, 0)),
            pl.BlockSpec((None, d, tn), lambda i, j: (i, 0, j)),
            pl.BlockSpec((None, 1, tn), lambda i, j: (i, 0, j)),
        ],
        out_specs=pl.BlockSpec((None, cvec.shape[0], tn), lambda i, j: (i, 0, j)),
        out_shape=jax.ShapeDtypeStruct((depth, cvec.shape[0], nm), F32),
        compiler_params=pltpu.CompilerParams(vmem_limit_bytes=VMEM_LIMIT),
        name="ada_mod",
    )(cvec, ada_w, ada_b.reshape(depth, 1, nm))


def _const_spec(shape):
    nd = len(shape)
    return pl.BlockSpec(shape, lambda *_: (0,) * nd, pipeline_mode=pl.Buffered(1))


def _tok_spec(width, skip=0):
    return pl.BlockSpec((None, TM, width), lambda b, t: (b, t + skip, 0))


def _mod_spec(ctx_tiles, skip=0):
    return pl.BlockSpec((None, None, N_MOD, D_MODEL),
                        lambda b, t: (b, jnp.where(t + skip >= ctx_tiles, 1, 0), 0, 0))


def _params():
    return pltpu.CompilerParams(dimension_semantics=("parallel", "arbitrary"), vmem_limit_bytes=VMEM_LIMIT)


def _swiglu(hb, wg_ref, wu_ref, wd_ref):
    a = jnp.dot(hb, wg_ref[...], preferred_element_type=F32)
    u = jnp.dot(hb, wu_ref[...], preferred_element_type=F32)
    return jnp.dot(_silu(a) * u, wd_ref[...], preferred_element_type=F32)


def _ffn_kernel(*refs, r0, n_ctx, split):
    if split:
        c_ref, x_ref, mod_ref, g_ref, wg_ref, wu_ref, wd_ref, o_ref = refs
    else:
        x_ref, mod_ref, g_ref, wg_ref, wu_ref, wd_ref, o_ref = refs
    x = x_ref[...]
    m = mod_ref[...]
    tm = x.shape[0]
    if n_ctx:
        is_ctx = pl.program_id(1) * tm + lax.broadcasted_iota(jnp.int32, (tm, 1), 0) < n_ctx
        row = lambda r: jnp.where(is_ctx, m[0, r:r + 1], m[1, r:r + 1])
        if split:
            x = jnp.where(is_ctx, c_ref[...], x)
    else:
        row = lambda r: m[1, r:r + 1]
    h = _rms_rows(x, g_ref[...]) * (1.0 + row(r0 + 1)) + row(r0)
    y = _swiglu(h, wg_ref, wu_ref, wd_ref)
    o_ref[...] = x + (FFN_RES * row(r0 + 2)) * y


def _ffn(xs, mod, g, wg, wu, wd, *, layer, half, r0, n_ctx, ctx=None):
    b, rows, d = xs.shape
    split = ctx is not None
    if split:
        tm = TM
        ctx_tiles = n_ctx // tm
        rows += n_ctx
        toks = [pl.BlockSpec((None, tm, d), lambda b_, t: (b_, jnp.minimum(t, ctx_tiles - 1), 0)),
                pl.BlockSpec((None, tm, d), lambda b_, t: (b_, jnp.maximum(t - ctx_tiles, 0), 0))]
        data = [ctx, xs]
    else:
        tm = next(t for t in FFN_TILES if rows % t == 0)
        toks = [pl.BlockSpec((None, tm, d), lambda b_, t: (b_, t, 0))]
        data = [xs]
    wspec = lambda w: pl.BlockSpec((None, None) + w.shape[2:], lambda b_, t: (layer, half, 0, 0),
                                   pipeline_mode=pl.Buffered(1))
    return pl.pallas_call(
        functools.partial(_ffn_kernel, r0=r0, n_ctx=n_ctx, split=split),
        grid=(b, rows // tm),
        in_specs=toks + [pl.BlockSpec((None, 2, N_MOD, d), lambda b_, t: (b_, 0, 0, 0)), _const_spec((1, d)),
                         wspec(wg), wspec(wu), wspec(wd)],
        out_specs=pl.BlockSpec((None, tm, d), lambda b_, t: (b_, t, 0)),
        out_shape=jax.ShapeDtypeStruct((b, rows, d), F32),
        compiler_params=_params(),
        name="ffn_half_step",
    )(*data, mod, g.reshape(1, d), wg, wu, wd)


def _inproj_even_kernel(x_ref, mod_ref, g_ref, win_ref, qan_ref, wq_ref, kvan_ref, wkk_ref, wkv_ref,
                        qn_ref, kn_ref, rc_ref, rs1_ref, rs2_ref,
                        q_ref, k_ref, v_ref, zg_ref, gate_ref, small_ref):
    x = x_ref[...]
    m = mod_ref[...]
    h = _rms_rows(x, g_ref[...]) * (1.0 + m[4:5]) + m[3:4]
    z = jnp.dot(h.astype(BF16), win_ref[...], preferred_element_type=F32)
    o = 0
    cq = z[:, o:o + MLA_Q_RANK]; o += MLA_Q_RANK
    ckv = z[:, o:o + MLA_KV_RANK]; o += MLA_KV_RANK
    kr = z[:, o:o + LANES]; o += LANES
    zg_ref[...] = z[:, o:o + 3 * GDN_QK]; o += 3 * GDN_QK
    gate_ref[...] = z[:, o:o + GDN_VW]; o += GDN_VW
    small_ref[...] = z[:, o:o + LANES]

    rc, rs1, rs2 = rc_ref[...], rs1_ref[...], rs2_ref[...]
    qn, kn = qn_ref[...], kn_ref[...]
    qf = jnp.dot(_rms_rows(cq, qan_ref[...]).astype(BF16), wq_ref[...], preferred_element_type=F32)
    ckvn = _rms_rows(ckv, kvan_ref[...]).astype(BF16)
    kf = jnp.dot(ckvn, wkk_ref[...], preferred_element_type=F32)
    lane = lax.broadcasted_iota(jnp.int32, (1, MLA_HEADS * HEAD_PAD), 1)
    ones_cols = (lane % HEAD_PAD >= MLA_V).astype(F32)
    v_ref[...] = (jnp.dot(ckvn, wkv_ref[...], preferred_element_type=F32) + ones_cols).astype(v_ref.dtype)
    slabs = [slice(hd * HEAD_PAD, (hd + 1) * HEAD_PAD) for hd in range(MLA_HEADS)]
    xs = [qf[:, sl] for sl in slabs] + [kf[:, sl] + kr for sl in slabs]
    gains = [qn * MLA_Q_PRESCALE] * MLA_HEADS + [kn] * MLA_HEADS
    ss = [jnp.sum(x * x, axis=-1, keepdims=True) for x in xs]
    xs = [x * lax.rsqrt(s * (1.0 / MLA_QK) + EPS) * g for x, s, g in zip(xs, ss, gains)]
    xs = [_rope(x, rc, rs1, rs2, MLA_ROPE // 2) for x in xs]
    for hd, sl in enumerate(slabs):
        q_ref[:, sl] = xs[hd].astype(q_ref.dtype)
        k_ref[:, sl] = xs[MLA_HEADS + hd].astype(k_ref.dtype)


def _inproj_even(xs, mod, g, p, rope):
    b, rows, d = xs.shape
    ctx_tiles = p["ctx_tiles"]
    consts = [g.reshape(1, d), p["w_in"], p["q_a_norm"], p["w_q"], p["kv_a_norm"], p["w_kk"], p["w_kv"],
              p["q_norm"], p["k_norm"]]
    rope_spec = pl.BlockSpec((TM, LANES), lambda b_, t: (t, 0))
    widths = [(MLA_HEADS * HEAD_PAD, BF16), (MLA_HEADS * HEAD_PAD, BF16), (MLA_HEADS * HEAD_PAD, BF16),
              (3 * GDN_QK, F32), (GDN_VW, F32), (LANES, F32)]
    return pl.pallas_call(
        _inproj_even_kernel,
        grid=(b, rows // TM),
        in_specs=[_tok_spec(d), _mod_spec(ctx_tiles)] + [_const_spec(c.shape) for c in consts] + [rope_spec] * 3,
        out_specs=[_tok_spec(w) for w, _ in widths],
        out_shape=[jax.ShapeDtypeStruct((b, rows, w), dt) for w, dt in widths],
        compiler_params=_params(),
        name="inproj_even",
    )(xs, mod, *consts, *rope)


def _mla_kernel(q_ref, k_ref, v_ref, o_ref, *, n_ctx, n_all, ctx_tiles):
    t = pl.program_id(2)
    lo = lax.broadcasted_iota(jnp.int32, (TM, LANES), 1) < MLA_V
    n_heads = q_ref.shape[-1] // HEAD_PAD

    def attend(nk):
        q = q_ref[...]
        sls = [slice(hh * HEAD_PAD, (hh + 1) * HEAD_PAD) for hh in range(n_heads)]
        s = [lax.dot_general(q[:, sl], k_ref[0:nk, sl], (((1,), (1,)), ((), ())), preferred_element_type=F32)
             for sl in sls]
        mx = [jnp.max(x, axis=-1, keepdims=True) for x in s]
        p = [jnp.exp2(x - m).astype(BF16) for x, m in zip(s, mx)]
        r = [jnp.dot(x, v_ref[0:nk, sl], preferred_element_type=F32) for x, sl in zip(p, sls)]
        outs = [x / x[:, MLA_V:MLA_V + 1] for x in r]
        for pr in range(n_heads // 2):
            o_ref[:, pr * LANES:(pr + 1) * LANES] = jnp.where(
                lo, outs[2 * pr], pltpu.roll(outs[2 * pr + 1], MLA_V, 1)).astype(o_ref.dtype)

    @pl.when(t < ctx_tiles)
    def _():
        attend(n_ctx)

    @pl.when(t >= ctx_tiles)
    def _():
        attend(n_all)


def _mla_attention(q, k, v, *, n_ctx):
    b, rows, _ = q.shape
    hs = MLA_STEP_HEADS
    return pl.pallas_call(
        functools.partial(_mla_kernel, n_ctx=n_ctx, n_all=rows, ctx_tiles=n_ctx // TM),
        grid=(b, MLA_HEADS // hs, rows // TM),
        in_specs=[
            pl.BlockSpec((None, TM, hs * HEAD_PAD), lambda b_, h, t: (b_, t, h)),
            pl.BlockSpec((None, rows, hs * HEAD_PAD), lambda b_, h, t: (b_, 0, h)),
            pl.BlockSpec((None, rows, hs * HEAD_PAD), lambda b_, h, t: (b_, 0, h)),
        ],
        out_specs=pl.BlockSpec((None, TM, hs * MLA_V), lambda b_, h, t: (b_, t, h)),
        out_shape=jax.ShapeDtypeStruct((b, rows, MLA_VW), BF16),
        compiler_params=pltpu.CompilerParams(dimension_semantics=("parallel", "parallel", "arbitrary"),
                                             vmem_limit_bytes=VMEM_LIMIT),
        name="mla_attention",
    )(q, k, v)


def _gdn_prep_kernel(z_ref, zp_ref, zn_ref, cw_ref, sm_ref, nega_ref, dtb_ref,
                     q_ref, k_ref, v_ref, g_ref, ext_ref, *, ctx_tiles, n_tiles):
    t = pl.program_id(1)
    first = (t == 0) | (t == ctx_tiles)
    last = (t == ctx_tiles - 1) | (t == n_tiles - 1)
    half = GDN_CONV // 2
    ext_ref[0:8, :] = jnp.where(first, 0.0, zp_ref[...])
    ext_ref[8:8 + TM, :] = z_ref[...]
    ext_ref[8 + TM:16 + TM, :] = jnp.where(last, 0.0, zn_ref[...])
    cw = cw_ref[...]
    acc = z_ref[...] * cw[half:half + 1]
    for j in range(GDN_CONV):
        if j != half:
            acc = acc + ext_ref[8 - half + j:8 - half + j + TM, :] * cw[j:j + 1]
    qkv = _silu(acc)
    q = qkv[:, :GDN_QK]
    k = qkv[:, GDN_QK:2 * GDN_QK]
    q_ref[...] = q * lax.rsqrt(_seg_mean_sq(q, GDN_DK, 1.0) + EPS) * (GDN_DK ** -0.5)
    k_ref[...] = k * lax.rsqrt(_seg_mean_sq(k, GDN_DK, 1.0) + EPS)
    v_ref[...] = qkv[:, 2 * GDN_QK:]

    sm = sm_ref[...]
    lane = lax.broadcasted_iota(jnp.int32, sm.shape, 1)
    g = nega_ref[...] * _softplus(sm + dtb_ref[...])
    g = jnp.where(lane < 2 * GDN_HEADS, g, 0.0)
    gc_f = _dot_sel(_chunk_tri(TM, False), g)
    gc_b = _dot_sel(_chunk_tri(TM, True), g)
    g_ref[...] = jnp.where(lane < GDN_HEADS, gc_f, jnp.where(lane < 2 * GDN_HEADS, gc_b, _sigmoid(sm)))


def _gdn_prep(zg, small, conv_w, nega, dtb, *, ctx_tiles):
    b, rows, w = zg.shape
    n_tiles = rows // TM
    hb = TM // 8
    n_hblk = rows // 8
    return pl.pallas_call(
        functools.partial(_gdn_prep_kernel, ctx_tiles=ctx_tiles, n_tiles=n_tiles),
        grid=(b, n_tiles),
        in_specs=[
            _tok_spec(w),
            pl.BlockSpec((None, 8, w), lambda b_, t: (b_, jnp.maximum(t * hb - 1, 0), 0)),
            pl.BlockSpec((None, 8, w), lambda b_, t: (b_, jnp.minimum((t + 1) * hb, n_hblk - 1), 0)),
            _const_spec(conv_w.shape), _tok_spec(LANES), _const_spec((1, LANES)), _const_spec((1, LANES)),
        ],
        out_specs=[_tok_spec(GDN_QK), _tok_spec(GDN_QK), _tok_spec(GDN_VW), _tok_spec(LANES)],
        out_shape=[jax.ShapeDtypeStruct((b, rows, GDN_QK), F32), jax.ShapeDtypeStruct((b, rows, GDN_QK), F32),
                   jax.ShapeDtypeStruct((b, rows, GDN_VW), F32), jax.ShapeDtypeStruct((b, rows, LANES), F32)],
        scratch_shapes=[pltpu.VMEM((TM + 16, w), F32)],
        compiler_params=_params(),
        name="gdn_prep",
    )(zg, zg, zg, conv_w, small, nega, dtb)


def _bd_halves(y, bdm):
    yb = y.astype(BF16)
    out = []
    for s in range(GDN_TILES):
        t = jnp.concatenate([yb[:, s * GDN_HALF:(s + 1) * GDN_HALF]] * (GDN_HALF // GDN_DK), axis=0)
        out.append(jnp.where(bdm, t, jnp.zeros_like(t)))
    return out


def _hprod(x, bd):
    xb = x.astype(BF16)
    d = functools.partial(jnp.dot, preferred_element_type=F32)
    return jnp.concatenate([d(xb[:, s * GDN_HALF:(s + 1) * GDN_HALF], bd[s]) for s in range(GDN_TILES)], axis=1)


def _hprod3(x, y, bdm):
    xh, xl = _split2(x)
    yh, yl = _split2(y)
    bh, bl = _bd_halves(yh, bdm), _bd_halves(yl, bdm)
    return _hprod(xh, bh) + (_hprod(xh, bl) + _hprod(xl, bh))


def _gdn_chunks(chains):
    ii = lax.broadcasted_iota(jnp.int32, (CHUNK, GDN_QK), 0)
    jj = lax.broadcasted_iota(jnp.int32, (CHUNK, GDN_QK), 1) % CHUNK
    r = lax.broadcasted_iota(jnp.int32, (GDN_HALF, GDN_HALF), 0) // GDN_DK
    c = lax.broadcasted_iota(jnp.int32, (GDN_HALF, GDN_HALF), 1) // GDN_DK
    bdm = r == c
    diag = ii == jj
    eye = diag.astype(F32)
    n = len(chains)
    qs, ks, vs, gs, srefs, revs = (list(t) for t in zip(*chains))
    each = lambda f, *ls: [f(*a) for a in zip(*ls)]
    bd = lambda ys: [_bd_halves(y, bdm) for y in ys]

    incl = [(ii <= jj) if rv else (ii >= jj) for rv in revs]
    strict = [(ii < jj) if rv else (ii > jj) for rv in revs]
    last = [0 if rv else CHUNK - 1 for rv in revs]
    gc = [g[:, 0:GDN_QK] for g in gs]
    beta = [g[:, GDN_QK:2 * GDN_QK] for g in gs]
    gc_row = [jnp.sum(jnp.where(diag, x, 0.0), axis=0, keepdims=True) for x in gc]
    decay = each(lambda x, xr, ic: jnp.exp(jnp.where(ic, x - xr, -jnp.inf)), gc, gc_row, incl)

    def kt_bd(k):
        kt = k.T.astype(BF16)
        out = []
        for s in range(GDN_TILES):
            t = jnp.concatenate([kt[s * GDN_HALF:(s + 1) * GDN_HALF]] * (GDN_HALF // CHUNK), axis=1)
            out.append(jnp.where(bdm, t, jnp.zeros_like(t)))
        return out

    kkqk = each(lambda k, q, rk: _hprod(jnp.concatenate([k, q], axis=0), rk), ks, qs, [kt_bd(k) for k in ks])
    a = each(lambda st, b, x, dc: jnp.where(st, b * x[0:CHUNK] * dc, 0.0), strict, beta, kkqk, decay)
    qk = each(lambda x, dc: x[CHUNK:2 * CHUNK] * dc, kkqk, decay)

    def level_mask(blk, rv):
        same = (ii // (2 * blk)) == (jj // (2 * blk))
        off = (ii // blk) < (jj // blk) if rv else (ii // blk) > (jj // blk)
        return same & off

    m = each(lambda x, rv: eye - jnp.where(level_mask(1, rv), x, 0.0), a, revs)
    blk = 2
    while blk < CHUNK:
        am = each(lambda x, rv: jnp.where(level_mask(blk, rv), x, 0.0), a, revs)
        x = each(_hprod, am, bd(m))
        m = each(lambda mm, y, ybd: mm - _hprod(mm, ybd), m, x, bd(x))
        blk *= 2
    am = each(lambda x, mm: _hprod3(x, mm, bdm), a, m)
    resid = each(lambda mm, y: eye - mm - y, m, am)
    m = each(lambda mm, rbd: mm + _hprod(mm, rbd), m, bd(resid))

    egc = [jnp.exp(x) for x in gc]
    u = each(_hprod, m, bd(each(lambda v, b: v * b, vs, beta)))
    w = each(_hprod, m, bd(each(lambda k, b, e: k * (b * e), ks, beta, egc)))
    g_last = each(lambda x, l: x[l:l + 1, :], gc, last)
    st = [[sr[s] for s in range(GDN_TILES)] for sr in srefs]

    def sprod(x, tiles):
        return jnp.concatenate([_dot(x[:, s * GDN_HALF:(s + 1) * GDN_HALF], tiles[s]) for s in range(GDN_TILES)],
                               axis=1)

    ws = each(lambda ww, q, e, tiles: sprod(jnp.concatenate([ww, q * e], axis=0), tiles), w, qs, egc, st)
    v_new = each(lambda uu, x: uu - x[0:CHUNK], u, ws)
    o = each(lambda x, y, vbd: x[CHUNK:2 * CHUNK] + _hprod(y, vbd), ws, qk, bd(v_new))
    k_dec = each(lambda k, gl, x: k * jnp.exp(gl - x), ks, g_last, gc)
    e_last = [jnp.exp(gl) for gl in g_last]
    for i in range(n):
        for s in range(GDN_TILES):
            sl = slice(s * GDN_HALF, (s + 1) * GDN_HALF)
            srefs[i][s] = st[i][s] * e_last[i][:, sl] + jnp.where(
                bdm, _dot_tn(k_dec[i][:, sl], v_new[i][:, sl]), 0.0)
    return o


def _gdn_scan_kernel(qf_ref, kf_ref, vf_ref, gf_ref, qb_ref, kb_ref, vb_ref, gb_ref, of_ref, ob_ref, sf_ref, sb_ref):
    @pl.when(pl.program_id(1) == 0)
    def _():
        sf_ref[...] = jnp.zeros_like(sf_ref)
        sb_ref[...] = jnp.zeros_like(sb_ref)

    nb = qf_ref.shape[0]
    row = lax.broadcasted_iota(jnp.int32, (LANES, 2 * GDN_QK), 0)
    col = lax.broadcasted_iota(jnp.int32, (LANES, 2 * GDN_QK), 1)
    src = (col % GDN_QK) // GDN_DK + jnp.where(col >= GDN_QK, 2 * GDN_HEADS, 0)

    def spread(ref, d):
        stack = jnp.concatenate([ref[i] for i in range(nb)], axis=0)
        return _dot_sel_r(stack, (row == src + d * GDN_HEADS).astype(F32))

    gf, gb = spread(gf_ref, 0), spread(gb_ref, 1)
    chains = []
    for i in range(nb):
        rows = slice(i * CHUNK, (i + 1) * CHUNK)
        chains.append((qf_ref[i], kf_ref[i], vf_ref[i], gf[rows], sf_ref.at[i], False))
        chains.append((qb_ref[i], kb_ref[i], vb_ref[i], gb[rows], sb_ref.at[i], True))
    outs = _gdn_chunks(chains)
    for i in range(nb):
        of_ref[i] = outs[2 * i]
        ob_ref[i] = outs[2 * i + 1]


def _scan_chunk_maps(ctx_chunks, n_chunks):
    fwd = lambda s: s
    bwd = lambda s: jnp.where(s < ctx_chunks, ctx_chunks - 1 - s, n_chunks - 1 - (s - ctx_chunks))
    return fwd, bwd


def _gdn_scan(q, k, v, gates, *, n_ctx):
    b, rows, _ = q.shape
    n_chunks = rows // CHUNK
    fwd, bwd = _scan_chunk_maps(n_ctx // CHUNK, n_chunks)

    nb = GDN_SCAN_SAMPLES if b % GDN_SCAN_SAMPLES == 0 else 1

    def specs(cm):
        tok = lambda w: pl.BlockSpec((nb, CHUNK, w), lambda b_, s: (b_, cm(s), 0))
        return [tok(GDN_QK), tok(GDN_QK), tok(GDN_VW), tok(LANES)]

    return pl.pallas_call(
        _gdn_scan_kernel,
        grid=(b // nb, n_chunks),
        in_specs=specs(fwd) + specs(bwd),
        out_specs=[pl.BlockSpec((nb, CHUNK, GDN_VW), lambda b_, s: (b_, fwd(s), 0)),
                   pl.BlockSpec((nb, CHUNK, GDN_VW), lambda b_, s: (b_, bwd(s), 0))],
        out_shape=[jax.ShapeDtypeStruct((b, rows, GDN_VW), F32)] * 2,
        scratch_shapes=[pltpu.VMEM((nb, GDN_TILES, GDN_HALF, GDN_HALF), F32)] * 2,
        compiler_params=_params(),
        name="gdn_scan",
    )(q, k, v, gates, q, k, v, gates)


def _outproj_ffn_kernel(x_ref, mod_ref, oa_ref, of_ref, ob_ref, gate_ref, gn_ref, w_ref, g_ref, wg_ref, wu_ref, wd_ref,
                        o_ref, *, seg, rec_first):
    m = mod_ref[...]
    o = of_ref[...] + ob_ref[...]
    y = o * lax.rsqrt(_seg_mean_sq(o, seg, float(seg)) + EPS) * gn_ref[...] * _silu(gate_ref[...])
    wr = y.shape[-1]
    wa = oa_ref.shape[-1]
    if rec_first:
        out = jnp.dot(y.astype(BF16), w_ref[0:wr, :], preferred_element_type=F32)
        out = out + jnp.dot(oa_ref[...], w_ref[wr:wr + wa, :], preferred_element_type=F32)
    else:
        out = jnp.dot(oa_ref[...], w_ref[0:wa, :], preferred_element_type=F32)
        out = out + jnp.dot(y.astype(BF16), w_ref[wa:wa + wr, :], preferred_element_type=F32)
    x1 = x_ref[...] + m[5:6] * out
    hb = _rms_rows(x1, g_ref[...]) * (1.0 + m[7:8]) + m[6:7]
    o_ref[...] = x1 + (FFN_RES * m[8:9]) * _swiglu(hb, wg_ref, wu_ref, wd_ref)


def _outproj_ffn(xs, mod, o_att, o_f, o_b, gate, gn, w_out, g, wg, wu, wd, *, layer, seg, rec_first, ctx_tiles,
                 skip=0, att_skip=0):
    b, rows, d = xs.shape
    out_rows = rows - skip * TM
    wspec = lambda w: pl.BlockSpec((None, None) + w.shape[2:], lambda b_, t: (layer, 1, 0, 0),
                                   pipeline_mode=pl.Buffered(1))
    return pl.pallas_call(
        functools.partial(_outproj_ffn_kernel, seg=seg, rec_first=rec_first),
        grid=(b, out_rows // TM),
        in_specs=[_tok_spec(d, skip), _mod_spec(ctx_tiles, skip), _tok_spec(o_att.shape[-1], att_skip),
                  _tok_spec(o_f.shape[-1], skip), _tok_spec(o_b.shape[-1], skip), _tok_spec(gate.shape[-1], skip),
                  _const_spec(gn.shape), _const_spec(w_out.shape), _const_spec((1, d)),
                  wspec(wg), wspec(wu), wspec(wd)],
        out_specs=_tok_spec(d),
        out_shape=jax.ShapeDtypeStruct((b, out_rows, d), F32),
        compiler_params=_params(),
        name="outproj_ffn",
    )(xs, mod, o_att, o_f, o_b, gate, gn, w_out, g.reshape(1, d), wg, wu, wd)


def _inproj_odd_kernel(x_ref, mod_ref, g_ref, win_ref, w2_ref, b2_ref, qn_ref, kn_ref, rc_ref, rs1_ref, rs2_ref,
                       gq_ref, gk_ref, gv_ref, rg_ref, bc_ref, sq_ref, sk_ref, sv_ref):
    x = x_ref[...]
    m = mod_ref[...]
    h = _rms_rows(x, g_ref[...]) * (1.0 + m[4:5]) + m[3:4]
    z = jnp.dot(h.astype(BF16), win_ref[...], preferred_element_type=F32)
    o = 0
    gq_ref[...] = z[:, o:o + GLA_QK]; o += GLA_QK
    gk_ref[...] = z[:, o:o + GLA_QK]; o += GLA_QK
    gv_ref[...] = z[:, o:o + GLA_VW]; o += GLA_VW
    rg_ref[...] = z[:, o:o + GLA_VW]; o += GLA_VW
    sq = z[:, o:o + SWA_QW]; o += SWA_QW
    sk = z[:, o:o + SWA_KW]; o += SWA_KW
    sv_ref[...] = z[:, o:o + SWA_KW].astype(sv_ref.dtype); o += SWA_KW
    lowrank = z[:, o:o + LANES]

    logit = _dot(lowrank, w2_ref[...]) + b2_ref[...]
    log_a = (jnp.minimum(logit, 0.0) - jnp.log1p(jnp.exp(-jnp.abs(logit)))) * (1.0 / GLA_TAU)
    bc_ref[:, 0:GLA_QK] = _dot_sel(_chunk_tri(TM, False), log_a[:, 0:GLA_QK])
    bc_ref[:, GLA_QK:2 * GLA_QK] = _dot_sel(_chunk_tri(TM, True), log_a[:, GLA_QK:2 * GLA_QK])

    rc, rs1, rs2 = rc_ref[...], rs1_ref[...], rs2_ref[...]
    sqn = sq * lax.rsqrt(_seg_mean_sq(sq, SWA_DH, float(SWA_DH)) + EPS) * (qn_ref[...] * SWA_Q_PRESCALE)
    for s in range(SWA_QW // LANES):
        sl = slice(s * LANES, (s + 1) * LANES)
        sq_ref[:, sl] = _rope(sqn[:, sl], rc, rs1, rs2, SWA_DH // 2).astype(sq_ref.dtype)
    skn = sk * lax.rsqrt(_seg_mean_sq(sk, SWA_DH, float(SWA_DH)) + EPS) * kn_ref[...]
    sk_ref[...] = _rope(skn, rc, rs1, rs2, SWA_DH // 2).astype(sk_ref.dtype)


def _inproj_odd(xs, mod, g, p, rope):
    b, rows, d = xs.shape
    consts = [g.reshape(1, d), p["w_in"], p["w2"], p["b2"], p["q_norm"], p["k_norm"]]
    rope_spec = pl.BlockSpec((TM, LANES), lambda b_, t: (t, 0))
    widths = [(GLA_QK, F32), (GLA_QK, F32), (GLA_VW, F32), (GLA_VW, F32), (2 * GLA_QK, F32),
              (SWA_QW, BF16), (SWA_KW, BF16), (SWA_KW, BF16)]
    return pl.pallas_call(
        _inproj_odd_kernel,
        grid=(b, rows // TM),
        in_specs=[_tok_spec(d), _mod_spec(p["ctx_tiles"])] + [_const_spec(c.shape) for c in consts] + [rope_spec] * 3,
        out_specs=[_tok_spec(w) for w, _ in widths],
        out_shape=[jax.ShapeDtypeStruct((b, rows, w), dt) for w, dt in widths],
        compiler_params=_params(),
        name="inproj_odd",
    )(xs, mod, *consts, *rope)


def _gla_chunk(q, k, v, bc, st_ref, reverse):
    def blk(shape, rdiv, cdiv):
        return (lax.broadcasted_iota(jnp.int32, shape, 0) // rdiv) == (lax.broadcasted_iota(jnp.int32, shape, 1) // cdiv)

    ii = lax.broadcasted_iota(jnp.int32, (CHUNK, GLA_QK), 0)
    jj = lax.broadcasted_iota(jnp.int32, (CHUNK, GLA_QK), 1) % CHUNK
    incl = (ii <= jj) if reverse else (ii >= jj)
    last = 0 if reverse else CHUNK - 1
    q_dec = (q * (GLA_DK ** -0.5)) * jnp.exp(bc)
    k_inv = k * jnp.exp(-bc)
    b_last = bc[last:last + 1, :]
    k_dec = k * jnp.exp(b_last - bc)
    e_last = jnp.exp(b_last)
    kt = jnp.concatenate([k_inv.T.astype(BF16)] * GLA_HEADS, axis=1)
    rk = jnp.where(blk((GLA_QK, GLA_QK), GLA_DK, CHUNK), kt, jnp.zeros_like(kt))
    attn = jnp.where(incl, jnp.dot(q_dec.astype(BF16), rk, preferred_element_type=F32), 0.0)
    vt = jnp.concatenate([v.astype(BF16)] * GLA_HEADS, axis=0)
    vbd = jnp.where(blk((GLA_QK, GLA_VW), CHUNK, GLA_DV), vt, jnp.zeros_like(vt))
    st = st_ref[...]
    o = jnp.dot(attn.astype(BF16), vbd, preferred_element_type=F32) + _dot_nt(q_dec, st)
    st_ref[...] = st * e_last + jnp.where(blk((GLA_VW, GLA_QK), GLA_DV, GLA_DK), _dot_tn(v, k_dec), 0.0)
    return o


def _gla_scan_kernel(qf_ref, kf_ref, vf_ref, bf_ref, qb_ref, kb_ref, vb_ref, bb_ref, of_ref, ob_ref, sf_ref, sb_ref):
    @pl.when(pl.program_id(1) == 0)
    def _():
        sf_ref[...] = jnp.zeros_like(sf_ref)
        sb_ref[...] = jnp.zeros_like(sb_ref)

    for i in range(qf_ref.shape[0]):
        of_ref[i] = _gla_chunk(qf_ref[i], kf_ref[i], vf_ref[i], bf_ref[i][:, 0:GLA_QK], sf_ref.at[i], False)
        ob_ref[i] = _gla_chunk(qb_ref[i], kb_ref[i], vb_ref[i], bb_ref[i][:, GLA_QK:2 * GLA_QK], sb_ref.at[i], True)


def _gla_scan(q, k, v, bc, *, n_ctx):
    b, rows, _ = q.shape
    n_chunks = rows // CHUNK
    fwd, bwd = _scan_chunk_maps(n_ctx // CHUNK, n_chunks)
    nb = GLA_SCAN_SAMPLES if b % GLA_SCAN_SAMPLES == 0 else 1

    def specs(cm):
        tok = lambda w: pl.BlockSpec((nb, CHUNK, w), lambda b_, s: (b_, cm(s), 0))
        return [tok(GLA_QK), tok(GLA_QK), tok(GLA_VW), tok(2 * GLA_QK)]

    return pl.pallas_call(
        _gla_scan_kernel,
        grid=(b // nb, n_chunks),
        in_specs=specs(fwd) + specs(bwd),
        out_specs=[pl.BlockSpec((nb, CHUNK, GLA_VW), lambda b_, s: (b_, fwd(s), 0)),
                   pl.BlockSpec((nb, CHUNK, GLA_VW), lambda b_, s: (b_, bwd(s), 0))],
        out_shape=[jax.ShapeDtypeStruct((b, rows, GLA_VW), F32)] * 2,
        scratch_shapes=[pltpu.VMEM((nb, GLA_VW, GLA_QK), F32)] * 2,
        compiler_params=_params(),
        name="gla_scan",
    )(q, k, v, bc, q, k, v, bc)


def _swa_kernel(q_ref, k_ref, v_ref, sink_ref, o_ref, *, n_ctx, n_lat):
    w = SWA_WINDOW
    n = pl.program_id(1)
    nb = n_lat // w

    def rows(ref, blk):
        return ref[pl.ds(pl.multiple_of(n_ctx + blk * w, w), w), :]

    pb = jnp.maximum(n - 1, 0)
    xb = jnp.minimum(n + 1, nb - 1)
    k_cat = jnp.concatenate([rows(k_ref, pb), rows(k_ref, n), rows(k_ref, xb), k_ref[0:n_ctx, :]], axis=0)
    v_cat = jnp.concatenate([rows(v_ref, pb), rows(v_ref, n), rows(v_ref, xb), v_ref[0:n_ctx, :]], axis=0)
    v_one = jnp.concatenate([v_cat, jnp.ones_like(v_cat)], axis=1)
    half = SWA_HEADS // 2
    ii = lax.broadcasted_iota(jnp.int32, (half * w, w), 0) % w
    jj = lax.broadcasted_iota(jnp.int32, (half * w, w), 1)
    ninf = -jnp.inf
    bias_prev = jnp.where((jj >= ii) & (n > 0), 0.0, ninf)
    bias_next = jnp.where((jj <= ii) & (n < nb - 1), 0.0, ninf)
    lo = lax.broadcasted_iota(jnp.int32, (half * w, LANES), 1) < SWA_DH
    q = q_ref[...]
    qst = jnp.concatenate([q[:, j * LANES:(j + 1) * LANES] for j in range(half)], axis=0)
    sink = sink_ref[...]
    dims = (((1,), (1,)), ((), ()))
    qm = [jnp.where(lo if g == 0 else jnp.logical_not(lo), qst, jnp.zeros_like(qst)) for g in range(SWA_KV_HEADS)]
    s = [lax.dot_general(x, k_cat, dims, preferred_element_type=F32) for x in qm]
    parts = [[x[:, 0:w] + bias_prev, x[:, w:2 * w], x[:, 2 * w:3 * w] + bias_next, x[:, 3 * w:]] for x in s]
    sk = [jnp.concatenate([jnp.broadcast_to(sink[0:1, half * g + j:half * g + j + 1], (w, 1)) for j in range(half)],
                          axis=0) for g in range(SWA_KV_HEADS)]
    def row_max(ps, k_):
        slabs = [x[:, c * LANES:(c + 1) * LANES] for x in ps for c in range(x.shape[1] // LANES)]
        return jnp.maximum(jnp.max(functools.reduce(jnp.maximum, slabs), axis=-1, keepdims=True), k_)

    mx = [row_max(ps, k_) for ps, k_ in zip(parts, sk)]
    p = [jnp.concatenate([jnp.exp2(x - m).astype(BF16) for x in ps], axis=1) for ps, m in zip(parts, mx)]
    r = [jnp.dot(x, v_one, preferred_element_type=F32) for x in p]
    res = [x[:, 0:LANES] / (x[:, LANES:2 * LANES] + jnp.exp2(k_ - m)) for x, k_, m in zip(r, sk, mx)]
    o = jnp.where(lo, res[0], res[1]).astype(o_ref.dtype)
    for j in range(half):
        o_ref[:, j * LANES:(j + 1) * LANES] = o[j * w:(j + 1) * w, :]


def _swa_attention(q, k, v, sink, *, n_ctx):
    b, rows, _ = q.shape
    n_lat = rows - n_ctx
    w = SWA_WINDOW
    skip = n_ctx // w
    return pl.pallas_call(
        functools.partial(_swa_kernel, n_ctx=n_ctx, n_lat=n_lat),
        grid=(b, n_lat // w),
        in_specs=[
            pl.BlockSpec((None, w, SWA_QW), lambda b_, n: (b_, n + skip, 0)),
            pl.BlockSpec((None, rows, SWA_KW), lambda b_, n: (b_, 0, 0)),
            pl.BlockSpec((None, rows, SWA_KW), lambda b_, n: (b_, 0, 0)),
            _const_spec((1, LANES)),
        ],
        out_specs=pl.BlockSpec((None, w, SWA_QW), lambda b_, n: (b_, n, 0)),
        out_shape=jax.ShapeDtypeStruct((b, n_lat, SWA_QW), BF16),
        compiler_params=_params(),
        name="swa_attention",
    )(q, k, v, sink)


def _rope_tables(n_lat, n_ctx, rot_dim):
    t = jnp.arange(n_lat)
    row = (t // GRID_W).astype(F32)
    col = (t % GRID_W).astype(F32)
    n_freq = rot_dim // 4
    inv = ROPE_THETA ** (-jnp.arange(n_freq, dtype=F32) / n_freq)
    ang = jnp.concatenate([row[:, None] * inv, col[:, None] * inv], axis=-1)
    half = rot_dim // 2
    cos = jnp.concatenate([jnp.ones((n_ctx, half), F32), jnp.cos(ang)], axis=0)
    sin = jnp.concatenate([jnp.zeros((n_ctx, half), F32), jnp.sin(ang)], axis=0)
    rows = n_ctx + n_lat
    one = lambda w: jnp.ones((rows, w), F32)
    zero = lambda w: jnp.zeros((rows, w), F32)
    if rot_dim == MLA_ROPE:
        c = jnp.concatenate([one(MLA_NOPE), cos, cos, one(HEAD_PAD - MLA_QK)], axis=1)
        s1 = jnp.concatenate([zero(MLA_NOPE), -sin, zero(half), zero(HEAD_PAD - MLA_QK)], axis=1)
        s2 = jnp.concatenate([zero(MLA_NOPE), zero(half), sin, zero(HEAD_PAD - MLA_QK)], axis=1)
    else:
        c = jnp.concatenate([cos, cos, cos, cos], axis=1)
        s1 = jnp.concatenate([-sin, zero(half), -sin, zero(half)], axis=1)
        s2 = jnp.concatenate([zero(half), sin, zero(half), sin], axis=1)
    return c, s1, s2


def _even_params(j, n_ctx, ev_w_in, ev_q_a_norm, ev_w_q_up, ev_kv_a_norm, ev_w_kv_up, ev_mla_q_norm, ev_mla_k_norm,
                 ev_gdn_conv, ev_gdn_a_log, ev_gdn_dt_bias, ev_gdn_out_norm, ev_w_out):
    w = ev_w_in[j]
    d = w.shape[0]
    z = lambda n: jnp.zeros((d, n), F32)
    o_kr = MLA_Q_RANK + MLA_KV_RANK
    o_g = o_kr + MLA_ROPE
    o_small = o_g + 3 * GDN_QK
    o_gate = o_small + 4 * GDN_HEADS
    w_in = jnp.concatenate([
        w[:, :o_kr], z(MLA_NOPE), w[:, o_kr:o_g], z(HEAD_PAD - MLA_QK),
        w[:, o_g:o_small], w[:, o_gate:o_gate + GDN_VW],
        w[:, o_small:o_gate], z(LANES - 4 * GDN_HEADS)], axis=1).astype(BF16)
    pad_h = HEAD_PAD - MLA_QK
    w_q = jnp.pad(ev_w_q_up[j].reshape(MLA_Q_RANK, MLA_HEADS, MLA_QK), ((0, 0), (0, 0), (0, pad_h)))
    wkv = ev_w_kv_up[j].reshape(MLA_KV_RANK, MLA_HEADS, MLA_NOPE + MLA_V)
    w_kk = jnp.pad(wkv[:, :, :MLA_NOPE], ((0, 0), (0, 0), (0, HEAD_PAD - MLA_NOPE)))
    lane_row = lambda vec: jnp.pad(vec, (0, LANES - vec.shape[0])).reshape(1, LANES)
    fb = lambda a: jnp.concatenate([a[0], a[1]])
    return {
        "ctx_tiles": n_ctx // TM,
        "w_in": w_in,
        "q_a_norm": ev_q_a_norm[j].reshape(1, -1),
        "w_q": w_q.reshape(MLA_Q_RANK, MLA_HEADS * HEAD_PAD).astype(BF16),
        "kv_a_norm": ev_kv_a_norm[j].reshape(1, -1),
        "w_kk": w_kk.reshape(MLA_KV_RANK, MLA_HEADS * HEAD_PAD).astype(BF16),
        "w_kv": jnp.pad(wkv[:, :, MLA_NOPE:], ((0, 0), (0, 0), (0, HEAD_PAD - MLA_V))
                        ).reshape(MLA_KV_RANK, MLA_HEADS * HEAD_PAD).astype(BF16),
        "q_norm": lane_row(ev_mla_q_norm[j]),
        "k_norm": lane_row(ev_mla_k_norm[j]),
        "conv_w": ev_gdn_conv[j],
        "neg_a": lane_row(-jnp.exp(fb(ev_gdn_a_log[j]))),
        "dt_bias": lane_row(fb(ev_gdn_dt_bias[j])),
        "out_norm": jnp.tile(ev_gdn_out_norm[j], GDN_HEADS).reshape(1, GDN_VW),
        "w_out": ev_w_out[j].astype(BF16),
    }


def _swa_head_perm():
    half = SWA_HEADS // 2
    heads = [h for j in range(half) for h in (j, half + j)]
    return jnp.concatenate([jnp.arange(SWA_DH) + h * SWA_DH for h in heads])


def _odd_params(j, n_ctx, od_w_in, od_gla_gate_w2, od_gla_gate_b, od_gla_out_norm, od_swa_q_norm, od_swa_k_norm,
                od_swa_sink, od_w_out):
    w = od_w_in[j]
    d = w.shape[0]
    o_gate = 2 * GLA_QK + GLA_VW
    o_rg = o_gate + 2 * GLA_RANK
    o_sq = o_rg + GLA_VW
    o_sk = o_sq + SWA_QW
    perm = _swa_head_perm()
    w_in = jnp.concatenate([
        w[:, :o_gate], w[:, o_rg:o_sq], w[:, o_sq:o_sk][:, perm], w[:, o_sk:],
        w[:, o_gate:o_rg], jnp.zeros((d, LANES - 2 * GLA_RANK), F32)], axis=1).astype(BF16)
    w2 = jnp.zeros((LANES, 2 * GLA_QK), F32)
    w2 = w2.at[0:GLA_RANK, 0:GLA_QK].set(od_gla_gate_w2[j, 0])
    w2 = w2.at[GLA_RANK:2 * GLA_RANK, GLA_QK:].set(od_gla_gate_w2[j, 1])
    wo = od_w_out[j]
    w_out = jnp.concatenate([wo[:GLA_VW], wo[GLA_VW:][perm]], axis=0).astype(BF16)
    return {
        "ctx_tiles": n_ctx // TM,
        "w_in": w_in,
        "w2": w2.astype(BF16),
        "b2": jnp.concatenate([od_gla_gate_b[j, 0], od_gla_gate_b[j, 1]]).reshape(1, 2 * GLA_QK),
        "q_norm": jnp.tile(od_swa_q_norm[j], SWA_HEADS).reshape(1, SWA_QW),
        "k_norm": jnp.tile(od_swa_k_norm[j], SWA_KV_HEADS).reshape(1, SWA_KW),
        "sink": jnp.pad(od_swa_sink[j] * math.log2(math.e), (0, LANES - SWA_HEADS)).reshape(1, LANES),
        "out_norm": jnp.tile(od_gla_out_norm[j], GLA_HEADS).reshape(1, GLA_VW),
        "w_out": w_out,
    }


def kernel(x, c, ctx, c_ctx, ada_w, ada_b, norm_g, ffn_w_gate, ffn_w_up, ffn_w_down, ev_w_in, ev_q_a_norm, ev_w_q_up, ev_kv_a_norm, ev_w_kv_up, ev_mla_q_norm, ev_mla_k_norm, ev_gdn_conv, ev_gdn_a_log, ev_gdn_dt_bias, ev_gdn_out_norm, ev_w_out, od_w_in, od_gla_gate_w2, od_gla_gate_b, od_gla_out_norm, od_swa_q_norm, od_swa_k_norm, od_swa_sink, od_w_out):
    b, n_lat, d = x.shape
    n_ctx = ctx.shape[1]
    depth = ada_w.shape[0]
    assert d == D_MODEL and n_ctx % TM == 0 and n_lat % TM == 0 and n_lat % GRID_W == 0
    assert depth % 2 == 0, "the last layer must be an odd (GLA/SWA) layer: context outputs of that mixer are not built"
    ctx_tiles = n_ctx // TM

    cvec = jnp.concatenate([c, c_ctx[None, :], jnp.zeros((16 - b - 1, d), F32)], axis=0)
    mod_all = _ada(cvec, ada_w, ada_b).reshape(depth, 16, N_MOD, d)
    rope_mla = _rope_tables(n_lat, n_ctx, MLA_ROPE)
    rope_swa = _rope_tables(n_lat, n_ctx, SWA_DH)

    wg, wu, wd = ffn_w_gate, ffn_w_up, ffn_w_down
    xs = x
    for i in range(depth):
        last = i == depth - 1
        mod = jnp.stack([jnp.broadcast_to(mod_all[i, b][None], (b, N_MOD, d)), mod_all[i, :b]], axis=1)
        xs = _ffn(xs, mod, norm_g[i, 0], wg, wu, wd, layer=i, half=0, r0=0, n_ctx=n_ctx, ctx=ctx if i == 0 else None)
        j = i // 2
        skip = ctx_tiles if last else 0
        if i % 2 == 0:
            p = _even_params(j, n_ctx, ev_w_in, ev_q_a_norm, ev_w_q_up, ev_kv_a_norm, ev_w_kv_up, ev_mla_q_norm,
                             ev_mla_k_norm, ev_gdn_conv, ev_gdn_a_log, ev_gdn_dt_bias, ev_gdn_out_norm, ev_w_out)
            q, k, v, zg, gate, small = _inproj_even(xs, mod, norm_g[i, 1], p, rope_mla)
            o_att = _mla_attention(q, k, v, n_ctx=n_ctx)
            gq, gk, gv, gates = _gdn_prep(zg, small, p["conv_w"], p["neg_a"], p["dt_bias"], ctx_tiles=ctx_tiles)
            o_f, o_b = _gdn_scan(gq, gk, gv, gates, n_ctx=n_ctx)
            mix = dict(o_att=o_att, o_f=o_f, o_b=o_b, gate=gate, seg=GDN_DV, rec_first=False, att_skip=skip)
        else:
            assert last, "odd layers that must also produce context outputs are not built"
            p = _odd_params(j, n_ctx, od_w_in, od_gla_gate_w2, od_gla_gate_b, od_gla_out_norm, od_swa_q_norm,
                            od_swa_k_norm, od_swa_sink, od_w_out)
            gq, gk, gv, rg, bc, sq, sk, sv = _inproj_odd(xs, mod, norm_g[i, 1], p, rope_swa)
            o_f, o_b = _gla_scan(gq, gk, gv, bc, n_ctx=n_ctx)
            o_att = _swa_attention(sq, sk, sv, p["sink"], n_ctx=n_ctx)
            mix = dict(o_att=o_att, o_f=o_f, o_b=o_b, gate=rg, seg=GLA_DV, rec_first=True, att_skip=0)
        xs = _outproj_ffn(xs, mod, mix["o_att"], mix["o_f"], mix["o_b"], mix["gate"], p["out_norm"], p["w_out"],
                          norm_g[i, 2], wg, wu, wd, layer=i, seg=mix["seg"], rec_first=mix["rec_first"],
                          ctx_tiles=ctx_tiles, skip=skip, att_skip=mix["att_skip"])
    return xs
```

```python
import functools
import math

import jax
import jax.numpy as jnp
from jax import lax
from jax.experimental import pallas as pl
from jax.experimental.pallas import tpu as pltpu

F32 = jnp.float32
BF16 = jnp.bfloat16

D_MODEL = 1024
GRID_W = 64
D_FF = 2816
FFN_RES = 0.5
N_MOD = 9
EPS = 1e-6
ROPE_THETA = 10000.0
CHUNK = 64

MLA_HEADS = 8
MLA_NOPE = 64
MLA_ROPE = 32
MLA_QK = MLA_NOPE + MLA_ROPE
MLA_V = 64
MLA_Q_RANK = 384
MLA_KV_RANK = 256
MLA_VW = MLA_HEADS * MLA_V
MLA_STEP_HEADS = 4
MLA_Q_PRESCALE =MLA_QK ** -0.5 * math.log2(math.e)

GDN_HEADS = 8
GDN_DK = 64
GDN_DV = 64
GDN_CONV = 5
GDN_QK = GDN_HEADS * GDN_DK
GDN_VW = GDN_HEADS * GDN_DV
GDN_HALF = 128
GDN_TILES = GDN_QK // GDN_HALF
GDN_SCAN_SAMPLES = 8

GLA_HEADS = 4
GLA_DK = 64
GLA_DV = 128
GLA_RANK = 16
GLA_TAU = 16.0
GLA_QK = GLA_HEADS * GLA_DK
GLA_VW = GLA_HEADS * GLA_DV
GLA_SCAN_SAMPLES = 8

SWA_HEADS = 8
SWA_KV_HEADS = 2
SWA_DH = 64
SWA_WINDOW = 128
SWA_QW = SWA_HEADS * SWA_DH
SWA_KW = SWA_KV_HEADS * SWA_DH
SWA_Q_PRESCALE = SWA_DH ** -0.5 * math.log2(math.e)

LANES = 128
HEAD_PAD = 128
TM = 256
FFN_TILES = (576, 512, 256, 128)
FFN_SPLIT = 1536
VMEM_LIMIT =56 * 1024 * 1024

EVEN_COLS = MLA_Q_RANK + MLA_KV_RANK + LANES + 2 * GDN_QK + GDN_VW + GDN_VW + LANES
ODD_COLS = 2 * GLA_QK + GLA_VW + GLA_VW + SWA_QW + 2 * SWA_KW + LANES


def _dot(a, b):
    return jnp.dot(a.astype(BF16), b.astype(BF16), preferred_element_type=F32)


def _dot_nt(a, b):
    return lax.dot_general(a.astype(BF16), b.astype(BF16), (((1,), (1,)), ((), ())), preferred_element_type=F32)


def _dot_tn(a, b):
    return lax.dot_general(a.astype(BF16), b.astype(BF16), (((0,), (0,)), ((), ())), preferred_element_type=F32)


def _split2(x):
    hi = x.astype(BF16)
    lo = (x - hi.astype(F32)).astype(BF16)
    return hi, lo


def _split3(x):
    hi = x.astype(BF16)
    r = x - hi.astype(F32)
    mid = r.astype(BF16)
    lo = (r - mid.astype(F32)).astype(BF16)
    return hi, mid, lo


def _dot3(a, b):
    ah, al = _split2(a)
    bh, bl = _split2(b)
    d = functools.partial(jnp.dot, preferred_element_type=F32)
    return d(ah, bh) + (d(ah, bl) + d(al, bh))


def _dot_sel(sel, x):
    s = sel.astype(BF16)
    hi, mid, lo = _split3(x)
    d = functools.partial(jnp.dot, preferred_element_type=F32)
    return d(s, hi) + (d(s, mid) + d(s, lo))


def _dot_sel_r(x, sel):
    s = sel.astype(BF16)
    hi, mid, lo = _split3(x)
    d = functools.partial(jnp.dot, preferred_element_type=F32)
    return d(hi, s) + (d(mid, s) + d(lo, s))


def _seg_mean_sq(x, seg, n_real):
    w = x.shape[-1]
    r = lax.broadcasted_iota(jnp.int32, (w, w), 0) // seg
    c = lax.broadcasted_iota(jnp.int32, (w, w), 1) // seg
    ones_bd = (r == c).astype(BF16)
    hi, lo = _split2(x * x)
    d = functools.partial(jnp.dot, preferred_element_type=F32)
    return (d(hi, ones_bd) + d(lo, ones_bd)) * (1.0 / n_real)


def _rms_rows(x, g):
    ms = jnp.mean(x * x, axis=-1, keepdims=True)
    return x * lax.rsqrt(ms + EPS) * g


def _sigmoid(x):
    return 1.0 / (1.0 + jnp.exp(-x))


def _silu(x):
    return x * _sigmoid(x)


def _softplus(x):
    return jnp.maximum(x, 0.0) + jnp.log1p(jnp.exp(-jnp.abs(x)))


def _rope(x, c, s1, s2, half):
    return x * c + pltpu.roll(x, LANES - half, 1) * s1 + pltpu.roll(x, half, 1) * s2


def _chunk_tri(n, reverse):
    i = lax.broadcasted_iota(jnp.int32, (n, n), 0)
    j = lax.broadcasted_iota(jnp.int32, (n, n), 1)
    same = (i // CHUNK) == (j // CHUNK)
    tri = (j >= i) if reverse else (j <= i)
    return (same & tri).astype(F32)


def _ada_kernel(c_ref, w_ref, b_ref, o_ref):
    sc = _silu(c_ref[...])
    o_ref[...] = _dot(sc, w_ref[...]) + b_ref[...]


def _ada(cvec, ada_w, ada_b):
    depth, d, nm = ada_w.shape
    tn = 1024
    return pl.pallas_call(
        _ada_kernel,
        grid=(depth, nm // tn),
        in_specs=[
            pl.BlockSpec(cvec.shape, lambda i, j: (0, 0)),
            pl.BlockSpec((None, d, tn), lambda i, j: (i, 0, j)),
            pl.BlockSpec((None, 1, tn), lambda i, j: (i, 0, j)),
        ],
        out_specs=pl.BlockSpec((None, cvec.shape[0], tn), lambda i, j: (i, 0, j)),
        out_shape=jax.ShapeDtypeStruct((depth, cvec.shape[0], nm), F32),
        compiler_params=pltpu.CompilerParams(vmem_limit_bytes=VMEM_LIMIT),
        name="ada_mod",
    )(cvec, ada_w, ada_b.reshape(depth, 1, nm))


def _const_spec(shape):
    nd = len(shape)
    return pl.BlockSpec(shape, lambda *_: (0,) * nd, pipeline_mode=pl.Buffered(1))


def _tok_spec(width):
    return pl.BlockSpec((None, TM, width), lambda b, t: (b, t, 0))


def _mod_spec(ctx_tiles):
    return pl.BlockSpec((None, None, N_MOD, D_MODEL), lambda b, t: (b, jnp.where(t >= ctx_tiles, 1, 0), 0, 0))


def _params():
    return pltpu.CompilerParams(dimension_semantics=("parallel", "arbitrary"), vmem_limit_bytes=VMEM_LIMIT)


def _swiglu(hb, wg_ref, wu_ref, wd_ref):
    f = wg_ref.shape[1]
    y = None
    for lo, hi in ((0, FFN_SPLIT), (FFN_SPLIT, f)):
        a = jnp.dot(hb, wg_ref[:, lo:hi], preferred_element_type=F32)
        u = jnp.dot(hb, wu_ref[:, lo:hi], preferred_element_type=F32)
        part = jnp.dot(_silu(a) * u, wd_ref[lo:hi, :], preferred_element_type=F32)
        y = part if y is None else y + part
    return y


def _ffn_kernel(*refs, r0, n_ctx, split):
    if split:
        c_ref, x_ref, mod_ref, g_ref, wg_ref, wu_ref, wd_ref, o_ref = refs
    else:
        x_ref, mod_ref, g_ref, wg_ref, wu_ref, wd_ref, o_ref = refs
    x = x_ref[...]
    m = mod_ref[...]
    tm = x.shape[0]
    if n_ctx:
        is_ctx = pl.program_id(1) * tm + lax.broadcasted_iota(jnp.int32, (tm, 1), 0) < n_ctx
        row = lambda r: jnp.where(is_ctx, m[0, r:r + 1], m[1, r:r + 1])
        if split:
            x = jnp.where(is_ctx, c_ref[...], x)
    else:
        row = lambda r: m[1, r:r + 1]
    h = _rms_rows(x, g_ref[...]) * (1.0 + row(r0 + 1)) + row(r0)
    y = _swiglu(h, wg_ref, wu_ref, wd_ref)
    o_ref[...] = x + (FFN_RES * row(r0 + 2)) * y


def _ffn(xs, mod, g, wg, wu, wd, *, layer, half, r0, n_ctx, ctx=None):
    b, rows, d = xs.shape
    split = ctx is not None
    if split:
        tm = TM
        ctx_tiles = n_ctx // tm
        rows += n_ctx
        toks = [pl.BlockSpec((None, tm, d), lambda b_, t: (b_, jnp.minimum(t, ctx_tiles - 1), 0)),
                pl.BlockSpec((None, tm, d), lambda b_, t: (b_, jnp.maximum(t - ctx_tiles, 0), 0))]
        data = [ctx, xs]
    else:
        tm = next(t for t in FFN_TILES if rows % t == 0)
        toks = [pl.BlockSpec((None, tm, d), lambda b_, t: (b_, t, 0))]
        data = [xs]
    wspec = lambda w: pl.BlockSpec((None, None) + w.shape[2:], lambda b_, t: (layer, half, 0, 0),
                                   pipeline_mode=pl.Buffered(1))
    return pl.pallas_call(
        functools.partial(_ffn_kernel, r0=r0, n_ctx=n_ctx, split=split),
        grid=(b, rows // tm),
        in_specs=toks + [pl.BlockSpec((None, 2, N_MOD, d), lambda b_, t: (b_, 0, 0, 0)), _const_spec((1, d)),
                         wspec(wg), wspec(wu), wspec(wd)],
        out_specs=pl.BlockSpec((None, tm, d), lambda b_, t: (b_, t, 0)),
        out_shape=jax.ShapeDtypeStruct((b, rows, d), F32),
        compiler_params=_params(),
        name="ffn_half_step",
    )(*data, mod, g.reshape(1, d), wg, wu, wd)


def _inproj_even_kernel(x_ref, mod_ref, g_ref, win_ref, qan_ref, wq_ref, kvan_ref, wkk_ref, wkv_ref,
                        qn_ref, kn_ref, rc_ref, rs1_ref, rs2_ref,
                        q_ref, k_ref, v_ref, zg_ref, gate_ref, small_ref):
    x = x_ref[...]
    m = mod_ref[...]
    h = _rms_rows(x, g_ref[...]) * (1.0 + m[4:5]) + m[3:4]
    hb = h.astype(BF16)
    n_mla = MLA_Q_RANK + MLA_KV_RANK + LANES
    z = jnp.dot(hb, win_ref[:, 0:n_mla], preferred_element_type=F32)
    cq = z[:, 0:MLA_Q_RANK]
    ckv = z[:, MLA_Q_RANK:MLA_Q_RANK + MLA_KV_RANK]
    kr = z[:, MLA_Q_RANK + MLA_KV_RANK:n_mla]

    rc, rs1, rs2 = rc_ref[...], rs1_ref[...], rs2_ref[...]
    qn, kn = qn_ref[...], kn_ref[...]
    qf = jnp.dot(_rms_rows(cq, qan_ref[...]).astype(BF16), wq_ref[...], preferred_element_type=F32)
    ckvn = _rms_rows(ckv, kvan_ref[...]).astype(BF16)
    kf = jnp.dot(ckvn, wkk_ref[...], preferred_element_type=F32)
    lane = lax.broadcasted_iota(jnp.int32, (1, MLA_HEADS * HEAD_PAD), 1)
    ones_cols = (lane % HEAD_PAD >= MLA_V).astype(F32)
    v_ref[...] = (jnp.dot(ckvn, wkv_ref[...], preferred_element_type=F32) + ones_cols).astype(v_ref.dtype)
    slabs = [slice(hd * HEAD_PAD, (hd + 1) * HEAD_PAD) for hd in range(MLA_HEADS)]
    xs = [qf[:, sl] for sl in slabs] + [kf[:, sl] + kr for sl in slabs]
    gains = [qn * MLA_Q_PRESCALE] * MLA_HEADS + [kn] * MLA_HEADS
    def gdn_cols(lo, hi):
        return jnp.dot(hb, win_ref[:, n_mla + lo:n_mla + hi], preferred_element_type=F32)

    ss = [jnp.sum(x * x, axis=-1, keepdims=True) for x in xs]
    zg_ref[:, 0:GDN_QK] = gdn_cols(0, GDN_QK)
    xs = [x * lax.rsqrt(s * (1.0 / MLA_QK) + EPS) * g for x, s, g in zip(xs, ss, gains)]
    zg_ref[:, GDN_QK:2 * GDN_QK] = gdn_cols(GDN_QK, 2 * GDN_QK)
    xs = [_rope(x, rc, rs1, rs2, MLA_ROPE // 2) for x in xs]
    zg_ref[:, 2 * GDN_QK:3 * GDN_QK] = gdn_cols(2 * GDN_QK, 3 * GDN_QK)
    for hd, sl in enumerate(slabs):
        q_ref[:, sl] = xs[hd].astype(q_ref.dtype)
        k_ref[:, sl] = xs[MLA_HEADS + hd].astype(k_ref.dtype)
    zr = gdn_cols(3 * GDN_QK, 3 * GDN_QK + GDN_VW + LANES)
    gate_ref[...] = zr[:, 0:GDN_VW]
    small_ref[...] = zr[:, GDN_VW:]


def _inproj_even(xs, mod, g, p, rope):
    b, rows, d = xs.shape
    ctx_tiles = p["ctx_tiles"]
    consts = [g.reshape(1, d), p["w_in"], p["q_a_norm"], p["w_q"], p["kv_a_norm"], p["w_kk"], p["w_kv"],
              p["q_norm"], p["k_norm"]]
    rope_spec = pl.BlockSpec((TM, LANES), lambda b_, t: (t, 0))
    widths = [(MLA_HEADS * HEAD_PAD, BF16), (MLA_HEADS * HEAD_PAD, BF16), (MLA_HEADS * HEAD_PAD, BF16),
              (3 * GDN_QK, F32), (GDN_VW, F32), (LANES, F32)]
    return pl.pallas_call(
        _inproj_even_kernel,
        grid=(b, rows // TM),
        in_specs=[_tok_spec(d), _mod_spec(ctx_tiles)] + [_const_spec(c.shape) for c in consts] + [rope_spec] * 3,
        out_specs=[_tok_spec(w) for w, _ in widths],
        out_shape=[jax.ShapeDtypeStruct((b, rows, w), dt) for w, dt in widths],
        compiler_params=_params(),
        name="inproj_even",
    )(xs, mod, *consts, *rope)


def _mla_kernel(q_ref, k_ref, v_ref, o_ref, *, n_ctx, n_all, ctx_tiles):
    t = pl.program_id(2)
    lo = lax.broadcasted_iota(jnp.int32, (TM, LANES), 1) < MLA_V
    n_heads = q_ref.shape[-1] // HEAD_PAD

    def attend(nk):
        q = q_ref[...]
        sls = [slice(hh * HEAD_PAD, (hh + 1) * HEAD_PAD) for hh in range(n_heads)]
        s = [lax.dot_general(q[:, sl], k_ref[0:nk, sl], (((1,), (1,)), ((), ())), preferred_element_type=F32)
             for sl in sls]
        mx = [jnp.max(x, axis=-1, keepdims=True) for x in s]
        p = [jnp.exp2(x - m).astype(BF16) for x, m in zip(s, mx)]
        r = [jnp.dot(x, v_ref[0:nk, sl], preferred_element_type=F32) for x, sl in zip(p, sls)]
        outs = [x / x[:, MLA_V:MLA_V + 1] for x in r]
        for pr in range(n_heads // 2):
            o_ref[:, pr * LANES:(pr + 1) * LANES] = jnp.where(
                lo, outs[2 * pr], pltpu.roll(outs[2 * pr + 1], MLA_V, 1)).astype(o_ref.dtype)

    @pl.when(t < ctx_tiles)
    def _():
        attend(n_ctx)

    @pl.when(t >= ctx_tiles)
    def _():
        attend(n_all)


def _mla_attention(q, k, v, *, n_ctx):
    b, rows, _ = q.shape
    hs = MLA_STEP_HEADS
    return pl.pallas_call(
        functools.partial(_mla_kernel, n_ctx=n_ctx, n_all=rows, ctx_tiles=n_ctx // TM),
        grid=(b, MLA_HEADS // hs, rows // TM),
        in_specs=[
            pl.BlockSpec((None, TM, hs * HEAD_PAD), lambda b_, h, t: (b_, t, h)),
            pl.BlockSpec((None, rows, hs * HEAD_PAD), lambda b_, h, t: (b_, 0, h)),
            pl.BlockSpec((None, rows, hs * HEAD_PAD), lambda b_, h, t: (b_, 0, h)),
        ],
        out_specs=pl.BlockSpec((None, TM, hs * MLA_V), lambda b_, h, t: (b_, t, h)),
        out_shape=jax.ShapeDtypeStruct((b, rows, MLA_VW), BF16),
        compiler_params=pltpu.CompilerParams(dimension_semantics=("parallel", "parallel", "arbitrary"),
                                             vmem_limit_bytes=VMEM_LIMIT),
        name="mla_attention",
    )(q, k, v)


def _gdn_prep_kernel(z_ref, zp_ref, zn_ref, cw_ref, sm_ref, nega_ref, dtb_ref,
                     q_ref, k_ref, v_ref, g_ref, ext_ref, *, ctx_tiles, n_tiles):
    t = pl.program_id(1)
    first = (t == 0) | (t == ctx_tiles)
    last = (t == ctx_tiles - 1) | (t == n_tiles - 1)
    half = GDN_CONV // 2
    ext_ref[0:8, :] = jnp.where(first, 0.0, zp_ref[...])
    ext_ref[8:8 + TM, :] = z_ref[...]
    ext_ref[8 + TM:16 + TM, :] = jnp.where(last, 0.0, zn_ref[...])
    cw = cw_ref[...]
    acc = z_ref[...] * cw[half:half + 1]
    for j in range(GDN_CONV):
        if j != half:
            acc = acc + ext_ref[8 - half + j:8 - half + j + TM, :] * cw[j:j + 1]
    qkv = _silu(acc)
    q = qkv[:, :GDN_QK]
    k = qkv[:, GDN_QK:2 * GDN_QK]
    q_ref[...] = q * lax.rsqrt(_seg_mean_sq(q, GDN_DK, 1.0) + EPS) * (GDN_DK ** -0.5)
    k_ref[...] = k * lax.rsqrt(_seg_mean_sq(k, GDN_DK, 1.0) + EPS)
    v_ref[...] = qkv[:, 2 * GDN_QK:]

    sm = sm_ref[...]
    lane = lax.broadcasted_iota(jnp.int32, sm.shape, 1)
    g = nega_ref[...] * _softplus(sm + dtb_ref[...])
    g = jnp.where(lane < 2 * GDN_HEADS, g, 0.0)
    gc_f = _dot_sel(_chunk_tri(TM, False), g)
    gc_b = _dot_sel(_chunk_tri(TM, True), g)
    g_ref[...] = jnp.where(lane < GDN_HEADS, gc_f, jnp.where(lane < 2 * GDN_HEADS, gc_b, _sigmoid(sm)))


def _gdn_prep(zg, small, conv_w, nega, dtb, *, ctx_tiles):
    b, rows, w = zg.shape
    n_tiles = rows // TM
    hb = TM // 8
    n_hblk = rows // 8
    return pl.pallas_call(
        functools.partial(_gdn_prep_kernel, ctx_tiles=ctx_tiles, n_tiles=n_tiles),
        grid=(b, n_tiles),
        in_specs=[
            _tok_spec(w),
            pl.BlockSpec((None, 8, w), lambda b_, t: (b_, jnp.maximum(t * hb - 1, 0), 0)),
            pl.BlockSpec((None, 8, w), lambda b_, t: (b_, jnp.minimum((t + 1) * hb, n_hblk - 1), 0)),
            _const_spec(conv_w.shape), _tok_spec(LANES), _const_spec((1, LANES)), _const_spec((1, LANES)),
        ],
        out_specs=[_tok_spec(GDN_QK), _tok_spec(GDN_QK), _tok_spec(GDN_VW), _tok_spec(LANES)],
        out_shape=[jax.ShapeDtypeStruct((b, rows, GDN_QK), F32), jax.ShapeDtypeStruct((b, rows, GDN_QK), F32),
                   jax.ShapeDtypeStruct((b, rows, GDN_VW), F32), jax.ShapeDtypeStruct((b, rows, LANES), F32)],
        scratch_shapes=[pltpu.VMEM((TM + 16, w), F32)],
        compiler_params=_params(),
        name="gdn_prep",
    )(zg, zg, zg, conv_w, small, nega, dtb)


def _bd_halves(y, bdm):
    yb = y.astype(BF16)
    out = []
    for s in range(GDN_TILES):
        t = jnp.concatenate([yb[:, s * GDN_HALF:(s + 1) * GDN_HALF]] * (GDN_HALF // GDN_DK), axis=0)
        out.append(jnp.where(bdm, t, jnp.zeros_like(t)))
    return out


def _hprod(x, bd):
    xb = x.astype(BF16)
    d = functools.partial(jnp.dot, preferred_element_type=F32)
    return jnp.concatenate([d(xb[:, s * GDN_HALF:(s + 1) * GDN_HALF], bd[s]) for s in range(GDN_TILES)], axis=1)


def _hprod3(x, y, bdm):
    xh, xl = _split2(x)
    yh, yl = _split2(y)
    bh, bl = _bd_halves(yh, bdm), _bd_halves(yl, bdm)
    return _hprod(xh, bh) + (_hprod(xh, bl) + _hprod(xl, bh))


def _gdn_chunks(chains):
    ii = lax.broadcasted_iota(jnp.int32, (CHUNK, GDN_QK), 0)
    jj = lax.broadcasted_iota(jnp.int32, (CHUNK, GDN_QK), 1) % CHUNK
    r = lax.broadcasted_iota(jnp.int32, (GDN_HALF, GDN_HALF), 0) // GDN_DK
    c = lax.broadcasted_iota(jnp.int32, (GDN_HALF, GDN_HALF), 1) // GDN_DK
    bdm = r == c
    diag = ii == jj
    eye = diag.astype(F32)
    n = len(chains)
    qs, ks, vs, gs, srefs, revs = (list(t) for t in zip(*chains))
    each = lambda f, *ls: [f(*a) for a in zip(*ls)]
    bd = lambda ys: [_bd_halves(y, bdm) for y in ys]

    incl = [(ii <= jj) if rv else (ii >= jj) for rv in revs]
    strict = [(ii < jj) if rv else (ii > jj) for rv in revs]
    last = [0 if rv else CHUNK - 1 for rv in revs]
    gc = [g[:, 0:GDN_QK] for g in gs]
    beta = [g[:, GDN_QK:2 * GDN_QK] for g in gs]
    gc_row = [jnp.sum(jnp.where(diag, x, 0.0), axis=0, keepdims=True) for x in gc]
    decay = each(lambda x, xr, ic: jnp.exp(jnp.where(ic, x - xr, -jnp.inf)), gc, gc_row, incl)

    def kt_bd(k):
        kt = k.T.astype(BF16)
        out = []
        for s in range(GDN_TILES):
            t = jnp.concatenate([kt[s * GDN_HALF:(s + 1) * GDN_HALF]] * (GDN_HALF // CHUNK), axis=1)
            out.append(jnp.where(bdm, t, jnp.zeros_like(t)))
        return out

    kkqk = each(lambda k, q, rk: _hprod(jnp.concatenate([k, q], axis=0), rk), ks, qs, [kt_bd(k) for k in ks])
    a = each(lambda st, b, x, dc: jnp.where(st, b * x[0:CHUNK] * dc, 0.0), strict, beta, kkqk, decay)
    qk = each(lambda x, dc: x[CHUNK:2 * CHUNK] * dc, kkqk, decay)

    def level_mask(blk, rv):
        same = (ii // (2 * blk)) == (jj // (2 * blk))
        off = (ii // blk) < (jj // blk) if rv else (ii // blk) > (jj // blk)
        return same & off

    m = each(lambda x, rv: eye - jnp.where(level_mask(1, rv), x, 0.0), a, revs)
    blk = 2
    while blk < CHUNK:
        am = each(lambda x, rv: jnp.where(level_mask(blk, rv), x, 0.0), a, revs)
        x = each(_hprod, am, bd(m))
        m = each(lambda mm, y, ybd: mm - _hprod(mm, ybd), m, x, bd(x))
        blk *= 2
    am = each(lambda x, mm: _hprod3(x, mm, bdm), a, m)
    resid = each(lambda mm, y: eye - mm - y, m, am)
    m = each(lambda mm, rbd: mm + _hprod(mm, rbd), m, bd(resid))

    egc = [jnp.exp(x) for x in gc]
    u = each(_hprod, m, bd(each(lambda v, b: v * b, vs, beta)))
    w = each(_hprod, m, bd(each(lambda k, b, e: k * (b * e), ks, beta, egc)))
    g_last = each(lambda x, l: x[l:l + 1, :], gc, last)
    st = [[sr[s] for s in range(GDN_TILES)] for sr in srefs]

    def sprod(x, tiles):
        return jnp.concatenate([_dot(x[:, s * GDN_HALF:(s + 1) * GDN_HALF], tiles[s]) for s in range(GDN_TILES)],
                               axis=1)

    ws = each(lambda ww, q, e, tiles: sprod(jnp.concatenate([ww, q * e], axis=0), tiles), w, qs, egc, st)
    v_new = each(lambda uu, x: uu - x[0:CHUNK], u, ws)
    o = each(lambda x, y, vbd: x[CHUNK:2 * CHUNK] + _hprod(y, vbd), ws, qk, bd(v_new))
    k_dec = each(lambda k, gl, x: k * jnp.exp(gl - x), ks, g_last, gc)
    e_last = [jnp.exp(gl) for gl in g_last]
    for i in range(n):
        for s in range(GDN_TILES):
            sl = slice(s * GDN_HALF, (s + 1) * GDN_HALF)
            srefs[i][s] = st[i][s] * e_last[i][:, sl] + jnp.where(
                bdm, _dot_tn(k_dec[i][:, sl], v_new[i][:, sl]), 0.0)
    return o


def _gdn_scan_kernel(qf_ref, kf_ref, vf_ref, gf_ref, qb_ref, kb_ref, vb_ref, gb_ref, of_ref, ob_ref, sf_ref, sb_ref):
    @pl.when(pl.program_id(1) == 0)
    def _():
        sf_ref[...] = jnp.zeros_like(sf_ref)
        sb_ref[...] = jnp.zeros_like(sb_ref)

    nb = qf_ref.shape[0]
    row = lax.broadcasted_iota(jnp.int32, (LANES, 2 * GDN_QK), 0)
    col = lax.broadcasted_iota(jnp.int32, (LANES, 2 * GDN_QK), 1)
    src = (col % GDN_QK) // GDN_DK + jnp.where(col >= GDN_QK, 2 * GDN_HEADS, 0)

    def spread(ref, d):
        stack = jnp.concatenate([ref[i] for i in range(nb)], axis=0)
        return _dot_sel_r(stack, (row == src + d * GDN_HEADS).astype(F32))

    gf, gb = spread(gf_ref, 0), spread(gb_ref, 1)
    chains = []
    for i in range(nb):
        rows = slice(i * CHUNK, (i + 1) * CHUNK)
        chains.append((qf_ref[i], kf_ref[i], vf_ref[i], gf[rows], sf_ref.at[i], False))
        chains.append((qb_ref[i], kb_ref[i], vb_ref[i], gb[rows], sb_ref.at[i], True))
    outs = _gdn_chunks(chains)
    for i in range(nb):
        of_ref[i] = outs[2 * i]
        ob_ref[i] = outs[2 * i + 1]


def _scan_chunk_maps(ctx_chunks, n_chunks):
    fwd = lambda s: s
    bwd = lambda s: jnp.where(s < ctx_chunks, ctx_chunks - 1 - s, n_chunks - 1 - (s - ctx_chunks))
    return fwd, bwd


def _gdn_scan(q, k, v, gates, *, n_ctx):
    b, rows, _ = q.shape
    n_chunks = rows // CHUNK
    fwd, bwd = _scan_chunk_maps(n_ctx // CHUNK, n_chunks)

    nb = GDN_SCAN_SAMPLES if b % GDN_SCAN_SAMPLES == 0 else 1

    def specs(cm):
        tok = lambda w: pl.BlockSpec((nb, CHUNK, w), lambda b_, s: (b_, cm(s), 0))
        return [tok(GDN_QK), tok(GDN_QK), tok(GDN_VW), tok(LANES)]

    return pl.pallas_call(
        _gdn_scan_kernel,
        grid=(b // nb, n_chunks),
        in_specs=specs(fwd) + specs(bwd),
        out_specs=[pl.BlockSpec((nb, CHUNK, GDN_VW), lambda b_, s: (b_, fwd(s), 0)),
                   pl.BlockSpec((nb, CHUNK, GDN_VW), lambda b_, s: (b_, bwd(s), 0))],
        out_shape=[jax.ShapeDtypeStruct((b, rows, GDN_VW), F32)] * 2,
        scratch_shapes=[pltpu.VMEM((nb, GDN_TILES, GDN_HALF, GDN_HALF), F32)] * 2,
        compiler_params=_params(),
        name="gdn_scan",
    )(q, k, v, gates, q, k, v, gates)


def _outproj_ffn_kernel(x_ref, mod_ref, oa_ref, of_ref, ob_ref, gate_ref, gn_ref, w_ref, g_ref, wg_ref, wu_ref, wd_ref,
                        o_ref, *, seg, rec_first):
    m = mod_ref[...]
    row = lambda r: m[r:r + 1]
    o = of_ref[...] + ob_ref[...]
    y = o * lax.rsqrt(_seg_mean_sq(o, seg, float(seg)) + EPS) * gn_ref[...] * _silu(gate_ref[...])
    wr = y.shape[-1]
    wa = oa_ref.shape[-1]
    if rec_first:
        out = jnp.dot(y.astype(BF16), w_ref[0:wr, :], preferred_element_type=F32)
        out = out + jnp.dot(oa_ref[...], w_ref[wr:wr + wa, :], preferred_element_type=F32)
    else:
        out = jnp.dot(oa_ref[...], w_ref[0:wa, :], preferred_element_type=F32)
        out = out + jnp.dot(y.astype(BF16), w_ref[wa:wa + wr, :], preferred_element_type=F32)
    x1 = x_ref[...] + row(5) * out
    hb = _rms_rows(x1, g_ref[...]) * (1.0 + row(7)) + row(6)
    o_ref[...] = x1 + (FFN_RES * row(8)) * _swiglu(hb, wg_ref, wu_ref, wd_ref)


def _outproj_ffn(xs, mod, o_att, o_f, o_b, gate, gn, w_out, g, wg, wu, wd, *, layer, seg, rec_first, n_ctx,
                 latent_only):
    b, rows, d = xs.shape
    tm = TM
    ctx_tiles = n_ctx // tm
    skip = ctx_tiles if latent_only else 0
    out_rows = rows - skip * tm
    tok = lambda w, sk: pl.BlockSpec((None, tm, w), lambda b_, t: (b_, t + sk, 0))
    wspec = lambda w: pl.BlockSpec((None, None) + w.shape[2:], lambda b_, t: (layer, 1, 0, 0),
                                   pipeline_mode=pl.Buffered(1))
    return pl.pallas_call(
        functools.partial(_outproj_ffn_kernel, seg=seg, rec_first=rec_first),
        grid=(b, out_rows // tm),
        in_specs=[tok(d, skip),
                  pl.BlockSpec((None, None, N_MOD, d), lambda b_, t: (b_, jnp.where(t + skip >= ctx_tiles, 1, 0), 0, 0)),
                  tok(o_att.shape[-1], 0), tok(o_f.shape[-1], skip), tok(o_b.shape[-1], skip),
                  tok(gate.shape[-1], skip), _const_spec(gn.shape), _const_spec(w_out.shape), _const_spec((1, d)),
                  wspec(wg), wspec(wu), wspec(wd)],
        out_specs=tok(d, 0),
        out_shape=jax.ShapeDtypeStruct((b, out_rows, d), F32),
        compiler_params=_params(),
        name="outproj_ffn",
    )(xs, mod, o_att, o_f, o_b, gate, gn, w_out, g.reshape(1, d), wg, wu, wd)


def _inproj_odd_kernel(x_ref, mod_ref, g_ref, win_ref, w2_ref, b2_ref, qn_ref, kn_ref, rc_ref, rs1_ref, rs2_ref,
                       gq_ref, gk_ref, gv_ref, rg_ref, bc_ref, sq_ref, sk_ref, sv_ref):
    x = x_ref[...]
    m = mod_ref[...]
    h = _rms_rows(x, g_ref[...]) * (1.0 + m[4:5]) + m[3:4]
    hb = h.astype(BF16)
    n_pass = 2 * GLA_QK + 2 * GLA_VW
    z = jnp.dot(hb, win_ref[:, n_pass:], preferred_element_type=F32)
    o = 0
    sq = z[:, o:o + SWA_QW]; o += SWA_QW
    sk = z[:, o:o + SWA_KW]; o += SWA_KW
    sv_ref[...] = z[:, o:o + SWA_KW].astype(sv_ref.dtype); o += SWA_KW
    lowrank = z[:, o:o + LANES]
    zp = jnp.dot(hb, win_ref[:, 0:n_pass], preferred_element_type=F32)
    o = 0
    gq_ref[...] = zp[:, o:o + GLA_QK]; o += GLA_QK
    gk_ref[...] = zp[:, o:o + GLA_QK]; o += GLA_QK
    gv_ref[...] = zp[:, o:o + GLA_VW]; o += GLA_VW
    rg_ref[...] = zp[:, o:o + GLA_VW]

    logit = _dot(lowrank, w2_ref[...]) + b2_ref[...]
    log_a = (jnp.minimum(logit, 0.0) - jnp.log1p(jnp.exp(-jnp.abs(logit)))) * (1.0 / GLA_TAU)
    bc_ref[:, 0:GLA_QK] = _dot_sel(_chunk_tri(TM, False), log_a[:, 0:GLA_QK])
    bc_ref[:, GLA_QK:2 * GLA_QK] = _dot_sel(_chunk_tri(TM, True), log_a[:, GLA_QK:2 * GLA_QK])

    rc, rs1, rs2 = rc_ref[...], rs1_ref[...], rs2_ref[...]
    sqn = sq * lax.rsqrt(_seg_mean_sq(sq, SWA_DH, float(SWA_DH)) + EPS) * (qn_ref[...] * SWA_Q_PRESCALE)
    for s in range(SWA_QW // LANES):
        sl = slice(s * LANES, (s + 1) * LANES)
        sq_ref[:, sl] = _rope(sqn[:, sl], rc, rs1, rs2, SWA_DH // 2).astype(sq_ref.dtype)
    skn = sk * lax.rsqrt(_seg_mean_sq(sk, SWA_DH, float(SWA_DH)) + EPS) * kn_ref[...]
    sk_ref[...] = _rope(skn, rc, rs1, rs2, SWA_DH // 2).astype(sk_ref.dtype)


def _inproj_odd(xs, mod, g, p, rope):
    b, rows, d = xs.shape
    consts = [g.reshape(1, d), p["w_in"], p["w2"], p["b2"], p["q_norm"], p["k_norm"]]
    rope_spec = pl.BlockSpec((TM, LANES), lambda b_, t: (t, 0))
    widths = [(GLA_QK, F32), (GLA_QK, F32), (GLA_VW, F32), (GLA_VW, F32), (2 * GLA_QK, F32),
              (SWA_QW, BF16), (SWA_KW, BF16), (SWA_KW, BF16)]
    return pl.pallas_call(
        _inproj_odd_kernel,
        grid=(b, rows // TM),
        in_specs=[_tok_spec(d), _mod_spec(p["ctx_tiles"])] + [_const_spec(c.shape) for c in consts] + [rope_spec] * 3,
        out_specs=[_tok_spec(w) for w, _ in widths],
        out_shape=[jax.ShapeDtypeStruct((b, rows, w), dt) for w, dt in widths],
        compiler_params=_params(),
        name="inproj_odd",
    )(xs, mod, *consts, *rope)


def _gla_chunk(q, k, v, bc, st_ref, reverse):
    def blk(shape, rdiv, cdiv):
        return (lax.broadcasted_iota(jnp.int32, shape, 0) // rdiv) == (lax.broadcasted_iota(jnp.int32, shape, 1) // cdiv)

    ii = lax.broadcasted_iota(jnp.int32, (CHUNK, GLA_QK), 0)
    jj = lax.broadcasted_iota(jnp.int32, (CHUNK, GLA_QK), 1) % CHUNK
    incl = (ii <= jj) if reverse else (ii >= jj)
    last = 0 if reverse else CHUNK - 1
    q_dec = (q * (GLA_DK ** -0.5)) * jnp.exp(bc)
    k_inv = k * jnp.exp(-bc)
    b_last = bc[last:last + 1, :]
    k_dec = k * jnp.exp(b_last - bc)
    e_last = jnp.exp(b_last)
    kt = jnp.concatenate([k_inv.T.astype(BF16)] * GLA_HEADS, axis=1)
    rk = jnp.where(blk((GLA_QK, GLA_QK), GLA_DK, CHUNK), kt, jnp.zeros_like(kt))
    attn = jnp.where(incl, jnp.dot(q_dec.astype(BF16), rk, preferred_element_type=F32), 0.0)
    vt = jnp.concatenate([v.astype(BF16)] * GLA_HEADS, axis=0)
    vbd = jnp.where(blk((GLA_QK, GLA_VW), CHUNK, GLA_DV), vt, jnp.zeros_like(vt))
    st = st_ref[...]
    o = jnp.dot(attn.astype(BF16), vbd, preferred_element_type=F32) + _dot_nt(q_dec, st)
    st_ref[...] = st * e_last + jnp.where(blk((GLA_VW, GLA_QK), GLA_DV, GLA_DK), _dot_tn(v, k_dec), 0.0)
    return o


def _gla_scan_kernel(qf_ref, kf_ref, vf_ref, bf_ref, qb_ref, kb_ref, vb_ref, bb_ref, of_ref, ob_ref, sf_ref, sb_ref):
    @pl.when(pl.program_id(1) == 0)
    def _():
        sf_ref[...] = jnp.zeros_like(sf_ref)
        sb_ref[...] = jnp.zeros_like(sb_ref)

    for i in range(qf_ref.shape[0]):
        of_ref[i] = _gla_chunk(qf_ref[i], kf_ref[i], vf_ref[i], bf_ref[i][:, 0:GLA_QK], sf_ref.at[i], False)
        ob_ref[i] = _gla_chunk(qb_ref[i], kb_ref[i], vb_ref[i], bb_ref[i][:, GLA_QK:2 * GLA_QK], sb_ref.at[i], True)


def _gla_scan(q, k, v, bc, *, n_ctx):
    b, rows, _ = q.shape
    n_chunks = rows // CHUNK
    fwd, bwd = _scan_chunk_maps(n_ctx // CHUNK, n_chunks)
    nb = GLA_SCAN_SAMPLES if b % GLA_SCAN_SAMPLES == 0 else 1

    def specs(cm):
        tok = lambda w: pl.BlockSpec((nb, CHUNK, w), lambda b_, s: (b_, cm(s), 0))
        return [tok(GLA_QK), tok(GLA_QK), tok(GLA_VW), tok(2 * GLA_QK)]

    return pl.pallas_call(
        _gla_scan_kernel,
        grid=(b // nb, n_chunks),
        in_specs=specs(fwd) + specs(bwd),
        out_specs=[pl.BlockSpec((nb, CHUNK, GLA_VW), lambda b_, s: (b_, fwd(s), 0)),
                   pl.BlockSpec((nb, CHUNK, GLA_VW), lambda b_, s: (b_, bwd(s), 0))],
        out_shape=[jax.ShapeDtypeStruct((b, rows, GLA_VW), F32)] * 2,
        scratch_shapes=[pltpu.VMEM((nb, GLA_VW, GLA_QK), F32)] * 2,
        compiler_params=_params(),
        name="gla_scan",
    )(q, k, v, bc, q, k, v, bc)


def _swa_kernel(q_ref, k_ref, v_ref, sink_ref, o_ref, *, n_ctx, n_lat):
    w = SWA_WINDOW
    n = pl.program_id(1)
    nb = n_lat // w

    def rows(ref, blk):
        return ref[pl.ds(pl.multiple_of(n_ctx + blk * w, w), w), :]

    pb = jnp.maximum(n - 1, 0)
    xb = jnp.minimum(n + 1, nb - 1)
    k_cat = jnp.concatenate([rows(k_ref, pb), rows(k_ref, n), rows(k_ref, xb), k_ref[0:n_ctx, :]], axis=0)
    v_cat = jnp.concatenate([rows(v_ref, pb), rows(v_ref, n), rows(v_ref, xb), v_ref[0:n_ctx, :]], axis=0)
    v_one = jnp.concatenate([v_cat, jnp.ones_like(v_cat)], axis=1)
    half = SWA_HEADS // 2
    ii = lax.broadcasted_iota(jnp.int32, (half * w, w), 0) % w
    jj = lax.broadcasted_iota(jnp.int32, (half * w, w), 1)
    ninf = -jnp.inf
    bias_prev = jnp.where((jj >= ii) & (n > 0), 0.0, ninf)
    bias_next = jnp.where((jj <= ii) & (n < nb - 1), 0.0, ninf)
    lo = lax.broadcasted_iota(jnp.int32, (half * w, LANES), 1) < SWA_DH
    q = q_ref[...]
    qst = jnp.concatenate([q[:, j * LANES:(j + 1) * LANES] for j in range(half)], axis=0)
    sink = sink_ref[...]
    dims = (((1,), (1,)), ((), ()))
    qm = [jnp.where(lo if g == 0 else jnp.logical_not(lo), qst, jnp.zeros_like(qst)) for g in range(SWA_KV_HEADS)]
    s = [lax.dot_general(x, k_cat, dims, preferred_element_type=F32) for x in qm]
    parts = [[x[:, 0:w] + bias_prev, x[:, w:2 * w], x[:, 2 * w:3 * w] + bias_next, x[:, 3 * w:]] for x in s]
    sk = [jnp.concatenate([jnp.broadcast_to(sink[0:1, half * g + j:half * g + j + 1], (w, 1)) for j in range(half)],
                          axis=0) for g in range(SWA_KV_HEADS)]
    def row_max(ps, k_):
        slabs = [x[:, c * LANES:(c + 1) * LANES] for x in ps for c in range(x.shape[1] // LANES)]
        return jnp.maximum(jnp.max(functools.reduce(jnp.maximum, slabs), axis=-1, keepdims=True), k_)

    mx = [row_max(ps, k_) for ps, k_ in zip(parts, sk)]
    p = [jnp.concatenate([jnp.exp2(x - m).astype(BF16) for x in ps], axis=1) for ps, m in zip(parts, mx)]
    r = [jnp.dot(x, v_one, preferred_element_type=F32) for x in p]
    res = [x[:, 0:LANES] / (x[:, LANES:2 * LANES] + jnp.exp2(k_ - m)) for x, k_, m in zip(r, sk, mx)]
    o = jnp.where(lo, res[0], res[1]).astype(o_ref.dtype)
    for j in range(half):
        o_ref[:, j * LANES:(j + 1) * LANES] = o[j * w:(j + 1) * w, :]


def _swa_attention(q, k, v, sink, *, n_ctx):
    b, rows, _ = q.shape
    n_lat = rows - n_ctx
    w = SWA_WINDOW
    skip = n_ctx // w
    return pl.pallas_call(
        functools.partial(_swa_kernel, n_ctx=n_ctx, n_lat=n_lat),
        grid=(b, n_lat // w),
        in_specs=[
            pl.BlockSpec((None, w, SWA_QW), lambda b_, n: (b_, n + skip, 0)),
            pl.BlockSpec((None, rows, SWA_KW), lambda b_, n: (b_, 0, 0)),
            pl.BlockSpec((None, rows, SWA_KW), lambda b_, n: (b_, 0, 0)),
            _const_spec((1, LANES)),
        ],
        out_specs=pl.BlockSpec((None, w, SWA_QW), lambda b_, n: (b_, n, 0)),
        out_shape=jax.ShapeDtypeStruct((b, n_lat, SWA_QW), BF16),
        compiler_params=_params(),
        name="swa_attention",
    )(q, k, v, sink)


def _rope_tables(n_lat, n_ctx, rot_dim):
    t = jnp.arange(n_lat)
    row = (t // GRID_W).astype(F32)
    col = (t % GRID_W).astype(F32)
    n_freq = rot_dim // 4
    inv = ROPE_THETA ** (-jnp.arange(n_freq, dtype=F32) / n_freq)
    ang = jnp.concatenate([row[:, None] * inv, col[:, None] * inv], axis=-1)
    half = rot_dim // 2
    cos = jnp.concatenate([jnp.ones((n_ctx, half), F32), jnp.cos(ang)], axis=0)
    sin = jnp.concatenate([jnp.zeros((n_ctx, half), F32), jnp.sin(ang)], axis=0)
    rows = n_ctx + n_lat
    one = lambda w: jnp.ones((rows, w), F32)
    zero = lambda w: jnp.zeros((rows, w), F32)
    if rot_dim == MLA_ROPE:
        c = jnp.concatenate([one(MLA_NOPE), cos, cos, one(HEAD_PAD - MLA_QK)], axis=1)
        s1 = jnp.concatenate([zero(MLA_NOPE), -sin, zero(half), zero(HEAD_PAD - MLA_QK)], axis=1)
        s2 = jnp.concatenate([zero(MLA_NOPE), zero(half), sin, zero(HEAD_PAD - MLA_QK)], axis=1)
    else:
        c = jnp.concatenate([cos, cos, cos, cos], axis=1)
        s1 = jnp.concatenate([-sin, zero(half), -sin, zero(half)], axis=1)
        s2 = jnp.concatenate([zero(half), sin, zero(half), sin], axis=1)
    return c, s1, s2


def _even_params(j, n_ctx, ev_w_in, ev_q_a_norm, ev_w_q_up, ev_kv_a_norm, ev_w_kv_up, ev_mla_q_norm, ev_mla_k_norm,
                 ev_gdn_conv, ev_gdn_a_log, ev_gdn_dt_bias, ev_gdn_out_norm, ev_w_out):
    w = ev_w_in[j]
    d = w.shape[0]
    z = lambda n: jnp.zeros((d, n), F32)
    o_kr = MLA_Q_RANK + MLA_KV_RANK
    o_g = o_kr + MLA_ROPE
    o_small = o_g + 3 * GDN_QK
    o_gate = o_small + 4 * GDN_HEADS
    w_in = jnp.concatenate([
        w[:, :o_kr], z(MLA_NOPE), w[:, o_kr:o_g], z(HEAD_PAD - MLA_QK),
        w[:, o_g:o_small], w[:, o_gate:o_gate + GDN_VW],
        w[:, o_small:o_gate], z(LANES - 4 * GDN_HEADS)], axis=1).astype(BF16)
    pad_h = HEAD_PAD - MLA_QK
    w_q = jnp.pad(ev_w_q_up[j].reshape(MLA_Q_RANK, MLA_HEADS, MLA_QK), ((0, 0), (0, 0), (0, pad_h)))
    wkv = ev_w_kv_up[j].reshape(MLA_KV_RANK, MLA_HEADS, MLA_NOPE + MLA_V)
    w_kk = jnp.pad(wkv[:, :, :MLA_NOPE], ((0, 0), (0, 0), (0, HEAD_PAD - MLA_NOPE)))
    lane_row = lambda vec: jnp.pad(vec, (0, LANES - vec.shape[0])).reshape(1, LANES)
    fb = lambda a: jnp.concatenate([a[0], a[1]])
    return {
        "ctx_tiles": n_ctx // TM,
        "w_in": w_in,
        "q_a_norm": ev_q_a_norm[j].reshape(1, -1),
        "w_q": w_q.reshape(MLA_Q_RANK, MLA_HEADS * HEAD_PAD).astype(BF16),
        "kv_a_norm": ev_kv_a_norm[j].reshape(1, -1),
        "w_kk": w_kk.reshape(MLA_KV_RANK, MLA_HEADS * HEAD_PAD).astype(BF16),
        "w_kv": jnp.pad(wkv[:, :, MLA_NOPE:], ((0, 0), (0, 0), (0, HEAD_PAD - MLA_V))
                        ).reshape(MLA_KV_RANK, MLA_HEADS * HEAD_PAD).astype(BF16),
        "q_norm": lane_row(ev_mla_q_norm[j]),
        "k_norm": lane_row(ev_mla_k_norm[j]),
        "conv_w": ev_gdn_conv[j],
        "neg_a": lane_row(-jnp.exp(fb(ev_gdn_a_log[j]))),
        "dt_bias": lane_row(fb(ev_gdn_dt_bias[j])),
        "out_norm": jnp.tile(ev_gdn_out_norm[j], GDN_HEADS).reshape(1, GDN_VW),
        "w_out": ev_w_out[j].astype(BF16),
    }


def _swa_head_perm():
    half = SWA_HEADS // 2
    heads = [h for j in range(half) for h in (j, half + j)]
    return jnp.concatenate([jnp.arange(SWA_DH) + h * SWA_DH for h in heads])


def _odd_params(j, n_ctx, od_w_in, od_gla_gate_w2, od_gla_gate_b, od_gla_out_norm, od_swa_q_norm, od_swa_k_norm,
                od_swa_sink, od_w_out):
    w = od_w_in[j]
    d = w.shape[0]
    o_gate = 2 * GLA_QK + GLA_VW
    o_rg = o_gate + 2 * GLA_RANK
    o_sq = o_rg + GLA_VW
    o_sk = o_sq + SWA_QW
    perm = _swa_head_perm()
    w_in = jnp.concatenate([
        w[:, :o_gate], w[:, o_rg:o_sq], w[:, o_sq:o_sk][:, perm], w[:, o_sk:],
        w[:, o_gate:o_rg], jnp.zeros((d, LANES - 2 * GLA_RANK), F32)], axis=1).astype(BF16)
    w2 = jnp.zeros((LANES, 2 * GLA_QK), F32)
    w2 = w2.at[0:GLA_RANK, 0:GLA_QK].set(od_gla_gate_w2[j, 0])
    w2 = w2.at[GLA_RANK:2 * GLA_RANK, GLA_QK:].set(od_gla_gate_w2[j, 1])
    wo = od_w_out[j]
    w_out = jnp.concatenate([wo[:GLA_VW], wo[GLA_VW:][perm]], axis=0).astype(BF16)
    return {
        "ctx_tiles": n_ctx // TM,
        "w_in": w_in,
        "w2": w2.astype(BF16),
        "b2": jnp.concatenate([od_gla_gate_b[j, 0], od_gla_gate_b[j, 1]]).reshape(1, 2 * GLA_QK),
        "q_norm": jnp.tile(od_swa_q_norm[j], SWA_HEADS).reshape(1, SWA_QW),
        "k_norm": jnp.tile(od_swa_k_norm[j], SWA_KV_HEADS).reshape(1, SWA_KW),
        "sink": jnp.pad(od_swa_sink[j] * math.log2(math.e), (0, LANES - SWA_HEADS)).reshape(1, LANES),
        "out_norm": jnp.tile(od_gla_out_norm[j], GLA_HEADS).reshape(1, GLA_VW),
        "w_out": w_out,
    }


def kernel(x, c, ctx, c_ctx, ada_w, ada_b, norm_g, ffn_w_gate, ffn_w_up, ffn_w_down, ev_w_in, ev_q_a_norm, ev_w_q_up, ev_kv_a_norm, ev_w_kv_up, ev_mla_q_norm, ev_mla_k_norm, ev_gdn_conv, ev_gdn_a_log, ev_gdn_dt_bias, ev_gdn_out_norm, ev_w_out, od_w_in, od_gla_gate_w2, od_gla_gate_b, od_gla_out_norm, od_swa_q_norm, od_swa_k_norm, od_swa_sink, od_w_out):
    b, n_lat, d = x.shape
    n_ctx = ctx.shape[1]
    depth = ada_w.shape[0]
    assert d == D_MODEL and n_ctx % TM == 0 and n_lat % TM == 0 and n_lat % GRID_W == 0
    assert depth % 2 == 0, "the last layer must be an odd (GLA/SWA) layer: context outputs of that mixer are not built"
    ctx_tiles = n_ctx // TM

    cvec = jnp.concatenate([c, c_ctx[None, :], jnp.zeros((16 - b - 1, d), F32)], axis=0)
    mod_all = _ada(cvec, ada_w, ada_b).reshape(depth, 16, N_MOD, d)
    rope_mla = _rope_tables(n_lat, n_ctx, MLA_ROPE)
    rope_swa = _rope_tables(n_lat, n_ctx, SWA_DH)

    wg, wu, wd = ffn_w_gate, ffn_w_up, ffn_w_down
    xs = x
    for i in range(depth):
        last = i == depth - 1
        mod = jnp.stack([jnp.broadcast_to(mod_all[i, b][None], (b, N_MOD, d)), mod_all[i, :b]], axis=1)
        xs = _ffn(xs, mod, norm_g[i, 0], wg, wu, wd, layer=i, half=0, r0=0, n_ctx=n_ctx, ctx=ctx if i == 0 else None)
        j = i // 2
        if i % 2 == 0:
            p = _even_params(j, n_ctx, ev_w_in, ev_q_a_norm, ev_w_q_up, ev_kv_a_norm, ev_w_kv_up, ev_mla_q_norm,
                             ev_mla_k_norm, ev_gdn_conv, ev_gdn_a_log, ev_gdn_dt_bias, ev_gdn_out_norm, ev_w_out)
            q, k, v, zg, gate, small = _inproj_even(xs, mod, norm_g[i, 1], p, rope_mla)
            o_att = _mla_attention(q, k, v, n_ctx=n_ctx)
            gq, gk, gv, gates = _gdn_prep(zg, small, p["conv_w"], p["neg_a"], p["dt_bias"], ctx_tiles=ctx_tiles)
            o_f, o_b = _gdn_scan(gq, gk, gv, gates, n_ctx=n_ctx)
            assert not last, "an even last layer would hand a joint o_att to the latent-only output stage"
            mix = dict(o_att=o_att, o_f=o_f, o_b=o_b, gate=gate, seg=GDN_DV, rec_first=False)
        else:
            assert last, "odd layers that must also produce context outputs are not built"
            p = _odd_params(j, n_ctx, od_w_in, od_gla_gate_w2, od_gla_gate_b, od_gla_out_norm, od_swa_q_norm,
                            od_swa_k_norm, od_swa_sink, od_w_out)
            gq, gk, gv, rg, bc, sq, sk, sv = _inproj_odd(xs, mod, norm_g[i, 1], p, rope_swa)
            o_f, o_b = _gla_scan(gq, gk, gv, bc, n_ctx=n_ctx)
            o_att = _swa_attention(sq, sk, sv, p["sink"], n_ctx=n_ctx)
            mix = dict(o_att=o_att, o_f=o_f, o_b=o_b, gate=rg, seg=GLA_DV, rec_first=True)
        xs = _outproj_ffn(xs, mod, mix["o_att"], mix["o_f"], mix["o_b"], mix["gate"], p["out_norm"], p["w_out"],
                          norm_g[i, 2], wg, wu, wd, layer=i, seg=mix["seg"], rec_first=mix["rec_first"],
                          n_ctx=n_ctx, latent_only=last)
    return xs
```

```python
import functools
import math

import jax
import jax.numpy as jnp
from jax import lax
from jax.experimental import pallas as pl
from jax.experimental.pallas import tpu as pltpu

F32 = jnp.float32
BF16 = jnp.bfloat16

D_MODEL = 1024
GRID_W = 64
D_FF = 2816
FFN_RES = 0.5
N_MOD = 9
EPS = 1e-6
ROPE_THETA = 10000.0
CHUNK = 64

MLA_HEADS = 8
MLA_NOPE = 64
MLA_ROPE = 32
MLA_QK = MLA_NOPE + MLA_ROPE
MLA_V = 64
MLA_Q_RANK = 384
MLA_KV_RANK = 256
MLA_VW = MLA_HEADS * MLA_V
MLA_STEP_HEADS = 8
MLA_Q_PRESCALE =MLA_QK ** -0.5 * math.log2(math.e)

GDN_HEADS = 8
GDN_DK = 64
GDN_DV = 64
GDN_CONV = 5
GDN_QK = GDN_HEADS * GDN_DK
GDN_VW = GDN_HEADS * GDN_DV
GDN_HALF = 128
GDN_TILES = GDN_QK // GDN_HALF
GDN_SCAN_SAMPLES = 8

GLA_HEADS = 4
GLA_DK = 64
GLA_DV = 128
GLA_RANK = 16
GLA_TAU = 16.0
GLA_QK = GLA_HEADS * GLA_DK
GLA_VW = GLA_HEADS * GLA_DV
GLA_SCAN_SAMPLES = 8

SWA_HEADS = 8
SWA_KV_HEADS = 2
SWA_DH = 64
SWA_WINDOW = 128
SWA_QW = SWA_HEADS * SWA_DH
SWA_KW = SWA_KV_HEADS * SWA_DH
SWA_Q_PRESCALE = SWA_DH ** -0.5 * math.log2(math.e)

LANES = 128
HEAD_PAD = 128
TM = 256
FFN_TILES = (576, 512, 256, 128)
FFN_SPLIT = 1536
VMEM_LIMIT =56 * 1024 * 1024

EVEN_COLS = MLA_Q_RANK + MLA_KV_RANK + LANES + 2 * GDN_QK + GDN_VW + GDN_VW + LANES
ODD_COLS = 2 * GLA_QK + GLA_VW + GLA_VW + SWA_QW + 2 * SWA_KW + LANES


def _dot(a, b):
    return jnp.dot(a.astype(BF16), b.astype(BF16), preferred_element_type=F32)


def _dot_nt(a, b):
    return lax.dot_general(a.astype(BF16), b.astype(BF16), (((1,), (1,)), ((), ())), preferred_element_type=F32)


def _dot_tn(a, b):
    return lax.dot_general(a.astype(BF16), b.astype(BF16), (((0,), (0,)), ((), ())), preferred_element_type=F32)


def _split2(x):
    hi = x.astype(BF16)
    lo = (x - hi.astype(F32)).astype(BF16)
    return hi, lo


def _split3(x):
    hi = x.astype(BF16)
    r = x - hi.astype(F32)
    mid = r.astype(BF16)
    lo = (r - mid.astype(F32)).astype(BF16)
    return hi, mid, lo


def _dot3(a, b):
    ah, al = _split2(a)
    bh, bl = _split2(b)
    d = functools.partial(jnp.dot, preferred_element_type=F32)
    return d(ah, bh) + (d(ah, bl) + d(al, bh))


def _dot_sel(sel, x):
    s = sel.astype(BF16)
    hi, mid, lo = _split3(x)
    d = functools.partial(jnp.dot, preferred_element_type=F32)
    return d(s, hi) + (d(s, mid) + d(s, lo))


def _dot_sel_r(x, sel):
    s = sel.astype(BF16)
    hi, mid, lo = _split3(x)
    d = functools.partial(jnp.dot, preferred_element_type=F32)
    return d(hi, s) + (d(mid, s) + d(lo, s))


def _seg_mean_sq(x, seg, n_real):
    w = x.shape[-1]
    r = lax.broadcasted_iota(jnp.int32, (w, w), 0) // seg
    c = lax.broadcasted_iota(jnp.int32, (w, w), 1) // seg
    ones_bd = (r == c).astype(BF16)
    hi, lo = _split2(x * x)
    d = functools.partial(jnp.dot, preferred_element_type=F32)
    return (d(hi, ones_bd) + d(lo, ones_bd)) * (1.0 / n_real)


def _rms_rows(x, g):
    ms = jnp.mean(x * x, axis=-1, keepdims=True)
    return x * lax.rsqrt(ms + EPS) * g


def _sigmoid(x):
    return 1.0 / (1.0 + jnp.exp(-x))


def _silu(x):
    return x * _sigmoid(x)


def _softplus(x):
    return jnp.maximum(x, 0.0) + jnp.log1p(jnp.exp(-jnp.abs(x)))


def _rope(x, c, s1, s2, half):
    return x * c + pltpu.roll(x, LANES - half, 1) * s1 + pltpu.roll(x, half, 1) * s2


def _chunk_tri(n, reverse):
    i = lax.broadcasted_iota(jnp.int32, (n, n), 0)
    j = lax.broadcasted_iota(jnp.int32, (n, n), 1)
    same = (i // CHUNK) == (j // CHUNK)
    tri = (j >= i) if reverse else (j <= i)
    return (same & tri).astype(F32)


def _ada_kernel(c_ref, w_ref, b_ref, o_ref):
    sc = _silu(c_ref[...])
    o_ref[...] = _dot(sc, w_ref[...]) + b_ref[...]


def _ada(cvec, ada_w, ada_b):
    depth, d, nm = ada_w.shape
    tn = 1024
    return pl.pallas_call(
        _ada_kernel,
        grid=(depth, nm // tn),
        in_specs=[
            pl.BlockSpec(cvec.shape, lambda i, j: (0, 0)),
            pl.BlockSpec((None, d, tn), lambda i, j: (i, 0, j)),
            pl.BlockSpec((None, 1, tn), lambda i, j: (i, 0, j)),
        ],
        out_specs=pl.BlockSpec((None, cvec.shape[0], tn), lambda i, j: (i, 0, j)),
        out_shape=jax.ShapeDtypeStruct((depth, cvec.shape[0], nm), F32),
        compiler_params=pltpu.CompilerParams(vmem_limit_bytes=VMEM_LIMIT),
        name="ada_mod",
    )(cvec, ada_w, ada_b.reshape(depth, 1, nm))


def _const_spec(shape):
    nd = len(shape)
    return pl.BlockSpec(shape, lambda *_: (0,) * nd, pipeline_mode=pl.Buffered(1))


def _tok_spec(width):
    return pl.BlockSpec((None, TM, width), lambda b, t: (b, t, 0))


def _mod_spec(ctx_tiles):
    return pl.BlockSpec((None, None, N_MOD, D_MODEL), lambda b, t: (b, jnp.where(t >= ctx_tiles, 1, 0), 0, 0))


def _params():
    return pltpu.CompilerParams(dimension_semantics=("parallel", "arbitrary"), vmem_limit_bytes=VMEM_LIMIT)


def _swiglu(hb, wg_ref, wu_ref, wd_ref):
    f = wg_ref.shape[1]
    y = None
    for lo, hi in ((0, FFN_SPLIT), (FFN_SPLIT, f)):
        a = jnp.dot(hb, wg_ref[:, lo:hi], preferred_element_type=F32)
        u = jnp.dot(hb, wu_ref[:, lo:hi], preferred_element_type=F32)
        part = jnp.dot(_silu(a) * u, wd_ref[lo:hi, :], preferred_element_type=F32)
        y = part if y is None else y + part
    return y


def _ffn_kernel(*refs, r0, n_ctx, split):
    if split:
        c_ref, x_ref, mod_ref, g_ref, wg_ref, wu_ref, wd_ref, o_ref = refs
    else:
        x_ref, mod_ref, g_ref, wg_ref, wu_ref, wd_ref, o_ref = refs
    x = x_ref[...]
    m = mod_ref[...]
    tm = x.shape[0]
    if n_ctx:
        is_ctx = pl.program_id(1) * tm + lax.broadcasted_iota(jnp.int32, (tm, 1), 0) < n_ctx
        row = lambda r: jnp.where(is_ctx, m[0, r:r + 1], m[1, r:r + 1])
        if split:
            x = jnp.where(is_ctx, c_ref[...], x)
    else:
        row = lambda r: m[1, r:r + 1]
    h = _rms_rows(x, g_ref[...]) * (1.0 + row(r0 + 1)) + row(r0)
    y = _swiglu(h, wg_ref, wu_ref, wd_ref)
    o_ref[...] = x + (FFN_RES * row(r0 + 2)) * y


def _ffn(xs, mod, g, wg, wu, wd, *, layer, half, r0, n_ctx, ctx=None):
    b, rows, d = xs.shape
    split = ctx is not None
    if split:
        tm = TM
        ctx_tiles = n_ctx // tm
        rows += n_ctx
        toks = [pl.BlockSpec((None, tm, d), lambda b_, t: (b_, jnp.minimum(t, ctx_tiles - 1), 0)),
                pl.BlockSpec((None, tm, d), lambda b_, t: (b_, jnp.maximum(t - ctx_tiles, 0), 0))]
        data = [ctx, xs]
    else:
        tm = next(t for t in FFN_TILES if rows % t == 0)
        toks = [pl.BlockSpec((None, tm, d), lambda b_, t: (b_, t, 0))]
        data = [xs]
    wspec = lambda w: pl.BlockSpec((None, None) + w.shape[2:], lambda b_, t: (layer, half, 0, 0),
                                   pipeline_mode=pl.Buffered(1))
    return pl.pallas_call(
        functools.partial(_ffn_kernel, r0=r0, n_ctx=n_ctx, split=split),
        grid=(b, rows // tm),
        in_specs=toks + [pl.BlockSpec((None, 2, N_MOD, d), lambda b_, t: (b_, 0, 0, 0)), _const_spec((1, d)),
                         wspec(wg), wspec(wu), wspec(wd)],
        out_specs=pl.BlockSpec((None, tm, d), lambda b_, t: (b_, t, 0)),
        out_shape=jax.ShapeDtypeStruct((b, rows, d), F32),
        compiler_params=_params(),
        name="ffn_half_step",
    )(*data, mod, g.reshape(1, d), wg, wu, wd)


def _inproj_even_kernel(x_ref, mod_ref, g_ref, win_ref, qan_ref, wq_ref, kvan_ref, wkk_ref, wkv_ref,
                        qn_ref, kn_ref, rc_ref, rs1_ref, rs2_ref,
                        q_ref, k_ref, v_ref, zg_ref, gate_ref, small_ref):
    x = x_ref[...]
    m = mod_ref[...]
    h = _rms_rows(x, g_ref[...]) * (1.0 + m[4:5]) + m[3:4]
    hb = h.astype(BF16)
    n_mla = MLA_Q_RANK + MLA_KV_RANK + LANES
    cq = jnp.dot(hb, win_ref[:, 0:MLA_Q_RANK], preferred_element_type=F32)
    ckv = jnp.dot(hb, win_ref[:, MLA_Q_RANK:MLA_Q_RANK + MLA_KV_RANK], preferred_element_type=F32)
    kr = jnp.dot(hb, win_ref[:, MLA_Q_RANK + MLA_KV_RANK:n_mla], preferred_element_type=F32)

    rc, rs1, rs2 = rc_ref[...], rs1_ref[...], rs2_ref[...]
    qn, kn = qn_ref[...], kn_ref[...]
    qf = jnp.dot(_rms_rows(cq, qan_ref[...]).astype(BF16), wq_ref[...], preferred_element_type=F32)
    ckvn = _rms_rows(ckv, kvan_ref[...]).astype(BF16)
    kf = jnp.dot(ckvn, wkk_ref[...], preferred_element_type=F32)
    lane = lax.broadcasted_iota(jnp.int32, (1, MLA_HEADS * HEAD_PAD), 1)
    ones_cols = (lane % HEAD_PAD >= MLA_V).astype(F32)
    v_ref[...] = (jnp.dot(ckvn, wkv_ref[...], preferred_element_type=F32) + ones_cols).astype(v_ref.dtype)
    slabs = [slice(hd * HEAD_PAD, (hd + 1) * HEAD_PAD) for hd in range(MLA_HEADS)]
    xs = [qf[:, sl] for sl in slabs] + [kf[:, sl] + kr for sl in slabs]
    gains = [qn * MLA_Q_PRESCALE] * MLA_HEADS + [kn] * MLA_HEADS
    def gdn_cols(lo, hi):
        return jnp.dot(hb, win_ref[:, n_mla + lo:n_mla + hi], preferred_element_type=F32)

    ss = [jnp.sum(x * x, axis=-1, keepdims=True) for x in xs]
    zg_ref[:, 0:GDN_QK] = gdn_cols(0, GDN_QK)
    xs = [x * lax.rsqrt(s * (1.0 / MLA_QK) + EPS) * g for x, s, g in zip(xs, ss, gains)]
    zg_ref[:, GDN_QK:2 * GDN_QK] = gdn_cols(GDN_QK, 2 * GDN_QK)
    xs = [_rope(x, rc, rs1, rs2, MLA_ROPE // 2) for x in xs]
    zg_ref[:, 2 * GDN_QK:3 * GDN_QK] = gdn_cols(2 * GDN_QK, 3 * GDN_QK)
    for hd, sl in enumerate(slabs):
        q_ref[:, sl] = xs[hd].astype(q_ref.dtype)
        k_ref[:, sl] = xs[MLA_HEADS + hd].astype(k_ref.dtype)
    zr = gdn_cols(3 * GDN_QK, 3 * GDN_QK + GDN_VW + LANES)
    gate_ref[...] = zr[:, 0:GDN_VW]
    small_ref[...] = zr[:, GDN_VW:]


def _inproj_even(xs, mod, g, p, rope):
    b, rows, d = xs.shape
    ctx_tiles = p["ctx_tiles"]
    consts = [g.reshape(1, d), p["w_in"], p["q_a_norm"], p["w_q"], p["kv_a_norm"], p["w_kk"], p["w_kv"],
              p["q_norm"], p["k_norm"]]
    rope_spec = pl.BlockSpec((TM, LANES), lambda b_, t: (t, 0))
    widths = [(MLA_HEADS * HEAD_PAD, BF16), (MLA_HEADS * HEAD_PAD, BF16), (MLA_HEADS * HEAD_PAD, BF16),
              (3 * GDN_QK, F32), (GDN_VW, F32), (LANES, F32)]
    return pl.pallas_call(
        _inproj_even_kernel,
        grid=(b, rows // TM),
        in_specs=[_tok_spec(d), _mod_spec(ctx_tiles)] + [_const_spec(c.shape) for c in consts] + [rope_spec] * 3,
        out_specs=[_tok_spec(w) for w, _ in widths],
        out_shape=[jax.ShapeDtypeStruct((b, rows, w), dt) for w, dt in widths],
        compiler_params=_params(),
        name="inproj_even",
    )(xs, mod, *consts, *rope)


def _mla_kernel(q_ref, k_ref, v_ref, o_ref, *, n_ctx, n_all, ctx_tiles):
    t = pl.program_id(2)
    lo = lax.broadcasted_iota(jnp.int32, (TM, LANES), 1) < MLA_V
    n_heads = q_ref.shape[-1] // HEAD_PAD

    def attend(nk):
        q = q_ref[...]
        sls = [slice(hh * HEAD_PAD, (hh + 1) * HEAD_PAD) for hh in range(n_heads)]
        s = [lax.dot_general(q[:, sl], k_ref[0:nk, sl], (((1,), (1,)), ((), ())), preferred_element_type=F32)
             for sl in sls]
        mx = [jnp.max(x, axis=-1, keepdims=True) for x in s]
        p = [jnp.exp2(x - m).astype(BF16) for x, m in zip(s, mx)]
        r = [jnp.dot(x, v_ref[0:nk, sl], preferred_element_type=F32) for x, sl in zip(p, sls)]
        outs = [x / x[:, MLA_V:MLA_V + 1] for x in r]
        for pr in range(n_heads // 2):
            o_ref[:, pr * LANES:(pr + 1) * LANES] = jnp.where(
                lo, outs[2 * pr], pltpu.roll(outs[2 * pr + 1], MLA_V, 1)).astype(o_ref.dtype)

    @pl.when(t < ctx_tiles)
    def _():
        attend(n_ctx)

    @pl.when(t >= ctx_tiles)
    def _():
        attend(n_all)


def _mla_attention(q, k, v, *, n_ctx):
    b, rows, _ = q.shape
    hs = MLA_STEP_HEADS
    return pl.pallas_call(
        functools.partial(_mla_kernel, n_ctx=n_ctx, n_all=rows, ctx_tiles=n_ctx // TM),
        grid=(b, MLA_HEADS // hs, rows // TM),
        in_specs=[
            pl.BlockSpec((None, TM, hs * HEAD_PAD), lambda b_, h, t: (b_, t, h)),
            pl.BlockSpec((None, rows, hs * HEAD_PAD), lambda b_, h, t: (b_, 0, h)),
            pl.BlockSpec((None, rows, hs * HEAD_PAD), lambda b_, h, t: (b_, 0, h)),
        ],
        out_specs=pl.BlockSpec((None, TM, hs * MLA_V), lambda b_, h, t: (b_, t, h)),
        out_shape=jax.ShapeDtypeStruct((b, rows, MLA_VW), BF16),
        compiler_params=pltpu.CompilerParams(dimension_semantics=("parallel", "parallel", "arbitrary"),
                                             vmem_limit_bytes=VMEM_LIMIT),
        name="mla_attention",
    )(q, k, v)


def _gdn_prep_kernel(z_ref, zp_ref, zn_ref, cw_ref, sm_ref, nega_ref, dtb_ref,
                     q_ref, k_ref, v_ref, g_ref, ext_ref, *, ctx_tiles, n_tiles):
    t = pl.program_id(1)
    first = (t == 0) | (t == ctx_tiles)
    last = (t == ctx_tiles - 1) | (t == n_tiles - 1)
    half = GDN_CONV // 2
    ext_ref[0:8, :] = jnp.where(first, 0.0, zp_ref[...])
    ext_ref[8:8 + TM, :] = z_ref[...]
    ext_ref[8 + TM:16 + TM, :] = jnp.where(last, 0.0, zn_ref[...])
    cw = cw_ref[...]
    acc = z_ref[...] * cw[half:half + 1]
    for j in range(GDN_CONV):
        if j != half:
            acc = acc + ext_ref[8 - half + j:8 - half + j + TM, :] * cw[j:j + 1]
    qkv = _silu(acc)
    q = qkv[:, :GDN_QK]
    k = qkv[:, GDN_QK:2 * GDN_QK]
    q_ref[...] = q * lax.rsqrt(_seg_mean_sq(q, GDN_DK, 1.0) + EPS) * (GDN_DK ** -0.5)
    k_ref[...] = k * lax.rsqrt(_seg_mean_sq(k, GDN_DK, 1.0) + EPS)
    v_ref[...] = qkv[:, 2 * GDN_QK:]

    sm = sm_ref[...]
    lane = lax.broadcasted_iota(jnp.int32, sm.shape, 1)
    g = nega_ref[...] * _softplus(sm + dtb_ref[...])
    g = jnp.where(lane < 2 * GDN_HEADS, g, 0.0)
    gc_f = _dot_sel(_chunk_tri(TM, False), g)
    gc_b = _dot_sel(_chunk_tri(TM, True), g)
    g_ref[...] = jnp.where(lane < GDN_HEADS, gc_f, jnp.where(lane < 2 * GDN_HEADS, gc_b, _sigmoid(sm)))


def _gdn_prep(zg, small, conv_w, nega, dtb, *, ctx_tiles):
    b, rows, w = zg.shape
    n_tiles = rows // TM
    hb = TM // 8
    n_hblk = rows // 8
    return pl.pallas_call(
        functools.partial(_gdn_prep_kernel, ctx_tiles=ctx_tiles, n_tiles=n_tiles),
        grid=(b, n_tiles),
        in_specs=[
            _tok_spec(w),
            pl.BlockSpec((None, 8, w), lambda b_, t: (b_, jnp.maximum(t * hb - 1, 0), 0)),
            pl.BlockSpec((None, 8, w), lambda b_, t: (b_, jnp.minimum((t + 1) * hb, n_hblk - 1), 0)),
            _const_spec(conv_w.shape), _tok_spec(LANES), _const_spec((1, LANES)), _const_spec((1, LANES)),
        ],
        out_specs=[_tok_spec(GDN_QK), _tok_spec(GDN_QK), _tok_spec(GDN_VW), _tok_spec(LANES)],
        out_shape=[jax.ShapeDtypeStruct((b, rows, GDN_QK), F32), jax.ShapeDtypeStruct((b, rows, GDN_QK), F32),
                   jax.ShapeDtypeStruct((b, rows, GDN_VW), F32), jax.ShapeDtypeStruct((b, rows, LANES), F32)],
        scratch_shapes=[pltpu.VMEM((TM + 16, w), F32)],
        compiler_params=_params(),
        name="gdn_prep",
    )(zg, zg, zg, conv_w, small, nega, dtb)


def _bd_halves(y, bdm):
    yb = y.astype(BF16)
    out = []
    for s in range(GDN_TILES):
        t = jnp.concatenate([yb[:, s * GDN_HALF:(s + 1) * GDN_HALF]] * (GDN_HALF // GDN_DK), axis=0)
        out.append(jnp.where(bdm, t, jnp.zeros_like(t)))
    return out


def _hprod(x, bd):
    xb = x.astype(BF16)
    d = functools.partial(jnp.dot, preferred_element_type=F32)
    return jnp.concatenate([d(xb[:, s * GDN_HALF:(s + 1) * GDN_HALF], bd[s]) for s in range(GDN_TILES)], axis=1)


def _hprod3(x, y, bdm):
    xh, xl = _split2(x)
    yh, yl = _split2(y)
    bh, bl = _bd_halves(yh, bdm), _bd_halves(yl, bdm)
    return _hprod(xh, bh) + (_hprod(xh, bl) + _hprod(xl, bh))


def _gdn_chunks(chains):
    ii = lax.broadcasted_iota(jnp.int32, (CHUNK, GDN_QK), 0)
    jj = lax.broadcasted_iota(jnp.int32, (CHUNK, GDN_QK), 1) % CHUNK
    r = lax.broadcasted_iota(jnp.int32, (GDN_HALF, GDN_HALF), 0) // GDN_DK
    c = lax.broadcasted_iota(jnp.int32, (GDN_HALF, GDN_HALF), 1) // GDN_DK
    bdm = r == c
    diag = ii == jj
    eye = diag.astype(F32)
    n = len(chains)
    qs, ks, vs, gs, srefs, revs = (list(t) for t in zip(*chains))
    each = lambda f, *ls: [f(*a) for a in zip(*ls)]
    bd = lambda ys: [_bd_halves(y, bdm) for y in ys]

    incl = [(ii <= jj) if rv else (ii >= jj) for rv in revs]
    strict = [(ii < jj) if rv else (ii > jj) for rv in revs]
    last = [0 if rv else CHUNK - 1 for rv in revs]
    gc = [g[:, 0:GDN_QK] for g in gs]
    beta = [g[:, GDN_QK:2 * GDN_QK] for g in gs]
    gc_row = [jnp.sum(jnp.where(diag, x, 0.0), axis=0, keepdims=True) for x in gc]
    decay = each(lambda x, xr, ic: jnp.exp(jnp.where(ic, x - xr, -jnp.inf)), gc, gc_row, incl)

    def kt_bd(k):
        kt = k.T.astype(BF16)
        out = []
        for s in range(GDN_TILES):
            t = jnp.concatenate([kt[s * GDN_HALF:(s + 1) * GDN_HALF]] * (GDN_HALF // CHUNK), axis=1)
            out.append(jnp.where(bdm, t, jnp.zeros_like(t)))
        return out

    kkqk = each(lambda k, q, rk: _hprod(jnp.concatenate([k, q], axis=0), rk), ks, qs, [kt_bd(k) for k in ks])
    a = each(lambda st, b, x, dc: jnp.where(st, b * x[0:CHUNK] * dc, 0.0), strict, beta, kkqk, decay)
    qk = each(lambda x, dc: x[CHUNK:2 * CHUNK] * dc, kkqk, decay)

    def level_mask(blk, rv):
        same = (ii // (2 * blk)) == (jj // (2 * blk))
        off = (ii // blk) < (jj // blk) if rv else (ii // blk) > (jj // blk)
        return same & off

    m = each(lambda x, rv: eye - jnp.where(level_mask(1, rv), x, 0.0), a, revs)
    blk = 2
    while blk < CHUNK:
        am = each(lambda x, rv: jnp.where(level_mask(blk, rv), x, 0.0), a, revs)
        x = each(_hprod, am, bd(m))
        m = each(lambda mm, y, ybd: mm - _hprod(mm, ybd), m, x, bd(x))
        blk *= 2
    am = each(lambda x, mm: _hprod3(x, mm, bdm), a, m)
    resid = each(lambda mm, y: eye - mm - y, m, am)
    m = each(lambda mm, rbd: mm + _hprod(mm, rbd), m, bd(resid))

    egc = [jnp.exp(x) for x in gc]
    u = each(_hprod, m, bd(each(lambda v, b: v * b, vs, beta)))
    w = each(_hprod, m, bd(each(lambda k, b, e: k * (b * e), ks, beta, egc)))
    g_last = each(lambda x, l: x[l:l + 1, :], gc, last)
    st = [[sr[s] for s in range(GDN_TILES)] for sr in srefs]

    def sprod(x, tiles):
        return jnp.concatenate([_dot(x[:, s * GDN_HALF:(s + 1) * GDN_HALF], tiles[s]) for s in range(GDN_TILES)],
                               axis=1)

    ws = each(lambda ww, q, e, tiles: sprod(jnp.concatenate([ww, q * e], axis=0), tiles), w, qs, egc, st)
    v_new = each(lambda uu, x: uu - x[0:CHUNK], u, ws)
    o = each(lambda x, y, vbd: x[CHUNK:2 * CHUNK] + _hprod(y, vbd), ws, qk, bd(v_new))
    k_dec = each(lambda k, gl, x: k * jnp.exp(gl - x), ks, g_last, gc)
    e_last = [jnp.exp(gl) for gl in g_last]
    for i in range(n):
        for s in range(GDN_TILES):
            sl = slice(s * GDN_HALF, (s + 1) * GDN_HALF)
            srefs[i][s] = st[i][s] * e_last[i][:, sl] + jnp.where(
                bdm, _dot_tn(k_dec[i][:, sl], v_new[i][:, sl]), 0.0)
    return o


def _gdn_scan_kernel(qf_ref, kf_ref, vf_ref, gf_ref, qb_ref, kb_ref, vb_ref, gb_ref, of_ref, ob_ref, sf_ref, sb_ref):
    @pl.when(pl.program_id(1) == 0)
    def _():
        sf_ref[...] = jnp.zeros_like(sf_ref)
        sb_ref[...] = jnp.zeros_like(sb_ref)

    nb = qf_ref.shape[0]
    row = lax.broadcasted_iota(jnp.int32, (LANES, 2 * GDN_QK), 0)
    col = lax.broadcasted_iota(jnp.int32, (LANES, 2 * GDN_QK), 1)
    src = (col % GDN_QK) // GDN_DK + jnp.where(col >= GDN_QK, 2 * GDN_HEADS, 0)

    def spread(ref, d):
        stack = jnp.concatenate([ref[i] for i in range(nb)], axis=0)
        return _dot_sel_r(stack, (row == src + d * GDN_HEADS).astype(F32))

    gf, gb = spread(gf_ref, 0), spread(gb_ref, 1)
    chains = []
    for i in range(nb):
        rows = slice(i * CHUNK, (i + 1) * CHUNK)
        chains.append((qf_ref[i], kf_ref[i], vf_ref[i], gf[rows], sf_ref.at[i], False))
        chains.append((qb_ref[i], kb_ref[i], vb_ref[i], gb[rows], sb_ref.at[i], True))
    outs = _gdn_chunks(chains)
    for i in range(nb):
        of_ref[i] = outs[2 * i]
        ob_ref[i] = outs[2 * i + 1]


def _scan_chunk_maps(ctx_chunks, n_chunks):
    fwd = lambda s: s
    bwd = lambda s: jnp.where(s < ctx_chunks, ctx_chunks - 1 - s, n_chunks - 1 - (s - ctx_chunks))
    return fwd, bwd


def _gdn_scan(q, k, v, gates, *, n_ctx):
    b, rows, _ = q.shape
    n_chunks = rows // CHUNK
    fwd, bwd = _scan_chunk_maps(n_ctx // CHUNK, n_chunks)

    nb = GDN_SCAN_SAMPLES if b % GDN_SCAN_SAMPLES == 0 else 1

    def specs(cm):
        tok = lambda w: pl.BlockSpec((nb, CHUNK, w), lambda b_, s: (b_, cm(s), 0))
        return [tok(GDN_QK), tok(GDN_QK), tok(GDN_VW), tok(LANES)]

    return pl.pallas_call(
        _gdn_scan_kernel,
        grid=(b // nb, n_chunks),
        in_specs=specs(fwd) + specs(bwd),
        out_specs=[pl.BlockSpec((nb, CHUNK, GDN_VW), lambda b_, s: (b_, fwd(s), 0)),
                   pl.BlockSpec((nb, CHUNK, GDN_VW), lambda b_, s: (b_, bwd(s), 0))],
        out_shape=[jax.ShapeDtypeStruct((b, rows, GDN_VW), F32)] * 2,
        scratch_shapes=[pltpu.VMEM((nb, GDN_TILES, GDN_HALF, GDN_HALF), F32)] * 2,
        compiler_params=_params(),
        name="gdn_scan",
    )(q, k, v, gates, q, k, v, gates)


def _outproj_ffn_kernel(x_ref, mod_ref, oa_ref, of_ref, ob_ref, gate_ref, gn_ref, w_ref, g_ref, wg_ref, wu_ref, wd_ref,
                        o_ref, *, seg, rec_first):
    m = mod_ref[...]
    row = lambda r: m[r:r + 1]
    o = of_ref[...] + ob_ref[...]
    y = o * lax.rsqrt(_seg_mean_sq(o, seg, float(seg)) + EPS) * gn_ref[...] * _silu(gate_ref[...])
    wr = y.shape[-1]
    wa = oa_ref.shape[-1]
    if rec_first:
        out = jnp.dot(y.astype(BF16), w_ref[0:wr, :], preferred_element_type=F32)
        out = out + jnp.dot(oa_ref[...], w_ref[wr:wr + wa, :], preferred_element_type=F32)
    else:
        out = jnp.dot(oa_ref[...], w_ref[0:wa, :], preferred_element_type=F32)
        out = out + jnp.dot(y.astype(BF16), w_ref[wa:wa + wr, :], preferred_element_type=F32)
    x1 = x_ref[...] + row(5) * out
    hb = _rms_rows(x1, g_ref[...]) * (1.0 + row(7)) + row(6)
    o_ref[...] = x1 + (FFN_RES * row(8)) * _swiglu(hb, wg_ref, wu_ref, wd_ref)


def _outproj_ffn(xs, mod, o_att, o_f, o_b, gate, gn, w_out, g, wg, wu, wd, *, layer, seg, rec_first, n_ctx,
                 latent_only):
    b, rows, d = xs.shape
    tm = TM
    ctx_tiles = n_ctx // tm
    skip = ctx_tiles if latent_only else 0
    out_rows = rows - skip * tm
    tok = lambda w, sk: pl.BlockSpec((None, tm, w), lambda b_, t: (b_, t + sk, 0))
    wspec = lambda w: pl.BlockSpec((None, None) + w.shape[2:], lambda b_, t: (layer, 1, 0, 0),
                                   pipeline_mode=pl.Buffered(1))
    return pl.pallas_call(
        functools.partial(_outproj_ffn_kernel, seg=seg, rec_first=rec_first),
        grid=(b, out_rows // tm),
        in_specs=[tok(d, skip),
                  pl.BlockSpec((None, None, N_MOD, d), lambda b_, t: (b_, jnp.where(t + skip >= ctx_tiles, 1, 0), 0, 0)),
                  tok(o_att.shape[-1], 0), tok(o_f.shape[-1], skip), tok(o_b.shape[-1], skip),
                  tok(gate.shape[-1], skip), _const_spec(gn.shape), _const_spec(w_out.shape), _const_spec((1, d)),
                  wspec(wg), wspec(wu), wspec(wd)],
        out_specs=tok(d, 0),
        out_shape=jax.ShapeDtypeStruct((b, out_rows, d), F32),
        compiler_params=_params(),
        name="outproj_ffn",
    )(xs, mod, o_att, o_f, o_b, gate, gn, w_out, g.reshape(1, d), wg, wu, wd)


def _inproj_odd_kernel(x_ref, mod_ref, g_ref, win_ref, w2_ref, b2_ref, qn_ref, kn_ref, rc_ref, rs1_ref, rs2_ref,
                       gq_ref, gk_ref, gv_ref, rg_ref, bc_ref, sq_ref, sk_ref, sv_ref):
    x = x_ref[...]
    m = mod_ref[...]
    h = _rms_rows(x, g_ref[...]) * (1.0 + m[4:5]) + m[3:4]
    hb = h.astype(BF16)
    n_pass = 2 * GLA_QK + 2 * GLA_VW
    z = jnp.dot(hb, win_ref[:, n_pass:], preferred_element_type=F32)
    o = 0
    sq = z[:, o:o + SWA_QW]; o += SWA_QW
    sk = z[:, o:o + SWA_KW]; o += SWA_KW
    sv_ref[...] = z[:, o:o + SWA_KW].astype(sv_ref.dtype); o += SWA_KW
    lowrank = z[:, o:o + LANES]
    zp = jnp.dot(hb, win_ref[:, 0:n_pass], preferred_element_type=F32)
    o = 0
    gq_ref[...] = zp[:, o:o + GLA_QK]; o += GLA_QK
    gk_ref[...] = zp[:, o:o + GLA_QK]; o += GLA_QK
    gv_ref[...] = zp[:, o:o + GLA_VW]; o += GLA_VW
    rg_ref[...] = zp[:, o:o + GLA_VW]

    logit = _dot(lowrank, w2_ref[...]) + b2_ref[...]
    log_a = (jnp.minimum(logit, 0.0) - jnp.log1p(jnp.exp(-jnp.abs(logit)))) * (1.0 / GLA_TAU)
    bc_ref[:, 0:GLA_QK] = _dot_sel(_chunk_tri(TM, False), log_a[:, 0:GLA_QK])
    bc_ref[:, GLA_QK:2 * GLA_QK] = _dot_sel(_chunk_tri(TM, True), log_a[:, GLA_QK:2 * GLA_QK])

    rc, rs1, rs2 = rc_ref[...], rs1_ref[...], rs2_ref[...]
    sqn = sq * lax.rsqrt(_seg_mean_sq(sq, SWA_DH, float(SWA_DH)) + EPS) * (qn_ref[...] * SWA_Q_PRESCALE)
    for s in range(SWA_QW // LANES):
        sl = slice(s * LANES, (s + 1) * LANES)
        sq_ref[:, sl] = _rope(sqn[:, sl], rc, rs1, rs2, SWA_DH // 2).astype(sq_ref.dtype)
    skn = sk * lax.rsqrt(_seg_mean_sq(sk, SWA_DH, float(SWA_DH)) + EPS) * kn_ref[...]
    sk_ref[...] = _rope(skn, rc, rs1, rs2, SWA_DH // 2).astype(sk_ref.dtype)


def _inproj_odd(xs, mod, g, p, rope):
    b, rows, d = xs.shape
    consts = [g.reshape(1, d), p["w_in"], p["w2"], p["b2"], p["q_norm"], p["k_norm"]]
    rope_spec = pl.BlockSpec((TM, LANES), lambda b_, t: (t, 0))
    widths = [(GLA_QK, F32), (GLA_QK, F32), (GLA_VW, F32), (GLA_VW, F32), (2 * GLA_QK, F32),
              (SWA_QW, BF16), (SWA_KW, BF16), (SWA_KW, BF16)]
    return pl.pallas_call(
        _inproj_odd_kernel,
        grid=(b, rows // TM),
        in_specs=[_tok_spec(d), _mod_spec(p["ctx_tiles"])] + [_const_spec(c.shape) for c in consts] + [rope_spec] * 3,
        out_specs=[_tok_spec(w) for w, _ in widths],
        out_shape=[jax.ShapeDtypeStruct((b, rows, w), dt) for w, dt in widths],
        compiler_params=_params(),
        name="inproj_odd",
    )(xs, mod, *consts, *rope)


def _gla_chunk(q, k, v, bc, st_ref, reverse):
    def blk(shape, rdiv, cdiv):
        return (lax.broadcasted_iota(jnp.int32, shape, 0) // rdiv) == (lax.broadcasted_iota(jnp.int32, shape, 1) // cdiv)

    ii = lax.broadcasted_iota(jnp.int32, (CHUNK, GLA_QK), 0)
    jj = lax.broadcasted_iota(jnp.int32, (CHUNK, GLA_QK), 1) % CHUNK
    incl = (ii <= jj) if reverse else (ii >= jj)
    last = 0 if reverse else CHUNK - 1
    q_dec = (q * (GLA_DK ** -0.5)) * jnp.exp(bc)
    k_inv = k * jnp.exp(-bc)
    b_last = bc[last:last + 1, :]
    k_dec = k * jnp.exp(b_last - bc)
    e_last = jnp.exp(b_last)
    kt = jnp.concatenate([k_inv.T.astype(BF16)] * GLA_HEADS, axis=1)
    rk = jnp.where(blk((GLA_QK, GLA_QK), GLA_DK, CHUNK), kt, jnp.zeros_like(kt))
    attn = jnp.where(incl, jnp.dot(q_dec.astype(BF16), rk, preferred_element_type=F32), 0.0)
    vt = jnp.concatenate([v.astype(BF16)] * GLA_HEADS, axis=0)
    vbd = jnp.where(blk((GLA_QK, GLA_VW), CHUNK, GLA_DV), vt, jnp.zeros_like(vt))
    st = st_ref[...]
    o = jnp.dot(attn.astype(BF16), vbd, preferred_element_type=F32) + _dot_nt(q_dec, st)
    st_ref[...] = st * e_last + jnp.where(blk((GLA_VW, GLA_QK), GLA_DV, GLA_DK), _dot_tn(v, k_dec), 0.0)
    return o


def _gla_scan_kernel(qf_ref, kf_ref, vf_ref, bf_ref, qb_ref, kb_ref, vb_ref, bb_ref, of_ref, ob_ref, sf_ref, sb_ref):
    @pl.when(pl.program_id(1) == 0)
    def _():
        sf_ref[...] = jnp.zeros_like(sf_ref)
        sb_ref[...] = jnp.zeros_like(sb_ref)

    for i in range(qf_ref.shape[0]):
        of_ref[i] = _gla_chunk(qf_ref[i], kf_ref[i], vf_ref[i], bf_ref[i][:, 0:GLA_QK], sf_ref.at[i], False)
        ob_ref[i] = _gla_chunk(qb_ref[i], kb_ref[i], vb_ref[i], bb_ref[i][:, GLA_QK:2 * GLA_QK], sb_ref.at[i], True)


def _gla_scan(q, k, v, bc, *, n_ctx):
    b, rows, _ = q.shape
    n_chunks = rows // CHUNK
    fwd, bwd = _scan_chunk_maps(n_ctx // CHUNK, n_chunks)
    nb = GLA_SCAN_SAMPLES if b % GLA_SCAN_SAMPLES == 0 else 1

    def specs(cm):
        tok = lambda w: pl.BlockSpec((nb, CHUNK, w), lambda b_, s: (b_, cm(s), 0))
        return [tok(GLA_QK), tok(GLA_QK), tok(GLA_VW), tok(2 * GLA_QK)]

    return pl.pallas_call(
        _gla_scan_kernel,
        grid=(b // nb, n_chunks),
        in_specs=specs(fwd) + specs(bwd),
        out_specs=[pl.BlockSpec((nb, CHUNK, GLA_VW), lambda b_, s: (b_, fwd(s), 0)),
                   pl.BlockSpec((nb, CHUNK, GLA_VW), lambda b_, s: (b_, bwd(s), 0))],
        out_shape=[jax.ShapeDtypeStruct((b, rows, GLA_VW), F32)] * 2,
        scratch_shapes=[pltpu.VMEM((nb, GLA_VW, GLA_QK), F32)] * 2,
        compiler_params=_params(),
        name="gla_scan",
    )(q, k, v, bc, q, k, v, bc)


def _swa_kernel(q_ref, k_ref, v_ref, sink_ref, o_ref, *, n_ctx, n_lat):
    w = SWA_WINDOW
    n = pl.program_id(1)
    nb = n_lat // w

    def rows(ref, blk):
        return ref[pl.ds(pl.multiple_of(n_ctx + blk * w, w), w), :]

    pb = jnp.maximum(n - 1, 0)
    xb = jnp.minimum(n + 1, nb - 1)
    k_cat = jnp.concatenate([rows(k_ref, pb), rows(k_ref, n), rows(k_ref, xb), k_ref[0:n_ctx, :]], axis=0)
    v_cat = jnp.concatenate([rows(v_ref, pb), rows(v_ref, n), rows(v_ref, xb), v_ref[0:n_ctx, :]], axis=0)
    v_one = jnp.concatenate([v_cat, jnp.ones_like(v_cat)], axis=1)
    half = SWA_HEADS // 2
    ii = lax.broadcasted_iota(jnp.int32, (half * w, w), 0) % w
    jj = lax.broadcasted_iota(jnp.int32, (half * w, w), 1)
    ninf = -jnp.inf
    bias_prev = jnp.where((jj >= ii) & (n > 0), 0.0, ninf)
    bias_next = jnp.where((jj <= ii) & (n < nb - 1), 0.0, ninf)
    lo = lax.broadcasted_iota(jnp.int32, (half * w, LANES), 1) < SWA_DH
    q = q_ref[...]
    qst = jnp.concatenate([q[:, j * LANES:(j + 1) * LANES] for j in range(half)], axis=0)
    sink = sink_ref[...]
    dims = (((1,), (1,)), ((), ()))
    qm = [jnp.where(lo if g == 0 else jnp.logical_not(lo), qst, jnp.zeros_like(qst)) for g in range(SWA_KV_HEADS)]
    s = [lax.dot_general(x, k_cat, dims, preferred_element_type=F32) for x in qm]
    parts = [[x[:, 0:w] + bias_prev, x[:, w:2 * w], x[:, 2 * w:3 * w] + bias_next, x[:, 3 * w:]] for x in s]
    sk = [jnp.concatenate([jnp.broadcast_to(sink[0:1, half * g + j:half * g + j + 1], (w, 1)) for j in range(half)],
                          axis=0) for g in range(SWA_KV_HEADS)]
    def row_max(ps, k_):
        slabs = [x[:, c * LANES:(c + 1) * LANES] for x in ps for c in range(x.shape[1] // LANES)]
        return jnp.maximum(jnp.max(functools.reduce(jnp.maximum, slabs), axis=-1, keepdims=True), k_)

    mx = [row_max(ps, k_) for ps, k_ in zip(parts, sk)]
    p = [jnp.concatenate([jnp.exp2(x - m).astype(BF16) for x in ps], axis=1) for ps, m in zip(parts, mx)]
    r = [jnp.dot(x, v_one, preferred_element_type=F32) for x in p]
    res = [x[:, 0:LANES] / (x[:, LANES:2 * LANES] + jnp.exp2(k_ - m)) for x, k_, m in zip(r, sk, mx)]
    o = jnp.where(lo, res[0], res[1]).astype(o_ref.dtype)
    for j in range(half):
        o_ref[:, j * LANES:(j + 1) * LANES] = o[j * w:(j + 1) * w, :]


def _swa_attention(q, k, v, sink, *, n_ctx):
    b, rows, _ = q.shape
    n_lat = rows - n_ctx
    w = SWA_WINDOW
    skip = n_ctx // w
    return pl.pallas_call(
        functools.partial(_swa_kernel, n_ctx=n_ctx, n_lat=n_lat),
        grid=(b, n_lat // w),
        in_specs=[
            pl.BlockSpec((None, w, SWA_QW), lambda b_, n: (b_, n + skip, 0)),
            pl.BlockSpec((None, rows, SWA_KW), lambda b_, n: (b_, 0, 0)),
            pl.BlockSpec((None, rows, SWA_KW), lambda b_, n: (b_, 0, 0)),
            _const_spec((1, LANES)),
        ],
        out_specs=pl.BlockSpec((None, w, SWA_QW), lambda b_, n: (b_, n, 0)),
        out_shape=jax.ShapeDtypeStruct((b, n_lat, SWA_QW), BF16),
        compiler_params=_params(),
        name="swa_attention",
    )(q, k, v, sink)


def _rope_tables(n_lat, n_ctx, rot_dim):
    t = jnp.arange(n_lat)
    row = (t // GRID_W).astype(F32)
    col = (t % GRID_W).astype(F32)
    n_freq = rot_dim // 4
    inv = ROPE_THETA ** (-jnp.arange(n_freq, dtype=F32) / n_freq)
    ang = jnp.concatenate([row[:, None] * inv, col[:, None] * inv], axis=-1)
    half = rot_dim // 2
    cos = jnp.concatenate([jnp.ones((n_ctx, half), F32), jnp.cos(ang)], axis=0)
    sin = jnp.concatenate([jnp.zeros((n_ctx, half), F32), jnp.sin(ang)], axis=0)
    rows = n_ctx + n_lat
    one = lambda w: jnp.ones((rows, w), F32)
    zero = lambda w: jnp.zeros((rows, w), F32)
    if rot_dim == MLA_ROPE:
        c = jnp.concatenate([one(MLA_NOPE), cos, cos, one(HEAD_PAD - MLA_QK)], axis=1)
        s1 = jnp.concatenate([zero(MLA_NOPE), -sin, zero(half), zero(HEAD_PAD - MLA_QK)], axis=1)
        s2 = jnp.concatenate([zero(MLA_NOPE), zero(half), sin, zero(HEAD_PAD - MLA_QK)], axis=1)
    else:
        c = jnp.concatenate([cos, cos, cos, cos], axis=1)
        s1 = jnp.concatenate([-sin, zero(half), -sin, zero(half)], axis=1)
        s2 = jnp.concatenate([zero(half), sin, zero(half), sin], axis=1)
    return c, s1, s2


def _even_params(j, n_ctx, ev_w_in, ev_q_a_norm, ev_w_q_up, ev_kv_a_norm, ev_w_kv_up, ev_mla_q_norm, ev_mla_k_norm,
                 ev_gdn_conv, ev_gdn_a_log, ev_gdn_dt_bias, ev_gdn_out_norm, ev_w_out):
    w = ev_w_in[j]
    d = w.shape[0]
    z = lambda n: jnp.zeros((d, n), F32)
    o_kr = MLA_Q_RANK + MLA_KV_RANK
    o_g = o_kr + MLA_ROPE
    o_small = o_g + 3 * GDN_QK
    o_gate = o_small + 4 * GDN_HEADS
    w_in = jnp.concatenate([
        w[:, :o_kr], z(MLA_NOPE), w[:, o_kr:o_g], z(HEAD_PAD - MLA_QK),
        w[:, o_g:o_small], w[:, o_gate:o_gate + GDN_VW],
        w[:, o_small:o_gate], z(LANES - 4 * GDN_HEADS)], axis=1).astype(BF16)
    pad_h = HEAD_PAD - MLA_QK
    w_q = jnp.pad(ev_w_q_up[j].reshape(MLA_Q_RANK, MLA_HEADS, MLA_QK), ((0, 0), (0, 0), (0, pad_h)))
    wkv = ev_w_kv_up[j].reshape(MLA_KV_RANK, MLA_HEADS, MLA_NOPE + MLA_V)
    w_kk = jnp.pad(wkv[:, :, :MLA_NOPE], ((0, 0), (0, 0), (0, HEAD_PAD - MLA_NOPE)))
    lane_row = lambda vec: jnp.pad(vec, (0, LANES - vec.shape[0])).reshape(1, LANES)
    fb = lambda a: jnp.concatenate([a[0], a[1]])
    return {
        "ctx_tiles": n_ctx // TM,
        "w_in": w_in,
        "q_a_norm": ev_q_a_norm[j].reshape(1, -1),
        "w_q": w_q.reshape(MLA_Q_RANK, MLA_HEADS * HEAD_PAD).astype(BF16),
        "kv_a_norm": ev_kv_a_norm[j].reshape(1, -1),
        "w_kk": w_kk.reshape(MLA_KV_RANK, MLA_HEADS * HEAD_PAD).astype(BF16),
        "w_kv": jnp.pad(wkv[:, :, MLA_NOPE:], ((0, 0), (0, 0), (0, HEAD_PAD - MLA_V))
                        ).reshape(MLA_KV_RANK, MLA_HEADS * HEAD_PAD).astype(BF16),
        "q_norm": lane_row(ev_mla_q_norm[j]),
        "k_norm": lane_row(ev_mla_k_norm[j]),
        "conv_w": ev_gdn_conv[j],
        "neg_a": lane_row(-jnp.exp(fb(ev_gdn_a_log[j]))),
        "dt_bias": lane_row(fb(ev_gdn_dt_bias[j])),
        "out_norm": jnp.tile(ev_gdn_out_norm[j], GDN_HEADS).reshape(1, GDN_VW),
        "w_out": ev_w_out[j].astype(BF16),
    }


def _swa_head_perm():
    half = SWA_HEADS // 2
    heads = [h for j in range(half) for h in (j, half + j)]
    return jnp.concatenate([jnp.arange(SWA_DH) + h * SWA_DH for h in heads])


def _odd_params(j, n_ctx, od_w_in, od_gla_gate_w2, od_gla_gate_b, od_gla_out_norm, od_swa_q_norm, od_swa_k_norm,
                od_swa_sink, od_w_out):
    w = od_w_in[j]
    d = w.shape[0]
    o_gate = 2 * GLA_QK + GLA_VW
    o_rg = o_gate + 2 * GLA_RANK
    o_sq = o_rg + GLA_VW
    o_sk = o_sq + SWA_QW
    perm = _swa_head_perm()
    w_in = jnp.concatenate([
        w[:, :o_gate], w[:, o_rg:o_sq], w[:, o_sq:o_sk][:, perm], w[:, o_sk:],
        w[:, o_gate:o_rg], jnp.zeros((d, LANES - 2 * GLA_RANK), F32)], axis=1).astype(BF16)
    w2 = jnp.zeros((LANES, 2 * GLA_QK), F32)
    w2 = w2.at[0:GLA_RANK, 0:GLA_QK].set(od_gla_gate_w2[j, 0])
    w2 = w2.at[GLA_RANK:2 * GLA_RANK, GLA_QK:].set(od_gla_gate_w2[j, 1])
    wo = od_w_out[j]
    w_out = jnp.concatenate([wo[:GLA_VW], wo[GLA_VW:][perm]], axis=0).astype(BF16)
    return {
        "ctx_tiles": n_ctx // TM,
        "w_in": w_in,
        "w2": w2.astype(BF16),
        "b2": jnp.concatenate([od_gla_gate_b[j, 0], od_gla_gate_b[j, 1]]).reshape(1, 2 * GLA_QK),
        "q_norm": jnp.tile(od_swa_q_norm[j], SWA_HEADS).reshape(1, SWA_QW),
        "k_norm": jnp.tile(od_swa_k_norm[j], SWA_KV_HEADS).reshape(1, SWA_KW),
        "sink": jnp.pad(od_swa_sink[j] * math.log2(math.e), (0, LANES - SWA_HEADS)).reshape(1, LANES),
        "out_norm": jnp.tile(od_gla_out_norm[j], GLA_HEADS).reshape(1, GLA_VW),
        "w_out": w_out,
    }


def kernel(x, c, ctx, c_ctx, ada_w, ada_b, norm_g, ffn_w_gate, ffn_w_up, ffn_w_down, ev_w_in, ev_q_a_norm, ev_w_q_up, ev_kv_a_norm, ev_w_kv_up, ev_mla_q_norm, ev_mla_k_norm, ev_gdn_conv, ev_gdn_a_log, ev_gdn_dt_bias, ev_gdn_out_norm, ev_w_out, od_w_in, od_gla_gate_w2, od_gla_gate_b, od_gla_out_norm, od_swa_q_norm, od_swa_k_norm, od_swa_sink, od_w_out):
    b, n_lat, d = x.shape
    n_ctx = ctx.shape[1]
    depth = ada_w.shape[0]
    assert d == D_MODEL and n_ctx % TM == 0 and n_lat % TM == 0 and n_lat % GRID_W == 0
    assert depth % 2 == 0, "the last layer must be an odd (GLA/SWA) layer: context outputs of that mixer are not built"
    ctx_tiles = n_ctx // TM

    cvec = jnp.concatenate([c, c_ctx[None, :], jnp.zeros((16 - b - 1, d), F32)], axis=0)
    mod_all = _ada(cvec, ada_w, ada_b).reshape(depth, 16, N_MOD, d)
    rope_mla = _rope_tables(n_lat, n_ctx, MLA_ROPE)
    rope_swa = _rope_tables(n_lat, n_ctx, SWA_DH)

    wg, wu, wd = ffn_w_gate, ffn_w_up, ffn_w_down
    xs = x
    for i in range(depth):
        last = i == depth - 1
        mod = jnp.stack([jnp.broadcast_to(mod_all[i, b][None], (b, N_MOD, d)), mod_all[i, :b]], axis=1)
        xs = _ffn(xs, mod, norm_g[i, 0], wg, wu, wd, layer=i, half=0, r0=0, n_ctx=n_ctx, ctx=ctx if i == 0 else None)
        j = i // 2
        if i % 2 == 0:
            p = _even_params(j, n_ctx, ev_w_in, ev_q_a_norm, ev_w_q_up, ev_kv_a_norm, ev_w_kv_up, ev_mla_q_norm,
                             ev_mla_k_norm, ev_gdn_conv, ev_gdn_a_log, ev_gdn_dt_bias, ev_gdn_out_norm, ev_w_out)
            q, k, v, zg, gate, small = _inproj_even(xs, mod, norm_g[i, 1], p, rope_mla)
            o_att = _mla_attention(q, k, v, n_ctx=n_ctx)
            gq, gk, gv, gates = _gdn_prep(zg, small, p["conv_w"], p["neg_a"], p["dt_bias"], ctx_tiles=ctx_tiles)
            o_f, o_b = _gdn_scan(gq, gk, gv, gates, n_ctx=n_ctx)
            assert not last, "an even last layer would hand a joint o_att to the latent-only output stage"
            mix = dict(o_att=o_att, o_f=o_f, o_b=o_b, gate=gate, seg=GDN_DV, rec_first=False)
        else:
            assert last, "odd layers that must also produce context outputs are not built"
            p = _odd_params(j, n_ctx, od_w_in, od_gla_gate_w2, od_gla_gate_b, od_gla_out_norm, od_swa_q_norm,
                            od_swa_k_norm, od_swa_sink, od_w_out)
            gq, gk, gv, rg, bc, sq, sk, sv = _inproj_odd(xs, mod, norm_g[i, 1], p, rope_swa)
            o_f, o_b = _gla_scan(gq, gk, gv, bc, n_ctx=n_ctx)
            o_att = _swa_attention(sq, sk, sv, p["sink"], n_ctx=n_ctx)
            mix = dict(o_att=o_att, o_f=o_f, o_b=o_b, gate=rg, seg=GLA_DV, rec_first=True)
        xs = _outproj_ffn(xs, mod, mix["o_att"], mix["o_f"], mix["o_b"], mix["gate"], p["out_norm"], p["w_out"],
                          norm_g[i, 2], wg, wu, wd, layer=i, seg=mix["seg"], rec_first=mix["rec_first"],
                          n_ctx=n_ctx, latent_only=last)
    return xs
```

```python
import functools
import math

import jax
import jax.numpy as jnp
from jax import lax
from jax.experimental import pallas as pl
from jax.experimental.pallas import tpu as pltpu

F32 = jnp.float32
BF16 = jnp.bfloat16

D_MODEL = 1024
GRID_W = 64
D_FF = 2816
FFN_RES = 0.5
N_MOD = 9
EPS = 1e-6
ROPE_THETA = 10000.0
CHUNK = 64

MLA_HEADS = 8
MLA_NOPE = 64
MLA_ROPE = 32
MLA_QK = MLA_NOPE + MLA_ROPE
MLA_V = 64
MLA_Q_RANK = 384
MLA_KV_RANK = 256
MLA_VW = MLA_HEADS * MLA_V
MLA_STEP_HEADS = 4
MLA_Q_PRESCALE =MLA_QK ** -0.5 * math.log2(math.e)

GDN_HEADS = 8
GDN_DK = 64
GDN_DV = 64
GDN_CONV = 5
GDN_QK = GDN_HEADS * GDN_DK
GDN_VW = GDN_HEADS * GDN_DV
GDN_HALF = 128
GDN_TILES = GDN_QK // GDN_HALF
GDN_SCAN_SAMPLES = 8

GLA_HEADS = 4
GLA_DK = 64
GLA_DV = 128
GLA_RANK = 16
GLA_TAU = 16.0
GLA_QK = GLA_HEADS * GLA_DK
GLA_VW = GLA_HEADS * GLA_DV
GLA_SCAN_SAMPLES = 8

SWA_HEADS = 8
SWA_KV_HEADS = 2
SWA_DH = 64
SWA_WINDOW = 128
SWA_QW = SWA_HEADS * SWA_DH
SWA_KW = SWA_KV_HEADS * SWA_DH
SWA_Q_PRESCALE = SWA_DH ** -0.5 * math.log2(math.e)

LANES = 128
HEAD_PAD = 128
TM = 256
FFN_TILES = (576, 512, 256, 128)
FFN_SPLIT = 1536
VMEM_LIMIT =56 * 1024 * 1024

EVEN_COLS = MLA_Q_RANK + MLA_KV_RANK + LANES + 2 * GDN_QK + GDN_VW + GDN_VW + LANES
ODD_COLS = 2 * GLA_QK + GLA_VW + GLA_VW + SWA_QW + 2 * SWA_KW + LANES


def _dot(a, b):
    return jnp.dot(a.astype(BF16), b.astype(BF16), preferred_element_type=F32)


def _dot_nt(a, b):
    return lax.dot_general(a.astype(BF16), b.astype(BF16), (((1,), (1,)), ((), ())), preferred_element_type=F32)


def _dot_tn(a, b):
    return lax.dot_general(a.astype(BF16), b.astype(BF16), (((0,), (0,)), ((), ())), preferred_element_type=F32)


def _split2(x):
    hi = x.astype(BF16)
    lo = (x - hi.astype(F32)).astype(BF16)
    return hi, lo


def _split3(x):
    hi = x.astype(BF16)
    r = x - hi.astype(F32)
    mid = r.astype(BF16)
    lo = (r - mid.astype(F32)).astype(BF16)
    return hi, mid, lo


def _dot3(a, b):
    ah, al = _split2(a)
    bh, bl = _split2(b)
    d = functools.partial(jnp.dot, preferred_element_type=F32)
    return d(ah, bh) + (d(ah, bl) + d(al, bh))


def _dot_sel(sel, x):
    s = sel.astype(BF16)
    hi, mid, lo = _split3(x)
    d = functools.partial(jnp.dot, preferred_element_type=F32)
    return d(s, hi) + (d(s, mid) + d(s, lo))


def _dot_sel_r(x, sel):
    s = sel.astype(BF16)
    hi, mid, lo = _split3(x)
    d = functools.partial(jnp.dot, preferred_element_type=F32)
    return d(hi, s) + (d(mid, s) + d(lo, s))


def _seg_mean_sq(x, seg, n_real):
    w = x.shape[-1]
    r = lax.broadcasted_iota(jnp.int32, (w, w), 0) // seg
    c = lax.broadcasted_iota(jnp.int32, (w, w), 1) // seg
    ones_bd = (r == c).astype(BF16)
    hi, lo = _split2(x * x)
    d = functools.partial(jnp.dot, preferred_element_type=F32)
    return (d(hi, ones_bd) + d(lo, ones_bd)) * (1.0 / n_real)


def _rms_rows(x, g):
    ms = jnp.mean(x * x, axis=-1, keepdims=True)
    return x * lax.rsqrt(ms + EPS) * g


def _sigmoid(x):
    return 1.0 / (1.0 + jnp.exp(-x))


def _silu(x):
    return x * _sigmoid(x)


def _softplus(x):
    return jnp.maximum(x, 0.0) + jnp.log1p(jnp.exp(-jnp.abs(x)))


def _rope(x, c, s1, s2, half):
    return x * c + pltpu.roll(x, LANES - half, 1) * s1 + pltpu.roll(x, half, 1) * s2


def _chunk_tri(n, reverse):
    i = lax.broadcasted_iota(jnp.int32, (n, n), 0)
    j = lax.broadcasted_iota(jnp.int32, (n, n), 1)
    same = (i // CHUNK) == (j // CHUNK)
    tri = (j >= i) if reverse else (j <= i)
    return (same & tri).astype(F32)


def _ada_kernel(c_ref, w_ref, b_ref, o_ref):
    sc = _silu(c_ref[...])
    o_ref[...] = _dot(sc, w_ref[...]) + b_ref[...]


def _ada(cvec, ada_w, ada_b):
    depth, d, nm = ada_w.shape
    tn = 1024
    return pl.pallas_call(
        _ada_kernel,
        grid=(depth, nm // tn),
        in_specs=[
            pl.BlockSpec(cvec.shape, lambda i, j: (0, 0)),
            pl.BlockSpec((None, d, tn), lambda i, j: (i, 0, j)),
            pl.BlockSpec((None, 1, tn), lambda i, j: (i, 0, j)),
        ],
        out_specs=pl.BlockSpec((None, cvec.shape[0], tn), lambda i, j: (i, 0, j)),
        out_shape=jax.ShapeDtypeStruct((depth, cvec.shape[0], nm), F32),
        compiler_params=pltpu.CompilerParams(vmem_limit_bytes=VMEM_LIMIT),
        name="ada_mod",
    )(cvec, ada_w, ada_b.reshape(depth, 1, nm))


def _const_spec(shape):
    nd = len(shape)
    return pl.BlockSpec(shape, lambda *_: (0,) * nd, pipeline_mode=pl.Buffered(1))


def _tok_spec(width):
    return pl.BlockSpec((None, TM, width), lambda b, t: (b, t, 0))


def _mod_spec(ctx_tiles):
    return pl.BlockSpec((None, None, N_MOD, D_MODEL), lambda b, t: (b, jnp.where(t >= ctx_tiles, 1, 0), 0, 0))


def _params():
    return pltpu.CompilerParams(dimension_semantics=("parallel", "arbitrary"), vmem_limit_bytes=VMEM_LIMIT)


def _swiglu(hb, wg_ref, wu_ref, wd_ref):
    f = wg_ref.shape[1]
    y = None
    for lo, hi in ((0, FFN_SPLIT), (FFN_SPLIT, f)):
        a = jnp.dot(hb, wg_ref[:, lo:hi], preferred_element_type=F32)
        u = jnp.dot(hb, wu_ref[:, lo:hi], preferred_element_type=F32)
        part = jnp.dot(_silu(a) * u, wd_ref[lo:hi, :], preferred_element_type=F32)
        y = part if y is None else y + part
    return y


def _ffn_kernel(*refs, r0, n_ctx, split):
    if split:
        c_ref, x_ref, mod_ref, g_ref, wg_ref, wu_ref, wd_ref, o_ref = refs
    else:
        x_ref, mod_ref, g_ref, wg_ref, wu_ref, wd_ref, o_ref = refs
    x = x_ref[...]
    m = mod_ref[...]
    tm = x.shape[0]
    if n_ctx:
        is_ctx = pl.program_id(1) * tm + lax.broadcasted_iota(jnp.int32, (tm, 1), 0) < n_ctx
        row = lambda r: jnp.where(is_ctx, m[0, r:r + 1], m[1, r:r + 1])
        if split:
            x = jnp.where(is_ctx, c_ref[...], x)
    else:
        row = lambda r: m[1, r:r + 1]
    h = _rms_rows(x, g_ref[...]) * (1.0 + row(r0 + 1)) + row(r0)
    y = _swiglu(h, wg_ref, wu_ref, wd_ref)
    o_ref[...] = x + (FFN_RES * row(r0 + 2)) * y


def _ffn(xs, mod, g, wg, wu, wd, *, layer, half, r0, n_ctx, ctx=None):
    b, rows, d = xs.shape
    split = ctx is not None
    if split:
        tm = TM
        ctx_tiles = n_ctx // tm
        rows += n_ctx
        toks = [pl.BlockSpec((None, tm, d), lambda b_, t: (b_, jnp.minimum(t, ctx_tiles - 1), 0)),
                pl.BlockSpec((None, tm, d), lambda b_, t: (b_, jnp.maximum(t - ctx_tiles, 0), 0))]
        data = [ctx, xs]
    else:
        tm = next(t for t in FFN_TILES if rows % t == 0)
        toks = [pl.BlockSpec((None, tm, d), lambda b_, t: (b_, t, 0))]
        data = [xs]
    wspec = lambda w: pl.BlockSpec((None, None) + w.shape[2:], lambda b_, t: (layer, half, 0, 0),
                                   pipeline_mode=pl.Buffered(1))
    return pl.pallas_call(
        functools.partial(_ffn_kernel, r0=r0, n_ctx=n_ctx, split=split),
        grid=(b, rows // tm),
        in_specs=toks + [pl.BlockSpec((None, 2, N_MOD, d), lambda b_, t: (b_, 0, 0, 0)), _const_spec((1, d)),
                         wspec(wg), wspec(wu), wspec(wd)],
        out_specs=pl.BlockSpec((None, tm, d), lambda b_, t: (b_, t, 0)),
        out_shape=jax.ShapeDtypeStruct((b, rows, d), F32),
        compiler_params=_params(),
        name="ffn_half_step",
    )(*data, mod, g.reshape(1, d), wg, wu, wd)


def _inproj_even_kernel(x_ref, mod_ref, g_ref, win_ref, qan_ref, wq_ref, kvan_ref, wkk_ref, wkv_ref,
                        qn_ref, kn_ref, rc_ref, rs1_ref, rs2_ref,
                        q_ref, k_ref, v_ref, zg_ref, gate_ref, small_ref):
    x = x_ref[...]
    m = mod_ref[...]
    h = _rms_rows(x, g_ref[...]) * (1.0 + m[4:5]) + m[3:4]
    hb = h.astype(BF16)
    n_mla = MLA_Q_RANK + MLA_KV_RANK + LANES
    cq = jnp.dot(hb, win_ref[:, 0:MLA_Q_RANK], preferred_element_type=F32)
    ckv = jnp.dot(hb, win_ref[:, MLA_Q_RANK:MLA_Q_RANK + MLA_KV_RANK], preferred_element_type=F32)
    kr = jnp.dot(hb, win_ref[:, MLA_Q_RANK + MLA_KV_RANK:n_mla], preferred_element_type=F32)

    rc, rs1, rs2 = rc_ref[...], rs1_ref[...], rs2_ref[...]
    qn, kn = qn_ref[...], kn_ref[...]
    qf = jnp.dot(_rms_rows(cq, qan_ref[...]).astype(BF16), wq_ref[...], preferred_element_type=F32)
    ckvn = _rms_rows(ckv, kvan_ref[...]).astype(BF16)
    kf = jnp.dot(ckvn, wkk_ref[...], preferred_element_type=F32)
    lane = lax.broadcasted_iota(jnp.int32, (1, MLA_HEADS * HEAD_PAD), 1)
    ones_cols = (lane % HEAD_PAD >= MLA_V).astype(F32)
    v_ref[...] = (jnp.dot(ckvn, wkv_ref[...], preferred_element_type=F32) + ones_cols).astype(v_ref.dtype)
    slabs = [slice(hd * HEAD_PAD, (hd + 1) * HEAD_PAD) for hd in range(MLA_HEADS)]
    xs = [qf[:, sl] for sl in slabs] + [kf[:, sl] + kr for sl in slabs]
    gains = [qn * MLA_Q_PRESCALE] * MLA_HEADS + [kn] * MLA_HEADS
    def gdn_cols(lo, hi):
        return jnp.dot(hb, win_ref[:, n_mla + lo:n_mla + hi], preferred_element_type=F32)

    ss = [jnp.sum(x * x, axis=-1, keepdims=True) for x in xs]
    zg_ref[:, 0:GDN_QK] = gdn_cols(0, GDN_QK)
    xs = [x * lax.rsqrt(s * (1.0 / MLA_QK) + EPS) * g for x, s, g in zip(xs, ss, gains)]
    zg_ref[:, GDN_QK:2 * GDN_QK] = gdn_cols(GDN_QK, 2 * GDN_QK)
    xs = [_rope(x, rc, rs1, rs2, MLA_ROPE // 2) for x in xs]
    zg_ref[:, 2 * GDN_QK:3 * GDN_QK] = gdn_cols(2 * GDN_QK, 3 * GDN_QK)
    for hd, sl in enumerate(slabs):
        q_ref[:, sl] = xs[hd].astype(q_ref.dtype)
        k_ref[:, sl] = xs[MLA_HEADS + hd].astype(k_ref.dtype)
    zr = gdn_cols(3 * GDN_QK, 3 * GDN_QK + GDN_VW + LANES)
    gate_ref[...] = zr[:, 0:GDN_VW]
    small_ref[...] = zr[:, GDN_VW:]


def _inproj_even(xs, mod, g, p, rope):
    b, rows, d = xs.shape
    ctx_tiles = p["ctx_tiles"]
    consts = [g.reshape(1, d), p["w_in"], p["q_a_norm"], p["w_q"], p["kv_a_norm"], p["w_kk"], p["w_kv"],
              p["q_norm"], p["k_norm"]]
    rope_spec = pl.BlockSpec((TM, LANES), lambda b_, t: (t, 0))
    widths = [(MLA_HEADS * HEAD_PAD, BF16), (MLA_HEADS * HEAD_PAD, BF16), (MLA_HEADS * HEAD_PAD, BF16),
              (3 * GDN_QK, F32), (GDN_VW, F32), (LANES, F32)]
    return pl.pallas_call(
        _inproj_even_kernel,
        grid=(b, rows // TM),
        in_specs=[_tok_spec(d), _mod_spec(ctx_tiles)] + [_const_spec(c.shape) for c in consts] + [rope_spec] * 3,
        out_specs=[_tok_spec(w) for w, _ in widths],
        out_shape=[jax.ShapeDtypeStruct((b, rows, w), dt) for w, dt in widths],
        compiler_params=_params(),
        name="inproj_even",
    )(xs, mod, *consts, *rope)


def _mla_kernel(q_ref, k_ref, v_ref, o_ref, *, n_ctx, n_all, ctx_tiles):
    t = pl.program_id(2)
    lo = lax.broadcasted_iota(jnp.int32, (TM, LANES), 1) < MLA_V
    n_heads = q_ref.shape[-1] // HEAD_PAD

    def attend(nk):
        q = q_ref[...]
        sls = [slice(hh * HEAD_PAD, (hh + 1) * HEAD_PAD) for hh in range(n_heads)]
        s = [lax.dot_general(q[:, sl], k_ref[0:nk, sl], (((1,), (1,)), ((), ())), preferred_element_type=F32)
             for sl in sls]
        mx = [jnp.max(x, axis=-1, keepdims=True) for x in s]
        p = [jnp.exp2(x - m).astype(BF16) for x, m in zip(s, mx)]
        r = [jnp.dot(x, v_ref[0:nk, sl], preferred_element_type=F32) for x, sl in zip(p, sls)]
        outs = [x / x[:, MLA_V:MLA_V + 1] for x in r]
        for pr in range(n_heads // 2):
            o_ref[:, pr * LANES:(pr + 1) * LANES] = jnp.where(
                lo, outs[2 * pr], pltpu.roll(outs[2 * pr + 1], MLA_V, 1)).astype(o_ref.dtype)

    @pl.when(t < ctx_tiles)
    def _():
        attend(n_ctx)

    @pl.when(t >= ctx_tiles)
    def _():
        attend(n_all)


def _mla_attention(q, k, v, *, n_ctx):
    b, rows, _ = q.shape
    hs = MLA_STEP_HEADS
    return pl.pallas_call(
        functools.partial(_mla_kernel, n_ctx=n_ctx, n_all=rows, ctx_tiles=n_ctx // TM),
        grid=(b, MLA_HEADS // hs, rows // TM),
        in_specs=[
            pl.BlockSpec((None, TM, hs * HEAD_PAD), lambda b_, h, t: (b_, t, h)),
            pl.BlockSpec((None, rows, hs * HEAD_PAD), lambda b_, h, t: (b_, 0, h)),
            pl.BlockSpec((None, rows, hs * HEAD_PAD), lambda b_, h, t: (b_, 0, h)),
        ],
        out_specs=pl.BlockSpec((None, TM, hs * MLA_V), lambda b_, h, t: (b_, t, h)),
        out_shape=jax.ShapeDtypeStruct((b, rows, MLA_VW), BF16),
        compiler_params=pltpu.CompilerParams(dimension_semantics=("parallel", "parallel", "arbitrary"),
                                             vmem_limit_bytes=VMEM_LIMIT),
        name="mla_attention",
    )(q, k, v)


def _gdn_prep_kernel(z_ref, zp_ref, zn_ref, cw_ref, sm_ref, nega_ref, dtb_ref,
                     q_ref, k_ref, v_ref, g_ref, ext_ref, *, ctx_tiles, n_tiles):
    t = pl.program_id(1)
    first = (t == 0) | (t == ctx_tiles)
    last = (t == ctx_tiles - 1) | (t == n_tiles - 1)
    half = GDN_CONV // 2
    ext_ref[0:8, :] = jnp.where(first, 0.0, zp_ref[...])
    ext_ref[8:8 + TM, :] = z_ref[...]
    ext_ref[8 + TM:16 + TM, :] = jnp.where(last, 0.0, zn_ref[...])
    cw = cw_ref[...]
    acc = z_ref[...] * cw[half:half + 1]
    for j in range(GDN_CONV):
        if j != half:
            acc = acc + ext_ref[8 - half + j:8 - half + j + TM, :] * cw[j:j + 1]
    qkv = _silu(acc)
    q = qkv[:, :GDN_QK]
    k = qkv[:, GDN_QK:2 * GDN_QK]
    q_ref[...] = q * lax.rsqrt(_seg_mean_sq(q, GDN_DK, 1.0) + EPS) * (GDN_DK ** -0.5)
    k_ref[...] = k * lax.rsqrt(_seg_mean_sq(k, GDN_DK, 1.0) + EPS)
    v_ref[...] = qkv[:, 2 * GDN_QK:]

    sm = sm_ref[...]
    lane = lax.broadcasted_iota(jnp.int32, sm.shape, 1)
    g = nega_ref[...] * _softplus(sm + dtb_ref[...])
    g = jnp.where(lane < 2 * GDN_HEADS, g, 0.0)
    gc_f = _dot_sel(_chunk_tri(TM, False), g)
    gc_b = _dot_sel(_chunk_tri(TM, True), g)
    g_ref[...] = jnp.where(lane < GDN_HEADS, gc_f, jnp.where(lane < 2 * GDN_HEADS, gc_b, _sigmoid(sm)))


def _gdn_prep(zg, small, conv_w, nega, dtb, *, ctx_tiles):
    b, rows, w = zg.shape
    n_tiles = rows // TM
    hb = TM // 8
    n_hblk = rows // 8
    return pl.pallas_call(
        functools.partial(_gdn_prep_kernel, ctx_tiles=ctx_tiles, n_tiles=n_tiles),
        grid=(b, n_tiles),
        in_specs=[
            _tok_spec(w),
            pl.BlockSpec((None, 8, w), lambda b_, t: (b_, jnp.maximum(t * hb - 1, 0), 0)),
            pl.BlockSpec((None, 8, w), lambda b_, t: (b_, jnp.minimum((t + 1) * hb, n_hblk - 1), 0)),
            _const_spec(conv_w.shape), _tok_spec(LANES), _const_spec((1, LANES)), _const_spec((1, LANES)),
        ],
        out_specs=[_tok_spec(GDN_QK), _tok_spec(GDN_QK), _tok_spec(GDN_VW), _tok_spec(LANES)],
        out_shape=[jax.ShapeDtypeStruct((b, rows, GDN_QK), F32), jax.ShapeDtypeStruct((b, rows, GDN_QK), F32),
                   jax.ShapeDtypeStruct((b, rows, GDN_VW), F32), jax.ShapeDtypeStruct((b, rows, LANES), F32)],
        scratch_shapes=[pltpu.VMEM((TM + 16, w), F32)],
        compiler_params=_params(),
        name="gdn_prep",
    )(zg, zg, zg, conv_w, small, nega, dtb)


def _bd_halves(y, bdm):
    yb = y.astype(BF16)
    out = []
    for s in range(GDN_TILES):
        t = jnp.concatenate([yb[:, s * GDN_HALF:(s + 1) * GDN_HALF]] * (GDN_HALF // GDN_DK), axis=0)
        out.append(jnp.where(bdm, t, jnp.zeros_like(t)))
    return out


def _hprod(x, bd):
    xb = x.astype(BF16)
    d = functools.partial(jnp.dot, preferred_element_type=F32)
    return jnp.concatenate([d(xb[:, s * GDN_HALF:(s + 1) * GDN_HALF], bd[s]) for s in range(GDN_TILES)], axis=1)


def _hprod3(x, y, bdm):
    xh, xl = _split2(x)
    yh, yl = _split2(y)
    bh, bl = _bd_halves(yh, bdm), _bd_halves(yl, bdm)
    return _hprod(xh, bh) + (_hprod(xh, bl) + _hprod(xl, bh))


def _gdn_chunks(chains):
    ii = lax.broadcasted_iota(jnp.int32, (CHUNK, GDN_QK), 0)
    jj = lax.broadcasted_iota(jnp.int32, (CHUNK, GDN_QK), 1) % CHUNK
    r = lax.broadcasted_iota(jnp.int32, (GDN_HALF, GDN_HALF), 0) // GDN_DK
    c = lax.broadcasted_iota(jnp.int32, (GDN_HALF, GDN_HALF), 1) // GDN_DK
    bdm = r == c
    diag = ii == jj
    eye = diag.astype(F32)
    n = len(chains)
    qs, ks, vs, gs, srefs, revs = (list(t) for t in zip(*chains))
    each = lambda f, *ls: [f(*a) for a in zip(*ls)]
    bd = lambda ys: [_bd_halves(y, bdm) for y in ys]

    incl = [(ii <= jj) if rv else (ii >= jj) for rv in revs]
    strict = [(ii < jj) if rv else (ii > jj) for rv in revs]
    last = [0 if rv else CHUNK - 1 for rv in revs]
    gc = [g[:, 0:GDN_QK] for g in gs]
    beta = [g[:, GDN_QK:2 * GDN_QK] for g in gs]
    gc_row = [jnp.sum(jnp.where(diag, x, 0.0), axis=0, keepdims=True) for x in gc]
    decay = each(lambda x, xr, ic: jnp.exp(jnp.where(ic, x - xr, -jnp.inf)), gc, gc_row, incl)

    def kt_bd(k):
        kt = k.T.astype(BF16)
        out = []
        for s in range(GDN_TILES):
            t = jnp.concatenate([kt[s * GDN_HALF:(s + 1) * GDN_HALF]] * (GDN_HALF // CHUNK), axis=1)
            out.append(jnp.where(bdm, t, jnp.zeros_like(t)))
        return out

    kkqk = each(lambda k, q, rk: _hprod(jnp.concatenate([k, q], axis=0), rk), ks, qs, [kt_bd(k) for k in ks])
    a = each(lambda st, b, x, dc: jnp.where(st, b * x[0:CHUNK] * dc, 0.0), strict, beta, kkqk, decay)
    qk = each(lambda x, dc: x[CHUNK:2 * CHUNK] * dc, kkqk, decay)

    def level_mask(blk, rv):
        same = (ii // (2 * blk)) == (jj // (2 * blk))
        off = (ii // blk) < (jj // blk) if rv else (ii // blk) > (jj // blk)
        return same & off

    m = each(lambda x, rv: eye - jnp.where(level_mask(1, rv), x, 0.0), a, revs)
    blk = 2
    while blk < CHUNK:
        am = each(lambda x, rv: jnp.where(level_mask(blk, rv), x, 0.0), a, revs)
        x = each(_hprod, am, bd(m))
        m = each(lambda mm, y, ybd: mm - _hprod(mm, ybd), m, x, bd(x))
        blk *= 2
    am = each(lambda x, mm: _hprod3(x, mm, bdm), a, m)
    resid = each(lambda mm, y: eye - mm - y, m, am)
    m = each(lambda mm, rbd: mm + _hprod(mm, rbd), m, bd(resid))

    egc = [jnp.exp(x) for x in gc]
    u = each(_hprod, m, bd(each(lambda v, b: v * b, vs, beta)))
    w = each(_hprod, m, bd(each(lambda k, b, e: k * (b * e), ks, beta, egc)))
    g_last = each(lambda x, l: x[l:l + 1, :], gc, last)
    st = [[sr[s] for s in range(GDN_TILES)] for sr in srefs]

    def sprod(x, tiles):
        return jnp.concatenate([_dot(x[:, s * GDN_HALF:(s + 1) * GDN_HALF], tiles[s]) for s in range(GDN_TILES)],
                               axis=1)

    ws = each(lambda ww, q, e, tiles: sprod(jnp.concatenate([ww, q * e], axis=0), tiles), w, qs, egc, st)
    v_new = each(lambda uu, x: uu - x[0:CHUNK], u, ws)
    o = each(lambda x, y, vbd: x[CHUNK:2 * CHUNK] + _hprod(y, vbd), ws, qk, bd(v_new))
    k_dec = each(lambda k, gl, x: k * jnp.exp(gl - x), ks, g_last, gc)
    e_last = [jnp.exp(gl) for gl in g_last]
    for i in range(n):
        for s in range(GDN_TILES):
            sl = slice(s * GDN_HALF, (s + 1) * GDN_HALF)
            srefs[i][s] = st[i][s] * e_last[i][:, sl] + jnp.where(
                bdm, _dot_tn(k_dec[i][:, sl], v_new[i][:, sl]), 0.0)
    return o


def _gdn_scan_kernel(qf_ref, kf_ref, vf_ref, gf_ref, qb_ref, kb_ref, vb_ref, gb_ref, of_ref, ob_ref, sf_ref, sb_ref):
    @pl.when(pl.program_id(1) == 0)
    def _():
        sf_ref[...] = jnp.zeros_like(sf_ref)
        sb_ref[...] = jnp.zeros_like(sb_ref)

    nb = qf_ref.shape[0]
    row = lax.broadcasted_iota(jnp.int32, (LANES, 2 * GDN_QK), 0)
    col = lax.broadcasted_iota(jnp.int32, (LANES, 2 * GDN_QK), 1)
    src = (col % GDN_QK) // GDN_DK + jnp.where(col >= GDN_QK, 2 * GDN_HEADS, 0)

    def spread(ref, d):
        stack = jnp.concatenate([ref[i] for i in range(nb)], axis=0)
        return _dot_sel_r(stack, (row == src + d * GDN_HEADS).astype(F32))

    gf, gb = spread(gf_ref, 0), spread(gb_ref, 1)
    chains = []
    for i in range(nb):
        rows = slice(i * CHUNK, (i + 1) * CHUNK)
        chains.append((qf_ref[i], kf_ref[i], vf_ref[i], gf[rows], sf_ref.at[i], False))
        chains.append((qb_ref[i], kb_ref[i], vb_ref[i], gb[rows], sb_ref.at[i], True))
    outs = _gdn_chunks(chains)
    for i in range(nb):
        of_ref[i] = outs[2 * i]
        ob_ref[i] = outs[2 * i + 1]


def _scan_chunk_maps(ctx_chunks, n_chunks):
    fwd = lambda s: s
    bwd = lambda s: jnp.where(s < ctx_chunks, ctx_chunks - 1 - s, n_chunks - 1 - (s - ctx_chunks))
    return fwd, bwd


def _gdn_scan(q, k, v, gates, *, n_ctx):
    b, rows, _ = q.shape
    n_chunks = rows // CHUNK
    fwd, bwd = _scan_chunk_maps(n_ctx // CHUNK, n_chunks)

    nb = GDN_SCAN_SAMPLES if b % GDN_SCAN_SAMPLES == 0 else 1

    def specs(cm):
        tok = lambda w: pl.BlockSpec((nb, CHUNK, w), lambda b_, s: (b_, cm(s), 0))
        return [tok(GDN_QK), tok(GDN_QK), tok(GDN_VW), tok(LANES)]

    return pl.pallas_call(
        _gdn_scan_kernel,
        grid=(b // nb, n_chunks),
        in_specs=specs(fwd) + specs(bwd),
        out_specs=[pl.BlockSpec((nb, CHUNK, GDN_VW), lambda b_, s: (b_, fwd(s), 0)),
                   pl.BlockSpec((nb, CHUNK, GDN_VW), lambda b_, s: (b_, bwd(s), 0))],
        out_shape=[jax.ShapeDtypeStruct((b, rows, GDN_VW), F32)] * 2,
        scratch_shapes=[pltpu.VMEM((nb, GDN_TILES, GDN_HALF, GDN_HALF), F32)] * 2,
        compiler_params=_params(),
        name="gdn_scan",
    )(q, k, v, gates, q, k, v, gates)


def _outproj_ffn_kernel(x_ref, mod_ref, oa_ref, of_ref, ob_ref, gate_ref, gn_ref, w_ref, g_ref, wg_ref, wu_ref, wd_ref,
                        o_ref, *, seg, rec_first):
    m = mod_ref[...]
    row = lambda r: m[r:r + 1]
    o = of_ref[...] + ob_ref[...]
    y = o * lax.rsqrt(_seg_mean_sq(o, seg, float(seg)) + EPS) * gn_ref[...] * _silu(gate_ref[...])
    wr = y.shape[-1]
    wa = oa_ref.shape[-1]
    if rec_first:
        out = jnp.dot(y.astype(BF16), w_ref[0:wr, :], preferred_element_type=F32)
        out = out + jnp.dot(oa_ref[...], w_ref[wr:wr + wa, :], preferred_element_type=F32)
    else:
        out = jnp.dot(oa_ref[...], w_ref[0:wa, :], preferred_element_type=F32)
        out = out + jnp.dot(y.astype(BF16), w_ref[wa:wa + wr, :], preferred_element_type=F32)
    x1 = x_ref[...] + row(5) * out
    hb = _rms_rows(x1, g_ref[...]) * (1.0 + row(7)) + row(6)
    o_ref[...] = x1 + (FFN_RES * row(8)) * _swiglu(hb, wg_ref, wu_ref, wd_ref)


def _outproj_ffn(xs, mod, o_att, o_f, o_b, gate, gn, w_out, g, wg, wu, wd, *, layer, seg, rec_first, n_ctx,
                 latent_only):
    b, rows, d = xs.shape
    tm = TM
    ctx_tiles = n_ctx // tm
    skip = ctx_tiles if latent_only else 0
    out_rows = rows - skip * tm
    tok = lambda w, sk: pl.BlockSpec((None, tm, w), lambda b_, t: (b_, t + sk, 0))
    wspec = lambda w: pl.BlockSpec((None, None) + w.shape[2:], lambda b_, t: (layer, 1, 0, 0),
                                   pipeline_mode=pl.Buffered(1))
    return pl.pallas_call(
        functools.partial(_outproj_ffn_kernel, seg=seg, rec_first=rec_first),
        grid=(b, out_rows // tm),
        in_specs=[tok(d, skip),
                  pl.BlockSpec((None, None, N_MOD, d), lambda b_, t: (b_, jnp.where(t + skip >= ctx_tiles, 1, 0), 0, 0)),
                  tok(o_att.shape[-1], 0), tok(o_f.shape[-1], skip), tok(o_b.shape[-1], skip),
                  tok(gate.shape[-1], skip), _const_spec(gn.shape), _const_spec(w_out.shape), _const_spec((1, d)),
                  wspec(wg), wspec(wu), wspec(wd)],
        out_specs=tok(d, 0),
        out_shape=jax.ShapeDtypeStruct((b, out_rows, d), F32),
        compiler_params=_params(),
        name="outproj_ffn",
    )(xs, mod, o_att, o_f, o_b, gate, gn, w_out, g.reshape(1, d), wg, wu, wd)


def _inproj_odd_kernel(x_ref, mod_ref, g_ref, win_ref, w2_ref, b2_ref, qn_ref, kn_ref, rc_ref, rs1_ref, rs2_ref,
                       gq_ref, gk_ref, gv_ref, rg_ref, bc_ref, sq_ref, sk_ref, sv_ref):
    x = x_ref[...]
    m = mod_ref[...]
    h = _rms_rows(x, g_ref[...]) * (1.0 + m[4:5]) + m[3:4]
    hb = h.astype(BF16)
    n_pass = 2 * GLA_QK + 2 * GLA_VW
    z = jnp.dot(hb, win_ref[:, n_pass:], preferred_element_type=F32)
    o = 0
    sq = z[:, o:o + SWA_QW]; o += SWA_QW
    sk = z[:, o:o + SWA_KW]; o += SWA_KW
    sv_ref[...] = z[:, o:o + SWA_KW].astype(sv_ref.dtype); o += SWA_KW
    lowrank = z[:, o:o + LANES]
    zp = jnp.dot(hb, win_ref[:, 0:n_pass], preferred_element_type=F32)
    o = 0
    gq_ref[...] = zp[:, o:o + GLA_QK]; o += GLA_QK
    gk_ref[...] = zp[:, o:o + GLA_QK]; o += GLA_QK
    gv_ref[...] = zp[:, o:o + GLA_VW]; o += GLA_VW
    rg_ref[...] = zp[:, o:o + GLA_VW]

    logit = _dot(lowrank, w2_ref[...]) + b2_ref[...]
    log_a = (jnp.minimum(logit, 0.0) - jnp.log1p(jnp.exp(-jnp.abs(logit)))) * (1.0 / GLA_TAU)
    bc_ref[:, 0:GLA_QK] = _dot_sel(_chunk_tri(TM, False), log_a[:, 0:GLA_QK])
    bc_ref[:, GLA_QK:2 * GLA_QK] = _dot_sel(_chunk_tri(TM, True), log_a[:, GLA_QK:2 * GLA_QK])

    rc, rs1, rs2 = rc_ref[...], rs1_ref[...], rs2_ref[...]
    sqn = sq * lax.rsqrt(_seg_mean_sq(sq, SWA_DH, float(SWA_DH)) + EPS) * (qn_ref[...] * SWA_Q_PRESCALE)
    for s in range(SWA_QW // LANES):
        sl = slice(s * LANES, (s + 1) * LANES)
        sq_ref[:, sl] = _rope(sqn[:, sl], rc, rs1, rs2, SWA_DH // 2).astype(sq_ref.dtype)
    skn = sk * lax.rsqrt(_seg_mean_sq(sk, SWA_DH, float(SWA_DH)) + EPS) * kn_ref[...]
    sk_ref[...] = _rope(skn, rc, rs1, rs2, SWA_DH // 2).astype(sk_ref.dtype)


def _inproj_odd(xs, mod, g, p, rope):
    b, rows, d = xs.shape
    consts = [g.reshape(1, d), p["w_in"], p["w2"], p["b2"], p["q_norm"], p["k_norm"]]
    rope_spec = pl.BlockSpec((TM, LANES), lambda b_, t: (t, 0))
    widths = [(GLA_QK, F32), (GLA_QK, F32), (GLA_VW, F32), (GLA_VW, F32), (2 * GLA_QK, F32),
              (SWA_QW, BF16), (SWA_KW, BF16), (SWA_KW, BF16)]
    return pl.pallas_call(
        _inproj_odd_kernel,
        grid=(b, rows // TM),
        in_specs=[_tok_spec(d), _mod_spec(p["ctx_tiles"])] + [_const_spec(c.shape) for c in consts] + [rope_spec] * 3,
        out_specs=[_tok_spec(w) for w, _ in widths],
        out_shape=[jax.ShapeDtypeStruct((b, rows, w), dt) for w, dt in widths],
        compiler_params=_params(),
        name="inproj_odd",
    )(xs, mod, *consts, *rope)


def _gla_chunk(q, k, v, bc, st_ref, reverse):
    def blk(shape, rdiv, cdiv):
        return (lax.broadcasted_iota(jnp.int32, shape, 0) // rdiv) == (lax.broadcasted_iota(jnp.int32, shape, 1) // cdiv)

    ii = lax.broadcasted_iota(jnp.int32, (CHUNK, GLA_QK), 0)
    jj = lax.broadcasted_iota(jnp.int32, (CHUNK, GLA_QK), 1) % CHUNK
    incl = (ii <= jj) if reverse else (ii >= jj)
    last = 0 if reverse else CHUNK - 1
    q_dec = (q * (GLA_DK ** -0.5)) * jnp.exp(bc)
    k_inv = k * jnp.exp(-bc)
    b_last = bc[last:last + 1, :]
    k_dec = k * jnp.exp(b_last - bc)
    e_last = jnp.exp(b_last)
    kt = jnp.concatenate([k_inv.T.astype(BF16)] * GLA_HEADS, axis=1)
    rk = jnp.where(blk((GLA_QK, GLA_QK), GLA_DK, CHUNK), kt, jnp.zeros_like(kt))
    attn = jnp.where(incl, jnp.dot(q_dec.astype(BF16), rk, preferred_element_type=F32), 0.0)
    vt = jnp.concatenate([v.astype(BF16)] * GLA_HEADS, axis=0)
    vbd = jnp.where(blk((GLA_QK, GLA_VW), CHUNK, GLA_DV), vt, jnp.zeros_like(vt))
    st = st_ref[...]
    o = jnp.dot(attn.astype(BF16), vbd, preferred_element_type=F32) + _dot_nt(q_dec, st)
    st_ref[...] = st * e_last + jnp.where(blk((GLA_VW, GLA_QK), GLA_DV, GLA_DK), _dot_tn(v, k_dec), 0.0)
    return o


def _gla_scan_kernel(qf_ref, kf_ref, vf_ref, bf_ref, qb_ref, kb_ref, vb_ref, bb_ref, of_ref, ob_ref, sf_ref, sb_ref):
    @pl.when(pl.program_id(1) == 0)
    def _():
        sf_ref[...] = jnp.zeros_like(sf_ref)
        sb_ref[...] = jnp.zeros_like(sb_ref)

    for i in range(qf_ref.shape[0]):
        of_ref[i] = _gla_chunk(qf_ref[i], kf_ref[i], vf_ref[i], bf_ref[i][:, 0:GLA_QK], sf_ref.at[i], False)
        ob_ref[i] = _gla_chunk(qb_ref[i], kb_ref[i], vb_ref[i], bb_ref[i][:, GLA_QK:2 * GLA_QK], sb_ref.at[i], True)


def _gla_scan(q, k, v, bc, *, n_ctx):
    b, rows, _ = q.shape
    n_chunks = rows // CHUNK
    fwd, bwd = _scan_chunk_maps(n_ctx // CHUNK, n_chunks)
    nb = GLA_SCAN_SAMPLES if b % GLA_SCAN_SAMPLES == 0 else 1

    def specs(cm):
        tok = lambda w: pl.BlockSpec((nb, CHUNK, w), lambda b_, s: (b_, cm(s), 0))
        return [tok(GLA_QK), tok(GLA_QK), tok(GLA_VW), tok(2 * GLA_QK)]

    return pl.pallas_call(
        _gla_scan_kernel,
        grid=(b // nb, n_chunks),
        in_specs=specs(fwd) + specs(bwd),
        out_specs=[pl.BlockSpec((nb, CHUNK, GLA_VW), lambda b_, s: (b_, fwd(s), 0)),
                   pl.BlockSpec((nb, CHUNK, GLA_VW), lambda b_, s: (b_, bwd(s), 0))],
        out_shape=[jax.ShapeDtypeStruct((b, rows, GLA_VW), F32)] * 2,
        scratch_shapes=[pltpu.VMEM((nb, GLA_VW, GLA_QK), F32)] * 2,
        compiler_params=_params(),
        name="gla_scan",
    )(q, k, v, bc, q, k, v, bc)


def _swa_kernel(q_ref, k_ref, v_ref, sink_ref, o_ref, *, n_ctx, n_lat):
    w = SWA_WINDOW
    n = pl.program_id(1)
    nb = n_lat // w

    def rows(ref, blk):
        return ref[pl.ds(pl.multiple_of(n_ctx + blk * w, w), w), :]

    pb = jnp.maximum(n - 1, 0)
    xb = jnp.minimum(n + 1, nb - 1)
    k_cat = jnp.concatenate([rows(k_ref, pb), rows(k_ref, n), rows(k_ref, xb), k_ref[0:n_ctx, :]], axis=0)
    v_cat = jnp.concatenate([rows(v_ref, pb), rows(v_ref, n), rows(v_ref, xb), v_ref[0:n_ctx, :]], axis=0)
    v_one = jnp.concatenate([v_cat, jnp.ones_like(v_cat)], axis=1)
    half = SWA_HEADS // 2
    ii = lax.broadcasted_iota(jnp.int32, (half * w, w), 0) % w
    jj = lax.broadcasted_iota(jnp.int32, (half * w, w), 1)
    ninf = -jnp.inf
    bias_prev = jnp.where((jj >= ii) & (n > 0), 0.0, ninf)
    bias_next = jnp.where((jj <= ii) & (n < nb - 1), 0.0, ninf)
    lo = lax.broadcasted_iota(jnp.int32, (half * w, LANES), 1) < SWA_DH
    q = q_ref[...]
    qst = jnp.concatenate([q[:, j * LANES:(j + 1) * LANES] for j in range(half)], axis=0)
    sink = sink_ref[...]
    dims = (((1,), (1,)), ((), ()))
    qm = [jnp.where(lo if g == 0 else jnp.logical_not(lo), qst, jnp.zeros_like(qst)) for g in range(SWA_KV_HEADS)]
    s = [lax.dot_general(x, k_cat, dims, preferred_element_type=F32) for x in qm]
    parts = [[x[:, 0:w] + bias_prev, x[:, w:2 * w], x[:, 2 * w:3 * w] + bias_next, x[:, 3 * w:]] for x in s]
    sk = [jnp.concatenate([jnp.broadcast_to(sink[0:1, half * g + j:half * g + j + 1], (w, 1)) for j in range(half)],
                          axis=0) for g in range(SWA_KV_HEADS)]
    def row_max(ps, k_):
        slabs = [x[:, c * LANES:(c + 1) * LANES] for x in ps for c in range(x.shape[1] // LANES)]
        return jnp.maximum(jnp.max(functools.reduce(jnp.maximum, slabs), axis=-1, keepdims=True), k_)

    mx = [row_max(ps, k_) for ps, k_ in zip(parts, sk)]
    p = [jnp.concatenate([jnp.exp2(x - m).astype(BF16) for x in ps], axis=1) for ps, m in zip(parts, mx)]
    r = [jnp.dot(x, v_one, preferred_element_type=F32) for x in p]
    res = [x[:, 0:LANES] / (x[:, LANES:2 * LANES] + jnp.exp2(k_ - m)) for x, k_, m in zip(r, sk, mx)]
    o = jnp.where(lo, res[0], res[1]).astype(o_ref.dtype)
    for j in range(half):
        o_ref[:, j * LANES:(j + 1) * LANES] = o[j * w:(j + 1) * w, :]


def _swa_attention(q, k, v, sink, *, n_ctx):
    b, rows, _ = q.shape
    n_lat = rows - n_ctx
    w = SWA_WINDOW
    skip = n_ctx // w
    return pl.pallas_call(
        functools.partial(_swa_kernel, n_ctx=n_ctx, n_lat=n_lat),
        grid=(b, n_lat // w),
        in_specs=[
            pl.BlockSpec((None, w, SWA_QW), lambda b_, n: (b_, n + skip, 0)),
            pl.BlockSpec((None, rows, SWA_KW), lambda b_, n: (b_, 0, 0)),
            pl.BlockSpec((None, rows, SWA_KW), lambda b_, n: (b_, 0, 0)),
            _const_spec((1, LANES)),
        ],
        out_specs=pl.BlockSpec((None, w, SWA_QW), lambda b_, n: (b_, n, 0)),
        out_shape=jax.ShapeDtypeStruct((b, n_lat, SWA_QW), BF16),
        compiler_params=_params(),
        name="swa_attention",
    )(q, k, v, sink)


def _rope_tables(n_lat, n_ctx, rot_dim):
    t = jnp.arange(n_lat)
    row = (t // GRID_W).astype(F32)
    col = (t % GRID_W).astype(F32)
    n_freq = rot_dim // 4
    inv = ROPE_THETA ** (-jnp.arange(n_freq, dtype=F32) / n_freq)
    ang = jnp.concatenate([row[:, None] * inv, col[:, None] * inv], axis=-1)
    half = rot_dim // 2
    cos = jnp.concatenate([jnp.ones((n_ctx, half), F32), jnp.cos(ang)], axis=0)
    sin = jnp.concatenate([jnp.zeros((n_ctx, half), F32), jnp.sin(ang)], axis=0)
    rows = n_ctx + n_lat
    one = lambda w: jnp.ones((rows, w), F32)
    zero = lambda w: jnp.zeros((rows, w), F32)
    if rot_dim == MLA_ROPE:
        c = jnp.concatenate([one(MLA_NOPE), cos, cos, one(HEAD_PAD - MLA_QK)], axis=1)
        s1 = jnp.concatenate([zero(MLA_NOPE), -sin, zero(half), zero(HEAD_PAD - MLA_QK)], axis=1)
        s2 = jnp.concatenate([zero(MLA_NOPE), zero(half), sin, zero(HEAD_PAD - MLA_QK)], axis=1)
    else:
        c = jnp.concatenate([cos, cos, cos, cos], axis=1)
        s1 = jnp.concatenate([-sin, zero(half), -sin, zero(half)], axis=1)
        s2 = jnp.concatenate([zero(half), sin, zero(half), sin], axis=1)
    return c, s1, s2


def _even_params(j, n_ctx, ev_w_in, ev_q_a_norm, ev_w_q_up, ev_kv_a_norm, ev_w_kv_up, ev_mla_q_norm, ev_mla_k_norm,
                 ev_gdn_conv, ev_gdn_a_log, ev_gdn_dt_bias, ev_gdn_out_norm, ev_w_out):
    w = ev_w_in[j]
    d = w.shape[0]
    z = lambda n: jnp.zeros((d, n), F32)
    o_kr = MLA_Q_RANK + MLA_KV_RANK
    o_g = o_kr + MLA_ROPE
    o_small = o_g + 3 * GDN_QK
    o_gate = o_small + 4 * GDN_HEADS
    w_in = jnp.concatenate([
        w[:, :o_kr], z(MLA_NOPE), w[:, o_kr:o_g], z(HEAD_PAD - MLA_QK),
        w[:, o_g:o_small], w[:, o_gate:o_gate + GDN_VW],
        w[:, o_small:o_gate], z(LANES - 4 * GDN_HEADS)], axis=1).astype(BF16)
    pad_h = HEAD_PAD - MLA_QK
    w_q = jnp.pad(ev_w_q_up[j].reshape(MLA_Q_RANK, MLA_HEADS, MLA_QK), ((0, 0), (0, 0), (0, pad_h)))
    wkv = ev_w_kv_up[j].reshape(MLA_KV_RANK, MLA_HEADS, MLA_NOPE + MLA_V)
    w_kk = jnp.pad(wkv[:, :, :MLA_NOPE], ((0, 0), (0, 0), (0, HEAD_PAD - MLA_NOPE)))
    lane_row = lambda vec: jnp.pad(vec, (0, LANES - vec.shape[0])).reshape(1, LANES)
    fb = lambda a: jnp.concatenate([a[0], a[1]])
    return {
        "ctx_tiles": n_ctx // TM,
        "w_in": w_in,
        "q_a_norm": ev_q_a_norm[j].reshape(1, -1),
        "w_q": w_q.reshape(MLA_Q_RANK, MLA_HEADS * HEAD_PAD).astype(BF16),
        "kv_a_norm": ev_kv_a_norm[j].reshape(1, -1),
        "w_kk": w_kk.reshape(MLA_KV_RANK, MLA_HEADS * HEAD_PAD).astype(BF16),
        "w_kv": jnp.pad(wkv[:, :, MLA_NOPE:], ((0, 0), (0, 0), (0, HEAD_PAD - MLA_V))
                        ).reshape(MLA_KV_RANK, MLA_HEADS * HEAD_PAD).astype(BF16),
        "q_norm": lane_row(ev_mla_q_norm[j]),
        "k_norm": lane_row(ev_mla_k_norm[j]),
        "conv_w": ev_gdn_conv[j],
        "neg_a": lane_row(-jnp.exp(fb(ev_gdn_a_log[j]))),
        "dt_bias": lane_row(fb(ev_gdn_dt_bias[j])),
        "out_norm": jnp.tile(ev_gdn_out_norm[j], GDN_HEADS).reshape(1, GDN_VW),
        "w_out": ev_w_out[j].astype(BF16),
    }


def _swa_head_perm():
    half = SWA_HEADS // 2
    heads = [h for j in range(half) for h in (j, half + j)]
    return jnp.concatenate([jnp.arange(SWA_DH) + h * SWA_DH for h in heads])


def _odd_params(j, n_ctx, od_w_in, od_gla_gate_w2, od_gla_gate_b, od_gla_out_norm, od_swa_q_norm, od_swa_k_norm,
                od_swa_sink, od_w_out):
    w = od_w_in[j]
    d = w.shape[0]
    o_gate = 2 * GLA_QK + GLA_VW
    o_rg = o_gate + 2 * GLA_RANK
    o_sq = o_rg + GLA_VW
    o_sk = o_sq + SWA_QW
    perm = _swa_head_perm()
    w_in = jnp.concatenate([
        w[:, :o_gate], w[:, o_rg:o_sq], w[:, o_sq:o_sk][:, perm], w[:, o_sk:],
        w[:, o_gate:o_rg], jnp.zeros((d, LANES - 2 * GLA_RANK), F32)], axis=1).astype(BF16)
    w2 = jnp.zeros((LANES, 2 * GLA_QK), F32)
    w2 = w2.at[0:GLA_RANK, 0:GLA_QK].set(od_gla_gate_w2[j, 0])
    w2 = w2.at[GLA_RANK:2 * GLA_RANK, GLA_QK:].set(od_gla_gate_w2[j, 1])
    wo = od_w_out[j]
    w_out = jnp.concatenate([wo[:GLA_VW], wo[GLA_VW:][perm]], axis=0).astype(BF16)
    return {
        "ctx_tiles": n_ctx // TM,
        "w_in": w_in,
        "w2": w2.astype(BF16),
        "b2": jnp.concatenate([od_gla_gate_b[j, 0], od_gla_gate_b[j, 1]]).reshape(1, 2 * GLA_QK),
        "q_norm": jnp.tile(od_swa_q_norm[j], SWA_HEADS).reshape(1, SWA_QW),
        "k_norm": jnp.tile(od_swa_k_norm[j], SWA_KV_HEADS).reshape(1, SWA_KW),
        "sink": jnp.pad(od_swa_sink[j] * math.log2(math.e), (0, LANES - SWA_HEADS)).reshape(1, LANES),
        "out_norm": jnp.tile(od_gla_out_norm[j], GLA_HEADS).reshape(1, GLA_VW),
        "w_out": w_out,
    }


def kernel(x, c, ctx, c_ctx, ada_w, ada_b, norm_g, ffn_w_gate, ffn_w_up, ffn_w_down, ev_w_in, ev_q_a_norm, ev_w_q_up, ev_kv_a_norm, ev_w_kv_up, ev_mla_q_norm, ev_mla_k_norm, ev_gdn_conv, ev_gdn_a_log, ev_gdn_dt_bias, ev_gdn_out_norm, ev_w_out, od_w_in, od_gla_gate_w2, od_gla_gate_b, od_gla_out_norm, od_swa_q_norm, od_swa_k_norm, od_swa_sink, od_w_out):
    b, n_lat, d = x.shape
    n_ctx = ctx.shape[1]
    depth = ada_w.shape[0]
    assert d == D_MODEL and n_ctx % TM == 0 and n_lat % TM == 0 and n_lat % GRID_W == 0
    assert depth % 2 == 0, "the last layer must be an odd (GLA/SWA) layer: context outputs of that mixer are not built"
    ctx_tiles = n_ctx // TM

    cvec = jnp.concatenate([c, c_ctx[None, :], jnp.zeros((16 - b - 1, d), F32)], axis=0)
    mod_all = _ada(cvec, ada_w, ada_b).reshape(depth, 16, N_MOD, d)
    rope_mla = _rope_tables(n_lat, n_ctx, MLA_ROPE)
    rope_swa = _rope_tables(n_lat, n_ctx, SWA_DH)

    wg, wu, wd = ffn_w_gate, ffn_w_up, ffn_w_down
    xs = x
    for i in range(depth):
        last = i == depth - 1
        mod = jnp.stack([jnp.broadcast_to(mod_all[i, b][None], (b, N_MOD, d)), mod_all[i, :b]], axis=1)
        xs = _ffn(xs, mod, norm_g[i, 0], wg, wu, wd, layer=i, half=0, r0=0, n_ctx=n_ctx, ctx=ctx if i == 0 else None)
        j = i // 2
        if i % 2 == 0:
            p = _even_params(j, n_ctx, ev_w_in, ev_q_a_norm, ev_w_q_up, ev_kv_a_norm, ev_w_kv_up, ev_mla_q_norm,
                             ev_mla_k_norm, ev_gdn_conv, ev_gdn_a_log, ev_gdn_dt_bias, ev_gdn_out_norm, ev_w_out)
            q, k, v, zg, gate, small = _inproj_even(xs, mod, norm_g[i, 1], p, rope_mla)
            o_att = _mla_attention(q, k, v, n_ctx=n_ctx)
            gq, gk, gv, gates = _gdn_prep(zg, small, p["conv_w"], p["neg_a"], p["dt_bias"], ctx_tiles=ctx_tiles)
            o_f, o_b = _gdn_scan(gq, gk, gv, gates, n_ctx=n_ctx)
            assert not last, "an even last layer would hand a joint o_att to the latent-only output stage"
            mix = dict(o_att=o_att, o_f=o_f, o_b=o_b, gate=gate, seg=GDN_DV, rec_first=False)
        else:
            assert last, "odd layers that must also produce context outputs are not built"
            p = _odd_params(j, n_ctx, od_w_in, od_gla_gate_w2, od_gla_gate_b, od_gla_out_norm, od_swa_q_norm,
                            od_swa_k_norm, od_swa_sink, od_w_out)
            gq, gk, gv, rg, bc, sq, sk, sv = _inproj_odd(xs, mod, norm_g[i, 1], p, rope_swa)
            o_f, o_b = _gla_scan(gq, gk, gv, bc, n_ctx=n_ctx)
            o_att = _swa_attention(sq, sk, sv, p["sink"], n_ctx=n_ctx)
            mix = dict(o_att=o_att, o_f=o_f, o_b=o_b, gate=rg, seg=GLA_DV, rec_first=True)
        xs = _outproj_ffn(xs, mod, mix["o_att"], mix["o_f"], mix["o_b"], mix["gate"], p["out_norm"], p["w_out"],
                          norm_g[i, 2], wg, wu, wd, layer=i, seg=mix["seg"], rec_first=mix["rec_first"],
                          n_ctx=n_ctx, latent_only=last)
    return xs
```
